```python
import math
import jax, jax.numpy as jnp
from jax import lax
import numpy as np

D_MODEL = 2048
BATCH = 1
SEQ = 8192
DEPTH = 2
DEC_BATCH = 128
DEC_SEQ = 8
PAST_LEN = 16384
PAGE_SIZE = 128

A_WIDTH = D_MODEL // 2
A_GROUPS = 8
A_GROUP_DIM = A_WIDTH // A_GROUPS
A_CHUNK = 128
B_HEADS = 8
Q_LORA = 512
KV_LORA = 512
NOPE_DIM = 128
ROPE_DIM = 64
V_DIM = 128
ROPE_THETA = 10000.0
Q_BLOCK = 128
MLA_SCALE = (NOPE_DIM + ROPE_DIM) ** -0.5
OFF_V = A_WIDTH
OFF_Q = 2 * A_WIDTH
OFF_KV = OFF_Q + Q_LORA
OFF_PE = OFF_KV + KV_LORA
N_IN_AB = OFF_PE + ROPE_DIM
N_MIX_AB = A_WIDTH + B_HEADS * V_DIM
C_HEADS = 16
C_K = 128
C_V = D_MODEL // C_HEADS
C_FDIM = C_HEADS * C_K
C_CHUNK = 64
N_IN_C = 2 * C_FDIM + 2 * D_MODEL
D_FF = 5632
N_EXPERTS = 8
TOP_K = 2
D_FF_EXPERT = 7168
EPS = 1e-6

kernel_name = "hybrid_gmlp_mla_hgrn2_moe_step"


def rmsnorm(x, w):
    xf = x.astype(jnp.float32)
    y = xf * lax.rsqrt(jnp.mean(xf * xf, axis=-1, keepdims=True) + EPS)
    return (y * w.astype(jnp.float32)).astype(x.dtype)


def layernorm(x, w, b):
    xf = x.astype(jnp.float32)
    mu = jnp.mean(xf, axis=-1, keepdims=True)
    var = jnp.mean(jnp.square(xf - mu), axis=-1, keepdims=True)
    y = (xf - mu) * lax.rsqrt(var + EPS) * w.astype(jnp.float32) + b.astype(jnp.float32)
    return y.astype(x.dtype)


def rope(x, pos):
    half = ROPE_DIM // 2
    inv_freq = ROPE_THETA ** (-jnp.arange(half, dtype=jnp.float32) / half)
    ang = pos.astype(jnp.float32)[:, None] * inv_freq[None, :]
    cos = jnp.cos(ang)[None, :, None, :]
    sin = jnp.sin(ang)[None, :, None, :]
    xf = x.astype(jnp.float32)
    x1, x2 = xf[..., :half], xf[..., half:]
    return jnp.concatenate([x1 * cos - x2 * sin, x1 * sin + x2 * cos], axis=-1).astype(x.dtype)


def swiglu(h, w_gate, w_up, w_down):
    return (jax.nn.silu(h @ w_gate) * (h @ w_up)) @ w_down


def spatial_gating(u, v, w_s, b_s):
    B, L, _ = v.shape
    n = -(-L // A_CHUNK)
    vp = jnp.pad(v, ((0, 0), (0, n * A_CHUNK - L), (0, 0)))
    vp = vp.reshape(B, n, A_CHUNK, A_GROUPS, A_GROUP_DIM)
    causal = jnp.tril(jnp.ones((A_CHUNK, A_CHUNK), dtype=bool))
    ws = jnp.where(causal[None], w_s, 0.0)
    mixed = jnp.einsum('gts,bnsgc->bntgc', ws, vp) + b_s.T[None, None, :, :, None]
    mixed = mixed.reshape(B, n * A_CHUNK, A_WIDTH)[:, :L]
    return u * mixed


def ab_project(h, pos, w_in_ab, ln_v_w, ln_v_b, q_norm_w, w_q_b, kv_norm_w, w_kv_b):
    z = h @ w_in_ab
    u = jax.nn.gelu(z[..., :OFF_V])
    v = layernorm(jax.nn.gelu(z[..., OFF_V:OFF_Q]), ln_v_w, ln_v_b)
    c_q = rmsnorm(z[..., OFF_Q:OFF_KV], q_norm_w)
    c_kv = rmsnorm(z[..., OFF_KV:OFF_PE], kv_norm_w)
    k_pe = rope(z[..., OFF_PE:][:, :, None, :], pos)[:, :, 0, :]
    q = jnp.einsum('blc,chd->blhd', c_q, w_q_b)
    q_pe = rope(q[..., NOPE_DIM:], pos)
    q_lat = jnp.einsum('blhn,chn->blhc', q[..., :NOPE_DIM], w_kv_b[..., :NOPE_DIM])
    return u, v, q_lat, q_pe, c_kv, k_pe


def ab_output(u, v, o_lat, w_kv_b, w_s, b_s, w_out_ab):
    B, L = u.shape[0], u.shape[1]
    a = spatial_gating(u, v, w_s, b_s)
    o = jnp.einsum('blhc,chv->blhv', o_lat, w_kv_b[..., NOPE_DIM:]).reshape(B, L, B_HEADS * V_DIM)
    return jnp.concatenate([a, o], axis=-1) @ w_out_ab


def mla_prompt_attention(q_lat, q_pe, c_kv, k_pe):
    B, L, H, _ = q_lat.shape
    qb = min(Q_BLOCK, L)
    n = L // qb
    kf = c_kv.astype(jnp.float32)
    pf = k_pe.astype(jnp.float32)
    k_pos = jnp.arange(L)

    def block(i):
        start = i * qb
        ql = lax.dynamic_slice_in_dim(q_lat, start, qb, axis=1).astype(jnp.float32)
        qp = lax.dynamic_slice_in_dim(q_pe, start, qb, axis=1).astype(jnp.float32)
        s = (jnp.einsum('bqhc,bkc->bhqk', ql, kf) + jnp.einsum('bqhr,bkr->bhqk', qp, pf)) * MLA_SCALE
        q_pos = start + jnp.arange(qb)
        s = jnp.where(k_pos[None, :] <= q_pos[:, None], s, -jnp.inf)
        p = jax.nn.softmax(s, axis=-1)
        return jnp.einsum('bhqk,bkc->bqhc', p, kf)

    o = lax.map(block, jnp.arange(n))
    return o.transpose(1, 0, 2, 3, 4).reshape(B, L, H, KV_LORA).astype(q_lat.dtype)


def mla_sample_attention(q_lat, q_pe, c_kv, k_pe, cache_ckv, cache_kpe, page_table):
    L = q_lat.shape[1]
    ql = q_lat.astype(jnp.float32)
    qp = q_pe.astype(jnp.float32)
    kn = c_kv.astype(jnp.float32)
    s = (jnp.einsum('bqhc,bkc->bhqk', ql, kn) + jnp.einsum('bqhr,bkr->bhqk', qp, k_pe.astype(jnp.float32))) * MLA_SCALE
    causal = jnp.tril(jnp.ones((L, L), dtype=bool))
    s = jnp.where(causal[None, None], s, -jnp.inf)
    m0 = jnp.max(s, axis=-1)
    p = jnp.exp(s - m0[..., None])
    l0 = jnp.sum(p, axis=-1)
    acc0 = jnp.einsum('bhqk,bkc->bhqc', p, kn)

    def page_step(carry, phys):
        m, l, acc = carry
        kc = cache_ckv[phys].astype(jnp.float32)
        kp = cache_kpe[phys].astype(jnp.float32)
        sp = (jnp.einsum('bqhc,bkc->bhqk', ql, kc) + jnp.einsum('bqhr,bkr->bhqk', qp, kp)) * MLA_SCALE
        m_new = jnp.maximum(m, jnp.max(sp, axis=-1))
        corr = jnp.exp(m - m_new)
        pp = jnp.exp(sp - m_new[..., None])
        l = l * corr + jnp.sum(pp, axis=-1)
        acc = acc * corr[..., None] + jnp.einsum('bhqk,bkc->bhqc', pp, kc)
        return (m_new, l, acc), None

    (m, l, acc), _ = lax.scan(page_step, (m0, l0, acc0), page_table.T)
    o = acc / l[..., None]
    return o.transpose(0, 2, 1, 3).astype(q_lat.dtype)


def hgrn2_recurrence(q, k, log_f, v, s0):
    B, L, H, K = q.shape
    V = v.shape[-1]
    c = math.gcd(L, C_CHUNK)
    n = L // c

    def to_chunks(t):
        return t.reshape(B, n, c, H, t.shape[-1]).transpose(1, 0, 3, 2, 4)

    causal = jnp.tril(jnp.ones((c, c), dtype=bool))

    def step(S, inp):
        qc, kc, gc, vc = inp
        bcum = jnp.cumsum(gc, axis=2)
        diff = bcum[:, :, :, None, :] - bcum[:, :, None, :, :]
        decay = jnp.exp(jnp.where(causal[:, :, None], diff, -jnp.inf))
        scores = jnp.einsum('bhtk,bhtsk,bhsk->bhts', qc, decay, kc)
        o = jnp.einsum('bhts,bhsv->bhtv', scores, vc) + jnp.einsum('bhtk,bhkv->bhtv', qc * jnp.exp(bcum), S)
        blast = bcum[:, :, -1, :]
        S = jnp.exp(blast)[..., None] * S + jnp.einsum('bhsk,bhsv->bhkv', kc * jnp.exp(blast[:, :, None, :] - bcum), vc)
        return S, o

    S, o = lax.scan(step, s0, (to_chunks(q), to_chunks(k), to_chunks(log_f), to_chunks(v)))
    return o.transpose(1, 0, 3, 2, 4).reshape(B, L, H, V), S


def hgrn2_mixer(h, s0, layer, w_in_c, lb_logits, g_norm_w, w_out_c):
    B, L, _ = h.shape
    z = (h @ w_in_c).astype(jnp.float32)
    q = jax.nn.silu(z[..., :C_FDIM]).reshape(B, L, C_HEADS, C_K)
    f = z[..., C_FDIM:2 * C_FDIM]
    i = z[..., 2 * C_FDIM:2 * C_FDIM + D_MODEL].reshape(B, L, C_HEADS, C_V)
    g = z[..., 2 * C_FDIM + D_MODEL:].reshape(B, L, C_HEADS, C_V)
    lb_cum = jnp.cumsum(jax.nn.softmax(lb_logits.astype(jnp.float32), axis=0), axis=0)
    lb = lb_cum[layer] - lb_cum[0]
    log_f = jnp.log(lb + (1.0 - lb) * jax.nn.sigmoid(f)).reshape(B, L, C_HEADS, C_K)
    k = ((1.0 - lb) * jax.nn.sigmoid(-f)).reshape(B, L, C_HEADS, C_K)
    o, state = hgrn2_recurrence(q, k, log_f, i, s0.astype(jnp.float32))
    o = rmsnorm(o, g_norm_w) * jax.nn.silu(g)
    return o.reshape(B, L, D_MODEL).astype(h.dtype) @ w_out_c, state.astype(s0.dtype)


def moe_ffn(h, w_router, w_exp_gate, w_exp_up, w_exp_down):
    logits = (h @ w_router).astype(jnp.float32)
    top_val, top_idx = lax.top_k(logits, TOP_K)
    gates = jax.nn.softmax(top_val, axis=-1)
    combine = jnp.sum(jax.nn.one_hot(top_idx, N_EXPERTS, dtype=jnp.float32) * gates[..., None], axis=-2)
    y = jnp.zeros(h.shape, jnp.float32)
    for e in range(N_EXPERTS):
        y = y + combine[..., e:e + 1] * swiglu(h, w_exp_gate[e], w_exp_up[e], w_exp_down[e]).astype(jnp.float32)
    return y.astype(h.dtype)


def setup_inputs(seed: int = 0) -> dict:
    key = jax.random.key(seed)
    ks = jax.random.split(key, 32)
    f32 = jnp.float32

    def nrm(k, shape, scale):
        return jax.random.normal(k, shape, f32) * scale

    def gain(k, shape):
        return 1.0 + 0.05 * jax.random.normal(k, shape, f32)

    n_pages = PAST_LEN // PAGE_SIZE
    n_used = DEC_BATCH * n_pages
    n_pool = (5 * n_used + 3) // 4
    page_table = jax.random.permutation(ks[0], n_pool)[:n_used].reshape(DEC_BATCH, n_pages).astype(jnp.int32)
    return {
        'x_prompt': nrm(ks[1], (BATCH, SEQ, D_MODEL), 1.0),
        'x_sample': nrm(ks[2], (DEC_BATCH, DEC_SEQ, D_MODEL), 1.0),
        'cache_ckv': nrm(ks[3], (n_pool, PAGE_SIZE, KV_LORA), 1.0),
        'cache_kpe': nrm(ks[4], (n_pool, PAGE_SIZE, ROPE_DIM), 1.0),
        'state_hgrn': nrm(ks[5], (DEC_BATCH, C_HEADS, C_K, C_V), 0.5),
        'page_table': page_table,
        'norm_mix0': gain(ks[6], (D_MODEL,)),
        'w_in_ab': nrm(ks[7], (D_MODEL, N_IN_AB), D_MODEL ** -0.5),
        'ln_v_w': gain(ks[8], (A_WIDTH,)),
        'ln_v_b': nrm(ks[9], (A_WIDTH,), 0.02),
        'w_s': nrm(ks[10], (A_GROUPS, A_CHUNK, A_CHUNK), A_CHUNK ** -0.5),
        'b_s': gain(ks[11], (A_GROUPS, A_CHUNK)),
        'q_norm_w': gain(ks[12], (Q_LORA,)),
        'w_q_b': nrm(ks[13], (Q_LORA, B_HEADS, NOPE_DIM + ROPE_DIM), Q_LORA ** -0.5),
        'kv_norm_w': gain(ks[14], (KV_LORA,)),
        'w_kv_b': nrm(ks[15], (KV_LORA, B_HEADS, NOPE_DIM + V_DIM), KV_LORA ** -0.5),
        'w_out_ab': nrm(ks[16], (N_MIX_AB, D_MODEL), N_MIX_AB ** -0.5),
        'norm_ffn0': gain(ks[17], (D_MODEL,)),
        'w_ffn_gate': nrm(ks[18], (D_MODEL, D_FF), D_MODEL ** -0.5),
        'w_ffn_up': nrm(ks[19], (D_MODEL, D_FF), D_MODEL ** -0.5),
        'w_ffn_down': nrm(ks[20], (D_FF, D_MODEL), D_FF ** -0.5),
        'norm_mix1': gain(ks[21], (D_MODEL,)),
        'w_in_c': nrm(ks[22], (D_MODEL, N_IN_C), D_MODEL ** -0.5),
        'lb_logits': nrm(ks[23], (DEPTH, C_FDIM), 0.5),
        'g_norm_w': gain(ks[24], (C_V,)),
        'w_out_c': nrm(ks[25], (D_MODEL, D_MODEL), D_MODEL ** -0.5),
        'norm_ffn1': gain(ks[26], (D_MODEL,)),
        'w_router': nrm(ks[27], (D_MODEL, N_EXPERTS), D_MODEL ** -0.5),
        'w_exp_gate': nrm(ks[28], (N_EXPERTS, D_MODEL, D_FF_EXPERT), D_MODEL ** -0.5),
        'w_exp_up': nrm(ks[29], (N_EXPERTS, D_MODEL, D_FF_EXPERT), D_MODEL ** -0.5),
        'w_exp_down': nrm(ks[30], (N_EXPERTS, D_FF_EXPERT, D_MODEL), D_FF_EXPERT ** -0.5),
        'norm_final': gain(ks[31], (D_MODEL,)),
    }


def reference(x_prompt, x_sample, cache_ckv, cache_kpe, state_hgrn, page_table,
              norm_mix0, w_in_ab, ln_v_w, ln_v_b, w_s, b_s, q_norm_w, w_q_b, kv_norm_w, w_kv_b, w_out_ab,
              norm_ffn0, w_ffn_gate, w_ffn_up, w_ffn_down,
              norm_mix1, w_in_c, lb_logits, g_norm_w, w_out_c,
              norm_ffn1, w_router, w_exp_gate, w_exp_up, w_exp_down, norm_final):
    pos_prompt = jnp.arange(x_prompt.shape[1], dtype=jnp.int32)
    pos_sample = PAST_LEN + jnp.arange(x_sample.shape[1], dtype=jnp.int32)
    hp, hs = x_prompt, x_sample
    for layer in range(DEPTH):
        if layer % 2 == 0:
            u, v, q_lat, q_pe, ckv_prompt, kpe_prompt = ab_project(
                rmsnorm(hp, norm_mix0), pos_prompt, w_in_ab, ln_v_w, ln_v_b, q_norm_w, w_q_b, kv_norm_w, w_kv_b)
            o_lat = mla_prompt_attention(q_lat, q_pe, ckv_prompt, kpe_prompt)
            hp = hp + ab_output(u, v, o_lat, w_kv_b, w_s, b_s, w_out_ab)
            u, v_sample, q_lat, q_pe, ckv_sample, kpe_sample = ab_project(
                rmsnorm(hs, norm_mix0), pos_sample, w_in_ab, ln_v_w, ln_v_b, q_norm_w, w_q_b, kv_norm_w, w_kv_b)
            o_lat = mla_sample_attention(q_lat, q_pe, ckv_sample, kpe_sample, cache_ckv, cache_kpe, page_table)
            hs = hs + ab_output(u, v_sample, o_lat, w_kv_b, w_s, b_s, w_out_ab)
            hp = hp + swiglu(rmsnorm(hp, norm_ffn0), w_ffn_gate, w_ffn_up, w_ffn_down)
            hs = hs + swiglu(rmsnorm(hs, norm_ffn0), w_ffn_gate, w_ffn_up, w_ffn_down)
        else:
            s0 = jnp.zeros((x_prompt.shape[0], C_HEADS, C_K, C_V), state_hgrn.dtype)
            o, state_prompt = hgrn2_mixer(rmsnorm(hp, norm_mix1), s0, layer, w_in_c, lb_logits, g_norm_w, w_out_c)
            hp = hp + o
            o, state_sample = hgrn2_mixer(rmsnorm(hs, norm_mix1), state_hgrn, layer, w_in_c, lb_logits, g_norm_w, w_out_c)
            hs = hs + o
            hp = hp + moe_ffn(rmsnorm(hp, norm_ffn1), w_router, w_exp_gate, w_exp_up, w_exp_down)
            hs = hs + moe_ffn(rmsnorm(hs, norm_ffn1), w_router, w_exp_gate, w_exp_up, w_exp_down)
    y_prompt = rmsnorm(hp, norm_final)
    y_sample = rmsnorm(hs, norm_final)
    return (y_prompt, y_sample, ckv_prompt, kpe_prompt, ckv_sample, kpe_sample, v_sample, state_prompt, state_sample)
```

```python
import functools
import math

import jax
import jax.numpy as jnp
from jax import lax
from jax.experimental import pallas as pl
from jax.experimental.pallas import tpu as pltpu

F32 = jnp.float32
BF16 = jnp.bfloat16
EPS = 1e-6
ROPE_THETA = 10000.0
TOP_K = 2
VMEM_LIMIT_BYTES = 56 * 1024 * 1024
LANES = 128
SUBLANES = 8
HGRN_CHUNK = 64
ATTN_BQ = 256
ATTN_BK = 512
NEG_INF = float("-inf")


def _params(*sem):
    return pltpu.CompilerParams(dimension_semantics=sem, vmem_limit_bytes=VMEM_LIMIT_BYTES)


def _tile(n, pref, mult=SUBLANES):
    if n <= pref:
        return n
    for t in range(pref, 0, -1):
        if n % t == 0 and t % mult == 0:
            return t
    raise ValueError(f"no tile for {n} <= {pref}")


def _dot(a, b):
    return jnp.dot(a, b, preferred_element_type=F32)


def _dot_nt(a, b):
    return lax.dot_general(a, b, (((1,), (1,)), ((), ())), preferred_element_type=F32)


def _rms(x, w):
    return x * lax.rsqrt(jnp.mean(x * x, axis=-1, keepdims=True) + EPS) * w


def _rope_rows(x, cos, sin):
    half = x.shape[-1] // 2
    x1, x2 = x[:, :half], x[:, half:]
    return jnp.concatenate([x1 * cos - x2 * sin, x1 * sin + x2 * cos], axis=-1)


def _add_norm_kernel(*refs, n_in, want_sum):
    x = refs[0][...]
    for r in refs[1:n_in]:
        x = x + r[...]
    g_ref = refs[n_in]
    outs = refs[n_in + 1:]
    k = 0
    if want_sum:
        outs[0][...] = x
        k = 1
    outs[k][...] = _rms(x, g_ref[...]).astype(outs[k].dtype)


def _add_norm(parts, gain, *, want_sum, out_dtype):
    t, d = parts[0].shape
    tm = _tile(t, 256)
    row = pl.BlockSpec((tm, d), lambda i: (i, 0))
    out_shape = []
    if want_sum:
        out_shape.append(jax.ShapeDtypeStruct((t, d), F32))
    out_shape.append(jax.ShapeDtypeStruct((t, d), out_dtype))
    return pl.pallas_call(
        functools.partial(_add_norm_kernel, n_in=len(parts), want_sum=want_sum),
        grid=(t // tm,),
        in_specs=[row] * len(parts) + [pl.BlockSpec((1, d), lambda i: (0, 0))],
        out_specs=[row] * len(out_shape),
        out_shape=out_shape,
        compiler_params=_params("parallel"),
        name="add_norm",
    )(*parts, gain.reshape(1, d))


def _mm_kernel(x_ref, w_ref, *rest, n_extra, n_out, epilogue):
    extras = rest[:n_extra]
    outs = rest[n_extra:n_extra + n_out]
    w_bf = rest[n_extra + n_out]

    @pl.when(pl.program_id(1) == 0)
    def _():
        w_bf[...] = w_ref[...].astype(BF16)

    acc = _dot(x_ref[...], w_bf[...])
    res = epilogue(acc, *[e[...] for e in extras])
    for o, r in zip(outs, res):
        o[...] = r.astype(o.dtype)


def _mm(x, w, epilogue, outs, extras=(), *, tm_pref=1024, tn_pref=1024, name="mm"):
    t, k = x.shape
    n = w.shape[1]
    tm = _tile(t, tm_pref, 16)
    tn = _tile(n, tn_pref, LANES)
    nj = n // tn
    in_specs = [pl.BlockSpec((tm, k), lambda j, i: (i, 0)),
                pl.BlockSpec((k, tn), lambda j, i: (0, j))]
    args = [x, w]
    for arr, kind in extras:
        if kind == "row":
            in_specs.append(pl.BlockSpec((tm, arr.shape[1]), lambda j, i: (i, 0)))
        elif kind == "col":
            in_specs.append(pl.BlockSpec((1, tn), lambda j, i: (0, j)))
        elif kind == "tile":
            in_specs.append(pl.BlockSpec((tm, tn), lambda j, i: (i, j)))
        else:
            raise ValueError(kind)
        args.append(arr)
    out_specs = [pl.BlockSpec((tm, wd), lambda j, i: (i, j)) for wd, _ in outs]
    out_shape = [jax.ShapeDtypeStruct((t, nj * wd), dt) for wd, dt in outs]
    res = pl.pallas_call(
        functools.partial(_mm_kernel, n_extra=len(extras), n_out=len(outs), epilogue=epilogue),
        grid=(nj, t // tm),
        in_specs=in_specs,
        out_specs=out_specs,
        out_shape=out_shape,
        scratch_shapes=[pltpu.VMEM((k, tn), BF16)],
        compiler_params=_params("arbitrary", "arbitrary"),
        name=name,
    )(*args)
    return res


def _ep_gelu(acc):
    return (jax.nn.gelu(acc),)


def _ep_gelu_ln(acc, w, b):
    g = jax.nn.gelu(acc)
    mu = jnp.mean(g, axis=-1, keepdims=True)
    c = g - mu
    var = jnp.mean(c * c, axis=-1, keepdims=True)
    return (c * lax.rsqrt(var + EPS) * w + b,)


def _ep_rms(acc, w):
    return (_rms(acc, w),)


def _ep_kv(acc, cos, sin, w, *, kv_lora):
    ckv = _rms(acc[:, :kv_lora], w)
    kpe = _rope_rows(acc[:, kv_lora:], cos, sin)
    return ckv, kpe, ckv, kpe


def _ep_add(acc, resid):
    return (acc + resid,)


def _ep_id(acc):
    return (acc,)


def _q_proj_kernel(cq_ref, wqn_ref, wqr_ref, wabs_ref, cos_ref, sin_ref, ql_ref, qp_ref, *, heads, nope, rope, scale):
    cq = cq_ref[...]
    qn = _dot(cq, wqn_ref[...])
    qr = _dot(cq, wqr_ref[...])
    cos = cos_ref[...]
    sin = sin_ref[...]
    for h in range(heads):
        lat = _dot(qn[:, h * nope:(h + 1) * nope].astype(BF16), wabs_ref[h])
        ql_ref[h] = (lat * scale).astype(ql_ref.dtype)
        qp_ref[h] = (_rope_rows(qr[:, h * rope:(h + 1) * rope], cos, sin) * scale).astype(qp_ref.dtype)


def _q_proj(cq, wqn, wqr, wabs, cos, sin, scale):
    t, ql = cq.shape
    heads, nope, kv = wabs.shape
    rope = wqr.shape[1] // heads
    tm = _tile(t, 512, 16)
    const2 = lambda i: (0, 0)
    return pl.pallas_call(
        functools.partial(_q_proj_kernel, heads=heads, nope=nope, rope=rope, scale=scale),
        grid=(t // tm,),
        in_specs=[pl.BlockSpec((tm, ql), lambda i: (i, 0)),
                  pl.BlockSpec(wqn.shape, const2),
                  pl.BlockSpec(wqr.shape, const2),
                  pl.BlockSpec(wabs.shape, lambda i: (0, 0, 0)),
                  pl.BlockSpec((tm, rope // 2), lambda i: (i, 0)),
                  pl.BlockSpec((tm, rope // 2), lambda i: (i, 0))],
        out_specs=[pl.BlockSpec((heads, tm, kv), lambda i: (0, i, 0)),
                   pl.BlockSpec((heads, tm, rope), lambda i: (0, i, 0))],
        out_shape=[jax.ShapeDtypeStruct((heads, t, kv), BF16),
                   jax.ShapeDtypeStruct((heads, t, rope), BF16)],
        compiler_params=_params("parallel"),
        name="q_proj",
    )(cq, wqn, wqr, wabs, cos, sin)


def _gmlp_kernel(u_ref, v_ref, m_ref, b_ref, a_ref, *, groups, gd):
    for g in range(groups):
        sl = slice(g * gd, (g + 1) * gd)
        mixed = _dot(m_ref[0, g], v_ref[:, sl].astype(BF16)) + b_ref[0, :, sl]
        a_ref[:, sl] = (u_ref[:, sl].astype(F32) * mixed).astype(a_ref.dtype)


def _gmlp(u, v, mats, bias, n_first):
    t, a = u.shape
    _, groups, c, _ = mats.shape
    sel = lambda n: (jnp.where(n < n_first, 0, 1), 0, 0)
    return pl.pallas_call(
        functools.partial(_gmlp_kernel, groups=groups, gd=a // groups),
        grid=(t // c,),
        in_specs=[pl.BlockSpec((c, a), lambda n: (n, 0)),
                  pl.BlockSpec((c, a), lambda n: (n, 0)),
                  pl.BlockSpec((1, groups, c, c), lambda n: (jnp.where(n < n_first, 0, 1), 0, 0, 0)),
                  pl.BlockSpec((1, c, a), sel)],
        out_specs=pl.BlockSpec((c, a), lambda n: (n, 0)),
        out_shape=jax.ShapeDtypeStruct((t, a), BF16),
        compiler_params=_params("parallel"),
        name="gmlp",
    )(u, v, mats, bias)


def _softmax_step(s, m_ref, l_ref, acc_ref, vals):
    m_prev = m_ref[...]
    m_new = jnp.maximum(m_prev, jnp.max(s, axis=-1, keepdims=True))
    alpha = jnp.exp(m_prev - m_new)
    p = jnp.exp(s - m_new[:, :1])
    l_ref[...] = alpha * l_ref[...] + jnp.sum(p, axis=-1, keepdims=True)
    acc_ref[...] = acc_ref[...] * alpha[:, :1] + _dot(p.astype(vals.dtype), vals)
    m_ref[...] = m_new


def _attn_prompt_kernel(ql_ref, qp_ref, kc_ref, kp_ref, wv_ref, o_ref, m_ref, l_ref, acc_ref, *, heads, bq, bk, vd):
    i = pl.program_id(0)
    j = pl.program_id(1)
    rows = heads * bq

    @pl.when(j == 0)
    def _():
        m_ref[...] = jnp.full(m_ref.shape, NEG_INF, F32)
        l_ref[...] = jnp.zeros(l_ref.shape, F32)
        acc_ref[...] = jnp.zeros(acc_ref.shape, F32)

    @pl.when(j * bk <= i * bq + bq - 1)
    def _():
        q = ql_ref[...].reshape(rows, ql_ref.shape[-1])
        qp = qp_ref[...].reshape(rows, qp_ref.shape[-1])
        kc = kc_ref[...]
        s = _dot_nt(q, kc) + _dot_nt(qp, kp_ref[...])
        q_pos = i * bq + lax.broadcasted_iota(jnp.int32, s.shape, 0) % bq
        k_pos = j * bk + lax.broadcasted_iota(jnp.int32, s.shape, 1)
        s = jnp.where(k_pos <= q_pos, s, NEG_INF)
        _softmax_step(s, m_ref, l_ref, acc_ref, kc)

    @pl.when(j == pl.num_programs(1) - 1)
    def _():
        o = (acc_ref[...] / l_ref[...][:, :1]).astype(BF16)
        for h in range(heads):
            o_ref[:, h * vd:(h + 1) * vd] = _dot(o[h * bq:(h + 1) * bq], wv_ref[h]).astype(o_ref.dtype)


def _attn_prompt(ql, qp, kc, kp, wv, n_prompt):
    heads, _, kv = ql.shape
    rope = qp.shape[-1]
    vd = wv.shape[-1]
    bq = _tile(n_prompt, ATTN_BQ, 16)
    bk = _tile(n_prompt, ATTN_BK, LANES)
    nq, nk = n_prompt // bq, n_prompt // bk
    rows = heads * bq

    def kmap(i, j):
        return (jnp.minimum(j, (i * bq + bq - 1) // bk), 0)

    return pl.pallas_call(
        functools.partial(_attn_prompt_kernel, heads=heads, bq=bq, bk=bk, vd=vd),
        grid=(nq, nk),
        in_specs=[pl.BlockSpec((heads, bq, kv), lambda i, j: (0, i, 0)),
                  pl.BlockSpec((heads, bq, rope), lambda i, j: (0, i, 0)),
                  pl.BlockSpec((bk, kv), kmap),
                  pl.BlockSpec((bk, rope), kmap),
                  pl.BlockSpec(wv.shape, lambda i, j: (0, 0, 0))],
        out_specs=pl.BlockSpec((bq, heads * vd), lambda i, j: (i, 0)),
        out_shape=jax.ShapeDtypeStruct((n_prompt, heads * vd), BF16),
        scratch_shapes=[pltpu.VMEM((rows, LANES), F32), pltpu.VMEM((rows, LANES), F32),
                        pltpu.VMEM((rows, kv), F32)],
        compiler_params=_params("parallel", "arbitrary"),
        name="attn_prompt",
    )(ql, qp, kc, kp, wv)


def _attn_sample_kernel(pt_ref, ql_ref, qp_ref, kcn_ref, kpn_ref, wv_ref, *rest, heads, lq, pages, page, vd):
    ckv_pages = rest[:pages]
    kpe_pages = rest[pages:2 * pages]
    o_ref, kc_s, kp_s, m_ref, l_ref, acc_ref = rest[2 * pages:]
    j = pl.program_id(1)
    q = ql_ref[0]
    qp = qp_ref[0]

    @pl.when(j == 0)
    def _():
        kcn = kcn_ref[0]
        s = _dot_nt(q.astype(F32), kcn) + _dot_nt(qp.astype(F32), kpn_ref[0])
        t_q = lax.broadcasted_iota(jnp.int32, s.shape, 0) % lq
        t_k = lax.broadcasted_iota(jnp.int32, s.shape, 1)
        s = jnp.where(t_k <= t_q, s, NEG_INF)
        m0 = jnp.max(s, axis=-1, keepdims=True)
        p = jnp.exp(s - m0)
        m_ref[...] = jnp.broadcast_to(m0, m_ref.shape)
        l_ref[...] = jnp.broadcast_to(jnp.sum(p, axis=-1, keepdims=True), l_ref.shape)
        acc_ref[...] = _dot(p, kcn)

    for r in range(pages):
        kc_s[r * page:(r + 1) * page, :] = ckv_pages[r][0].astype(BF16)
        kp_s[r * page:(r + 1) * page, :] = kpe_pages[r][0].astype(BF16)
    kc = kc_s[...]
    s = _dot_nt(q, kc) + _dot_nt(qp, kp_s[...])
    _softmax_step(s, m_ref, l_ref, acc_ref, kc)

    @pl.when(j == pl.num_programs(1) - 1)
    def _():
        o = (acc_ref[...] / l_ref[...][:, :1]).astype(BF16)
        for h in range(heads):
            full = _dot(o, wv_ref[h])
            o_ref[0, :, h * vd:(h + 1) * vd] = full[h * lq:(h + 1) * lq].astype(o_ref.dtype)


def _attn_sample(page_table, ql, qp, kc_new, kp_new, wv, cache_ckv, cache_kpe):
    nb, rows, kv = ql.shape
    rope = qp.shape[-1]
    heads, _, vd = wv.shape
    lq = rows // heads
    n_pages = page_table.shape[1]
    page = cache_ckv.shape[1]
    pages = _tile(n_pages, 16, 1)
    steps = n_pages // pages

    def page_spec(width, r):
        return pl.BlockSpec((1, page, width), lambda b, j, pt: (pt[b, j * pages + r], 0, 0))

    in_specs = [pl.BlockSpec((1, rows, kv), lambda b, j, pt: (b, 0, 0)),
                pl.BlockSpec((1, rows, rope), lambda b, j, pt: (b, 0, 0)),
                pl.BlockSpec((1, lq, kv), lambda b, j, pt: (b, 0, 0)),
                pl.BlockSpec((1, lq, rope), lambda b, j, pt: (b, 0, 0)),
                pl.BlockSpec(wv.shape, lambda b, j, pt: (0, 0, 0))]
    in_specs += [page_spec(kv, r) for r in range(pages)]
    in_specs += [page_spec(rope, r) for r in range(pages)]
    grid_spec = pltpu.PrefetchScalarGridSpec(
        num_scalar_prefetch=1,
        grid=(nb, steps),
        in_specs=in_specs,
        out_specs=pl.BlockSpec((1, lq, heads * vd), lambda b, j, pt: (b, 0, 0)),
        scratch_shapes=[pltpu.VMEM((pages * page, kv), BF16), pltpu.VMEM((pages * page, rope), BF16),
                        pltpu.VMEM((rows, LANES), F32), pltpu.VMEM((rows, LANES), F32),
                        pltpu.VMEM((rows, kv), F32)],
    )
    return pl.pallas_call(
        functools.partial(_attn_sample_kernel, heads=heads, lq=lq, pages=pages, page=page, vd=vd),
        grid_spec=grid_spec,
        out_shape=jax.ShapeDtypeStruct((nb, lq, heads * vd), BF16),
        compiler_params=_params("parallel", "arbitrary"),
        name="attn_sample",
    )(page_table, ql, qp, kc_new, kp_new, wv, *([cache_ckv] * pages), *([cache_kpe] * pages))


def _ffn_kernel(be_ref, bv_ref, x_ref, wg_ref, wu_ref, wd_ref, gate_ref, o_ref, wg_s, wu_s, wd_s, *, tm, ts):
    i = pl.program_id(0)
    j = pl.program_id(1)
    valid = bv_ref[i]

    @pl.when(j == 0)
    def _():
        o_ref[...] = jnp.zeros(o_ref.shape, F32)

    @pl.when(valid > 0)
    def _():
        wg_s[...] = wg_ref[0].astype(BF16)
        wu_s[...] = wu_ref[0].astype(BF16)
        wd_s[...] = wd_ref[0].astype(BF16)

    for sub in range(tm // ts):
        @pl.when(valid > sub * ts)
        def _():
            rows = slice(sub * ts, (sub + 1) * ts)
            x = x_ref[rows, :]
            h = (jax.nn.silu(_dot(x, wg_s[...])) * _dot(x, wu_s[...])).astype(BF16)
            o_ref[rows, :] += _dot(h, wd_s[...])

    @pl.when(j == pl.num_programs(1) - 1)
    def _():
        o_ref[...] = o_ref[...] * gate_ref[...]


def _ffn(x, wg, wu, wd, gate, blk_expert, blk_valid, *, tm, tf_pref=256, ts_pref=256):
    s, d = x.shape
    f = wg.shape[2]
    tf = _tile(f, tf_pref, LANES)
    ts = _tile(tm, ts_pref, 16)
    nj = f // tf

    def jj(i, j, bv):
        return jnp.where(bv[i] > 0, j, nj - 1)

    grid_spec = pltpu.PrefetchScalarGridSpec(
        num_scalar_prefetch=2,
        grid=(s // tm, nj),
        in_specs=[pl.BlockSpec((tm, d), lambda i, j, be, bv: (i, 0)),
                  pl.BlockSpec((1, d, tf), lambda i, j, be, bv: (be[i], 0, jj(i, j, bv))),
                  pl.BlockSpec((1, d, tf), lambda i, j, be, bv: (be[i], 0, jj(i, j, bv))),
                  pl.BlockSpec((1, tf, d), lambda i, j, be, bv: (be[i], jj(i, j, bv), 0)),
                  pl.BlockSpec((tm, 1), lambda i, j, be, bv: (i, 0))],
        out_specs=pl.BlockSpec((tm, d), lambda i, j, be, bv: (i, 0)),
        scratch_shapes=[pltpu.VMEM((d, tf), BF16), pltpu.VMEM((d, tf), BF16), pltpu.VMEM((tf, d), BF16)],
    )
    return pl.pallas_call(
        functools.partial(_ffn_kernel, tm=tm, ts=ts),
        grid_spec=grid_spec,
        out_shape=jax.ShapeDtypeStruct((s, d), F32),
        compiler_params=_params("arbitrary", "arbitrary"),
        name="ffn",
    )(blk_expert, blk_valid, x, wg, wu, wd, gate)


def _split3(a):
    a1 = a.astype(BF16)
    r1 = a - a1.astype(F32)
    a2 = r1.astype(BF16)
    a3 = (r1 - a2.astype(F32)).astype(BF16)
    return a1, a2, a3


def _router_kernel(h_ref, d_ref, g_ref, wr_ref, hs_ref, hn_ref, comb_ref, sel_ref):
    x = h_ref[...] + d_ref[...]
    hs_ref[...] = x
    y = _rms(x, g_ref[...])
    hn_ref[...] = y.astype(hn_ref.dtype)
    ys = _split3(y)
    ws = _split3(wr_ref[...])
    logits = jnp.zeros((x.shape[0], wr_ref.shape[1]), F32)
    for a in range(3):
        for b in range(3 - a):
            logits = logits + _dot(ys[a], ws[b])
    n_e = logits.shape[1]
    lane = lax.broadcasted_iota(jnp.int32, logits.shape, 1).astype(F32)
    m1 = jnp.max(logits, axis=-1, keepdims=True)
    i1 = jnp.min(jnp.where(logits == m1, lane, n_e), axis=-1, keepdims=True)
    first = lane == i1
    rest = jnp.where(first, NEG_INF, logits)
    m2 = jnp.max(rest, axis=-1, keepdims=True)
    i2 = jnp.min(jnp.where(rest == m2, lane, n_e), axis=-1, keepdims=True)
    second = lane == i2
    e = jnp.exp(m2 - m1)
    den = 1.0 + e
    comb_ref[...] = jnp.where(first, 1.0 / den, 0.0) + jnp.where(second, e / den, 0.0)
    sel_ref[...] = jnp.where(first | second, 1.0, 0.0)


def _router(h, delta, gain, w_router):
    t, d = h.shape
    n_e = w_router.shape[1]
    tm = _tile(t, 256)
    row = pl.BlockSpec((tm, d), lambda i: (i, 0))
    small = pl.BlockSpec((tm, n_e), lambda i: (i, 0))
    return pl.pallas_call(
        _router_kernel,
        grid=(t // tm,),
        in_specs=[row, row, pl.BlockSpec((1, d), lambda i: (0, 0)), pl.BlockSpec((d, n_e), lambda i: (0, 0))],
        out_specs=[row, row, small, small],
        out_shape=[jax.ShapeDtypeStruct((t, d), F32), jax.ShapeDtypeStruct((t, d), BF16),
                   jax.ShapeDtypeStruct((t, n_e), F32), jax.ShapeDtypeStruct((t, n_e), F32)],
        compiler_params=_params("parallel"),
        name="router",
    )(h, delta, gain.reshape(1, d), w_router)


def _roll_rows(x, shift):
    n = x.shape[0]
    shift = shift % n
    return x if shift == 0 else pltpu.roll(x, shift, 0)


def _hgrn_inputs(zq, zf, lb):
    q = jax.nn.silu(zq)
    lf = jnp.log(lb + (1.0 - lb) * jax.nn.sigmoid(zf))
    k = (1.0 - lb) * jax.nn.sigmoid(-zf)
    return q, k, lf


def _prefix8(lf, r8):
    p = lf
    for sh in (1, 2, 4):
        p = p + jnp.where(r8 >= sh, _roll_rows(p, sh), 0.0)
    return p


def _block8_intra(q, k, v, p8, r8):
    o = jnp.sum(q * k, axis=-1, keepdims=True) * v
    for d in range(1, SUBLANES):
        e = jnp.exp(jnp.where(r8 >= d, p8 - _roll_rows(p8, d), NEG_INF))
        w = jnp.sum(q * _roll_rows(k, d) * e, axis=-1, keepdims=True)
        o = o + w * _roll_rows(v, d)
    return o


def _hgrn_finish(o, zg, gw):
    return _rms(o, gw) * jax.nn.silu(zg)


def _hgrn_prompt_kernel(zq_ref, zf_ref, zi_ref, zg_ref, lb_ref, gw_ref, o_ref, st_ref, st_t, *, c, n_chunks):
    @pl.when(pl.program_id(1) == 0)
    def _():
        st_t[...] = jnp.zeros(st_t.shape, F32)

    lb = lb_ref[...]
    gw = gw_ref[...]
    kd = zq_ref.shape[1]
    row = lax.broadcasted_iota(jnp.int32, (c, kd), 0)
    r8 = row % SUBLANES
    ri = lax.broadcasted_iota(jnp.int32, (c, c), 0)
    ci = lax.broadcasted_iota(jnp.int32, (c, c), 1)

    def chunk(n, carry):
        rows = pl.ds(pl.multiple_of(n * c, c), c)
        q, k, lf = _hgrn_inputs(zq_ref[rows, :], zf_ref[rows, :], lb)
        v = zi_ref[rows, :]
        p = _prefix8(lf, r8)
        o = _block8_intra(q, k, v, p, r8)
        tot = jnp.where(r8 == SUBLANES - 1, p, 0.0)
        for sh in (1, 2, 4):
            tot = tot + _roll_rows(tot, -sh)
        a = jnp.zeros((c, c), F32)
        s = SUBLANES
        while s < c:
            second = row % (2 * s) >= s
            qs = (q * jnp.exp(jnp.where(second, p, NEG_INF))).astype(BF16)
            ks = (k * jnp.exp(jnp.where(second, NEG_INF, tot - p))).astype(BF16)
            a_s = _dot_nt(qs, ks)
            a = a + (a_s if 2 * s == c else jnp.where(ri // (2 * s) == ci // (2 * s), a_s, 0.0))
            prev = _roll_rows(tot, s)
            p = p + jnp.where(second, prev, 0.0)
            tot = tot + jnp.where(second, prev, _roll_rows(tot, -s))
            s *= 2
        vb = v.astype(BF16)
        st = st_t[...]
        o = o + _dot(a.astype(BF16), vb) + _dot_nt((q * jnp.exp(p)).astype(BF16), st.astype(BF16))
        kb = (k * jnp.exp(tot - p)).astype(BF16)
        st_t[...] = st * jnp.exp(tot[0:1, :]) + _dot(v.T.astype(BF16), kb)
        o_ref[rows, :] = _hgrn_finish(o, zg_ref[rows, :], gw).astype(o_ref.dtype)
        return carry

    lax.fori_loop(0, n_chunks, chunk, 0)

    @pl.when(pl.program_id(1) == pl.num_programs(1) - 1)
    def _():
        st_ref[0] = st_t[...].T


def _hgrn_prompt(z, lb, gw, n_prompt, heads):
    kd = lb.shape[1] // heads
    c = HGRN_CHUNK
    lbk = _tile(n_prompt, 1024, c)

    def zspec(seg):
        return pl.BlockSpec((lbk, kd), lambda h, l: (l, seg * heads + h))

    return pl.pallas_call(
        functools.partial(_hgrn_prompt_kernel, c=c, n_chunks=lbk // c),
        grid=(heads, n_prompt // lbk),
        in_specs=[zspec(0), zspec(1), zspec(2), zspec(3),
                  pl.BlockSpec((1, kd), lambda h, l: (0, h)),
                  pl.BlockSpec((1, kd), lambda h, l: (0, 0))],
        out_specs=[pl.BlockSpec((lbk, kd), lambda h, l: (l, h)),
                   pl.BlockSpec((1, kd, kd), lambda h, l: (h, 0, 0))],
        out_shape=[jax.ShapeDtypeStruct((n_prompt, heads * kd), BF16),
                   jax.ShapeDtypeStruct((heads, kd, kd), F32)],
        scratch_shapes=[pltpu.VMEM((kd, kd), F32)],
        compiler_params=_params("parallel", "arbitrary"),
        name="hgrn_prompt",
    )(z, z, z, z, lb, gw)


def _hgrn_sample_kernel(z_ref, s0_ref, lb_ref, gw_ref, o_ref, s1_ref, *, heads, kd):
    lq = z_ref.shape[0]
    gw = gw_ref[...]
    r8 = lax.broadcasted_iota(jnp.int32, (lq, kd), 0)
    for h in range(heads):
        col = lambda seg: slice((seg * heads + h) * kd, (seg * heads + h + 1) * kd)
        q, k, lf = _hgrn_inputs(z_ref[:, col(0)], z_ref[:, col(1)], lb_ref[:, h * kd:(h + 1) * kd])
        v = z_ref[:, col(2)]
        p = _prefix8(lf, r8)
        last = p[lq - 1:lq, :]
        st = s0_ref[0, h].T
        o = _block8_intra(q, k, v, p, r8) + _dot_nt((q * jnp.exp(p)).astype(BF16), st.astype(BF16))
        kb = (k * jnp.exp(last - p)).astype(BF16)
        st = st * jnp.exp(last) + _dot(v.T.astype(BF16), kb)
        s1_ref[0, h] = st.T
        o_ref[:, h * kd:(h + 1) * kd] = _hgrn_finish(o, z_ref[:, col(3)], gw).astype(o_ref.dtype)


def _hgrn_sample(z, state, lb, gw, row0):
    nb, heads, kd, _ = state.shape
    lq = SUBLANES
    blk0 = row0 // lq
    return pl.pallas_call(
        functools.partial(_hgrn_sample_kernel, heads=heads, kd=kd),
        grid=(nb,),
        in_specs=[pl.BlockSpec((lq, z.shape[1]), lambda b: (blk0 + b, 0)),
                  pl.BlockSpec((1, heads, kd, kd), lambda b: (b, 0, 0, 0)),
                  pl.BlockSpec((1, heads * kd), lambda b: (0, 0)),
                  pl.BlockSpec((1, kd), lambda b: (0, 0))],
        out_specs=[pl.BlockSpec((lq, heads * kd), lambda b: (b, 0)),
                   pl.BlockSpec((1, heads, kd, kd), lambda b: (b, 0, 0, 0))],
        out_shape=[jax.ShapeDtypeStruct((nb * lq, heads * kd), F32),
                   jax.ShapeDtypeStruct(state.shape, F32)],
        compiler_params=_params("parallel"),
        name="hgrn_sample",
    )(z, state, lb, gw)


def _route_plan(sel, comb, tm):
    t, n_e = sel.shape
    n_blk = -(-(TOP_K * t) // tm) + n_e
    seli = sel.astype(jnp.int32)
    incl = jnp.cumsum(seli, axis=0)
    counts = incl[-1]
    rank = incl - seli
    blocks_per = (counts + tm - 1) // tm
    blk_end = jnp.cumsum(blocks_per)
    start = (blk_end - blocks_per) * tm
    slot = jnp.where(seli > 0, start[None, :] + rank, n_blk * tm)
    tok = jnp.broadcast_to(jnp.arange(t, dtype=jnp.int32)[:, None], slot.shape)
    flat = slot.reshape(-1)
    tok_of_slot = jnp.zeros((n_blk * tm,), jnp.int32).at[flat].set(tok.reshape(-1), mode="drop")
    gate_of_slot = jnp.zeros((n_blk * tm,), F32).at[flat].set(comb.reshape(-1), mode="drop")
    blk = jnp.arange(n_blk, dtype=jnp.int32)
    used = blk_end[-1]
    e_of_blk = jnp.searchsorted(blk_end, jnp.minimum(blk, used - 1), side="right").astype(jnp.int32)
    e_of_blk = jnp.minimum(e_of_blk, n_e - 1)
    valid = jnp.clip(counts[e_of_blk] - (blk * tm - start[e_of_blk]), 0, tm)
    valid = jnp.where(blk < used, valid, 0).astype(jnp.int32)
    order = jnp.argsort(jnp.where(seli > 0, 0, 1), axis=1, stable=True)[:, :TOP_K]
    tok_slots = jnp.take_along_axis(slot, order, axis=1)
    return tok_of_slot, gate_of_slot, e_of_blk, valid, tok_slots


def kernel(x_prompt, x_sample, cache_ckv, cache_kpe, state_hgrn, page_table, norm_mix0, w_in_ab, ln_v_w, ln_v_b, w_s, b_s, q_norm_w, w_q_b, kv_norm_w, w_kv_b, w_out_ab, norm_ffn0, w_ffn_gate, w_ffn_up, w_ffn_down, norm_mix1, w_in_c, lb_logits, g_norm_w, w_out_c, norm_ffn1, w_router, w_exp_gate, w_exp_up, w_exp_down, norm_final):
    bp, lp, d = x_prompt.shape
    nb, lq, _ = x_sample.shape
    assert bp == 1, "the prompt group is one sequence"
    n_p, n_s = bp * lp, nb * lq
    t = n_p + n_s
    groups, chunk, _ = w_s.shape
    a_w = ln_v_w.shape[0]
    q_lora, heads, qk = w_q_b.shape
    kv_lora = kv_norm_w.shape[0]
    rope = cache_kpe.shape[-1]
    nope = qk - rope
    vd = w_kv_b.shape[-1] - nope
    page = cache_ckv.shape[1]
    past = page_table.shape[1] * page
    c_heads, c_k = state_hgrn.shape[1], state_hgrn.shape[2]
    c_f = c_heads * c_k
    n_e = w_router.shape[1]
    assert lq == SUBLANES and chunk % lq == 0 and n_p % chunk == 0 and n_s % chunk == 0
    off_q, off_kv = 2 * a_w, 2 * a_w + q_lora
    scale = float(qk) ** -0.5

    x = jnp.concatenate([x_prompt.reshape(n_p, d), x_sample.reshape(n_s, d)], axis=0)

    pos = jnp.concatenate([jnp.arange(lp, dtype=F32), jnp.tile(past + jnp.arange(lq, dtype=F32), nb)])
    inv_freq = ROPE_THETA ** (-jnp.arange(rope // 2, dtype=F32) / (rope // 2))
    ang = pos[:, None] * inv_freq[None, :]
    cos, sin = jnp.cos(ang), jnp.sin(ang)

    hn = _add_norm([x], norm_mix0, want_sum=False, out_dtype=BF16)[0]
    (u,) = _mm(hn, w_in_ab[:, :a_w], _ep_gelu, [(a_w, BF16)], name="in_u")
    (v,) = _mm(hn, w_in_ab[:, a_w:off_q], _ep_gelu_ln, [(a_w, F32)],
               [(ln_v_w.reshape(1, a_w), "col"), (ln_v_b.reshape(1, a_w), "col")], name="in_v")
    (cq,) = _mm(hn, w_in_ab[:, off_q:off_kv], _ep_rms, [(q_lora, BF16)],
                [(q_norm_w.reshape(1, q_lora), "col")], name="in_cq")
    kvw = kv_lora + rope
    w_kvpe = w_in_ab[:, off_kv:]
    t_kv = _tile(t, 512, 16)
    ckv, kpe, ckv_b, kpe_b = pl.pallas_call(
        functools.partial(_mm_kernel, n_extra=3, n_out=4, epilogue=functools.partial(_ep_kv, kv_lora=kv_lora)),
        grid=(1, t // t_kv),
        in_specs=[pl.BlockSpec((t_kv, d), lambda j, i: (i, 0)),
                  pl.BlockSpec((d, kvw), lambda j, i: (0, 0)),
                  pl.BlockSpec((t_kv, rope // 2), lambda j, i: (i, 0)),
                  pl.BlockSpec((t_kv, rope // 2), lambda j, i: (i, 0)),
                  pl.BlockSpec((1, kv_lora), lambda j, i: (0, 0))],
        out_specs=[pl.BlockSpec((t_kv, kv_lora), lambda j, i: (i, 0)),
                   pl.BlockSpec((t_kv, rope), lambda j, i: (i, 0)),
                   pl.BlockSpec((t_kv, kv_lora), lambda j, i: (i, 0)),
                   pl.BlockSpec((t_kv, rope), lambda j, i: (i, 0))],
        out_shape=[jax.ShapeDtypeStruct((t, kv_lora), F32), jax.ShapeDtypeStruct((t, rope), F32),
                   jax.ShapeDtypeStruct((t, kv_lora), BF16), jax.ShapeDtypeStruct((t, rope), BF16)],
        scratch_shapes=[pltpu.VMEM((d, kvw), BF16)],
        compiler_params=_params("arbitrary", "arbitrary"),
        name="in_kv",
    )(hn, w_kvpe, cos, sin, kv_norm_w.reshape(1, kv_lora))

    wqn = w_q_b[:, :, :nope].reshape(q_lora, heads * nope).astype(BF16)
    wqr = w_q_b[:, :, nope:].reshape(q_lora, heads * rope).astype(BF16)
    wabs = jnp.transpose(w_kv_b[:, :, :nope], (1, 2, 0)).astype(BF16)
    wv = jnp.transpose(w_kv_b[:, :, nope:], (1, 0, 2)).astype(BF16)
    ql, qp = _q_proj(cq, wqn, wqr, wabs, cos, sin, scale)

    o_prompt = _attn_prompt(ql, qp, ckv_b, kpe_b, wv, n_p)
    to_rows = lambda a: jnp.transpose(a[:, n_p:].reshape(heads, nb, lq, a.shape[-1]), (1, 0, 2, 3)).reshape(
        nb, heads * lq, a.shape[-1])
    o_sample = _attn_sample(page_table, to_rows(ql), to_rows(qp),
                            ckv[n_p:].reshape(nb, lq, kv_lora), kpe[n_p:].reshape(nb, lq, rope),
                            wv, cache_ckv, cache_kpe)

    tril = jnp.tril(jnp.ones((chunk, chunk), dtype=bool))
    m_prompt = jnp.where(tril[None], w_s, 0.0)
    eye = jnp.eye(chunk // lq, dtype=F32)
    m_sample = jnp.einsum("ab,gts->gatbs", eye, m_prompt[:, :lq, :lq]).reshape(groups, chunk, chunk)
    mats = jnp.stack([m_prompt, m_sample]).astype(BF16)
    bias_p = jnp.repeat(b_s.T, a_w // groups, axis=1)
    bias = jnp.stack([bias_p, jnp.tile(bias_p[:lq], (chunk // lq, 1))])
    a_mix = _gmlp(u, v, mats, bias, n_p // chunk)

    mix = jnp.concatenate([a_mix, jnp.concatenate([o_prompt, o_sample.reshape(n_s, heads * vd)], axis=0)], axis=1)
    (h1,) = _mm(mix, w_out_ab, _ep_add, [(_tile(d, 1024, LANES), F32)], [(x, "tile")], name="out_ab")

    hn = _add_norm([h1], norm_ffn0, want_sum=False, out_dtype=BF16)[0]
    tm_f = _tile(t, 1024, 16)
    nblk = t // tm_f
    y_ffn = _ffn(hn, w_ffn_gate[None], w_ffn_up[None], w_ffn_down[None], jnp.ones((t, 1), F32),
                 jnp.zeros((nblk,), jnp.int32), jnp.full((nblk,), tm_f, jnp.int32), tm=tm_f)

    h2, hn = _add_norm([h1, y_ffn], norm_mix1, want_sum=True, out_dtype=BF16)
    (z,) = _mm(hn, w_in_c, _ep_id, [(_tile(w_in_c.shape[1], 1024, LANES), F32)], name="in_c")
    lb_cum = jnp.cumsum(jax.nn.softmax(lb_logits.astype(F32), axis=0), axis=0)
    lb = (lb_cum[1] - lb_cum[0]).reshape(1, c_f)
    gw = g_norm_w.reshape(1, -1)
    o_p, state_prompt = _hgrn_prompt(z, lb, gw, n_p, c_heads)
    o_s, state_sample = _hgrn_sample(z, state_hgrn, lb, gw, n_p)
    o_c = jnp.concatenate([o_p, o_s.astype(BF16)], axis=0)
    (d3,) = _mm(o_c, w_out_c, _ep_id, [(_tile(d, 1024, LANES), F32)], name="out_c")

    h3, hn, comb, sel = _router(h2, d3, norm_ffn1, w_router)
    tm_e = _tile(t, 1024, 16)
    tok_of_slot, gate_of_slot, e_of_blk, valid, tok_slots = _route_plan(sel, comb, tm_e)
    y_sorted = _ffn(jnp.take(hn, tok_of_slot, axis=0), w_exp_gate, w_exp_up, w_exp_down,
                    gate_of_slot[:, None], e_of_blk, valid, tm=tm_e)
    parts = [h3] + [jnp.take(y_sorted, tok_slots[:, r], axis=0) for r in range(TOP_K)]
    y = _add_norm(parts, norm_final, want_sum=False, out_dtype=F32)[0]

    y_prompt = y[:n_p].reshape(bp, lp, d)
    y_sample = y[n_p:].reshape(nb, lq, d)
    return (y_prompt, y_sample,
            ckv[:n_p].reshape(bp, lp, kv_lora), kpe[:n_p].reshape(bp, lp, rope),
            ckv[n_p:].reshape(nb, lq, kv_lora), kpe[n_p:].reshape(nb, lq, rope),
            v[n_p:].reshape(nb, lq, a_w),
            state_prompt.reshape(bp, c_heads, c_k, -1), state_sample)
```

```python
import functools
import math

import jax
import jax.numpy as jnp
from jax import lax
from jax.experimental import pallas as pl
from jax.experimental.pallas import tpu as pltpu

F32 = jnp.float32
BF16 = jnp.bfloat16
U32 = jnp.uint32
EPS = 1e-6
ROPE_THETA = 10000.0
TOP_K = 2
VMEM_LIMIT_BYTES = 56 * 1024 * 1024
LANES = 128
SUBLANES = 8
HGRN_CHUNK = 64
HGRN_HEADS_PER_STEP = 4
ATTN_BQ = 256
ATTN_BK = 512
SOFTMAX_ROWS = 32
SOFTMAX_UNROLL = 8
NEG_INF = float("-inf")


def _params(*sem):
    return pltpu.CompilerParams(dimension_semantics=sem, vmem_limit_bytes=VMEM_LIMIT_BYTES)


def _tile(n, pref, mult=SUBLANES):
    if n <= pref:
        return n
    for t in range(pref, 0, -1):
        if n % t == 0 and t % mult == 0:
            return t
    raise ValueError(f"no tile for {n} <= {pref}")


def _dot(a, b):
    return jnp.dot(a, b, preferred_element_type=F32)


def _dot_nt(a, b):
    return lax.dot_general(a, b, (((1,), (1,)), ((), ())), preferred_element_type=F32)


def _rms(x, w):
    return x * lax.rsqrt(jnp.mean(x * x, axis=-1, keepdims=True) + EPS) * w


def _rope_rows(x, cos, sin):
    half = x.shape[-1] // 2
    x1, x2 = x[:, :half], x[:, half:]
    return jnp.concatenate([x1 * cos - x2 * sin, x1 * sin + x2 * cos], axis=-1)


def _add_norm_kernel(*refs, n_in, want_sum):
    x = refs[0][...]
    for r in refs[1:n_in]:
        x = x + r[...]
    g_ref = refs[n_in]
    outs = refs[n_in + 1:]
    k = 0
    if want_sum:
        outs[0][...] = x
        k = 1
    outs[k][...] = _rms(x, g_ref[...]).astype(outs[k].dtype)


def _add_norm(parts, gain, *, want_sum, out_dtype):
    t, d = parts[0].shape
    tm = _tile(t, 256)
    row = pl.BlockSpec((tm, d), lambda i: (i, 0))
    out_shape = []
    if want_sum:
        out_shape.append(jax.ShapeDtypeStruct((t, d), F32))
    out_shape.append(jax.ShapeDtypeStruct((t, d), out_dtype))
    return pl.pallas_call(
        functools.partial(_add_norm_kernel, n_in=len(parts), want_sum=want_sum),
        grid=(t // tm,),
        in_specs=[row] * len(parts) + [pl.BlockSpec((1, d), lambda i: (0, 0))],
        out_specs=[row] * len(out_shape),
        out_shape=out_shape,
        compiler_params=_params("parallel"),
        name="add_norm",
    )(*parts, gain.reshape(1, d))


def _mm_kernel(x_ref, w_ref, *rest, n_extra, n_out, epilogue):
    extras = rest[:n_extra]
    outs = rest[n_extra:n_extra + n_out]
    w_bf = rest[n_extra + n_out]

    @pl.when(pl.program_id(1) == 0)
    def _():
        w_bf[...] = w_ref[...].astype(BF16)

    acc = _dot(x_ref[...], w_bf[...])
    res = epilogue(acc, *[e[...] for e in extras])
    for o, r in zip(outs, res):
        o[...] = r.astype(o.dtype)


def _mm(x, w, epilogue, outs, extras=(), *, tm_pref=1024, tn_pref=1024, name="mm"):
    t, k = x.shape
    n = w.shape[1]
    tm = _tile(t, tm_pref, 16)
    tn = _tile(n, tn_pref, LANES)
    nj = n // tn
    in_specs = [pl.BlockSpec((tm, k), lambda j, i: (i, 0)),
                pl.BlockSpec((k, tn), lambda j, i: (0, j))]
    args = [x, w]
    for arr, kind in extras:
        if kind == "row":
            in_specs.append(pl.BlockSpec((tm, arr.shape[1]), lambda j, i: (i, 0)))
        elif kind == "col":
            in_specs.append(pl.BlockSpec((1, tn), lambda j, i: (0, j)))
        elif kind == "tile":
            in_specs.append(pl.BlockSpec((tm, tn), lambda j, i: (i, j)))
        else:
            raise ValueError(kind)
        args.append(arr)
    out_specs = [pl.BlockSpec((tm, wd), lambda j, i: (i, j)) for wd, _ in outs]
    out_shape = [jax.ShapeDtypeStruct((t, nj * wd), dt) for wd, dt in outs]
    res = pl.pallas_call(
        functools.partial(_mm_kernel, n_extra=len(extras), n_out=len(outs), epilogue=epilogue),
        grid=(nj, t // tm),
        in_specs=in_specs,
        out_specs=out_specs,
        out_shape=out_shape,
        scratch_shapes=[pltpu.VMEM((k, tn), BF16)],
        compiler_params=_params("arbitrary", "arbitrary"),
        name=name,
    )(*args)
    return res


def _ep_gelu(acc):
    return (jax.nn.gelu(acc),)


def _ep_gelu_ln(acc, w, b):
    g = jax.nn.gelu(acc)
    mu = jnp.mean(g, axis=-1, keepdims=True)
    c = g - mu
    var = jnp.mean(c * c, axis=-1, keepdims=True)
    return (c * lax.rsqrt(var + EPS) * w + b,)


def _ep_rms(acc, w):
    return (_rms(acc, w),)


def _ep_kv(acc, cos, sin, w, *, kv_lora):
    ckv = _rms(acc[:, :kv_lora], w)
    kpe = _rope_rows(acc[:, kv_lora:], cos, sin)
    return ckv, kpe, ckv, kpe


def _ep_add(acc, resid):
    return (acc + resid,)


def _ep_id(acc):
    return (acc,)


def _q_proj_kernel(cq_ref, wqn_ref, wqr_ref, wabs_ref, cos_ref, sin_ref, ql_ref, qp_ref, *, heads, nope, rope, scale):
    cq = cq_ref[...]
    qn = _dot(cq, wqn_ref[...])
    qr = _dot(cq, wqr_ref[...])
    cos = cos_ref[...]
    sin = sin_ref[...]
    for h in range(heads):
        lat = _dot(qn[:, h * nope:(h + 1) * nope].astype(BF16), wabs_ref[h])
        ql_ref[h] = (lat * scale).astype(ql_ref.dtype)
        qp_ref[h] = (_rope_rows(qr[:, h * rope:(h + 1) * rope], cos, sin) * scale).astype(qp_ref.dtype)


def _q_proj(cq, wqn, wqr, wabs, cos, sin, scale):
    t, ql = cq.shape
    heads, nope, kv = wabs.shape
    rope = wqr.shape[1] // heads
    tm = _tile(t, 512, 16)
    const2 = lambda i: (0, 0)
    return pl.pallas_call(
        functools.partial(_q_proj_kernel, heads=heads, nope=nope, rope=rope, scale=scale),
        grid=(t // tm,),
        in_specs=[pl.BlockSpec((tm, ql), lambda i: (i, 0)),
                  pl.BlockSpec(wqn.shape, const2),
                  pl.BlockSpec(wqr.shape, const2),
                  pl.BlockSpec(wabs.shape, lambda i: (0, 0, 0)),
                  pl.BlockSpec((tm, rope // 2), lambda i: (i, 0)),
                  pl.BlockSpec((tm, rope // 2), lambda i: (i, 0))],
        out_specs=[pl.BlockSpec((heads, tm, kv), lambda i: (0, i, 0)),
                   pl.BlockSpec((heads, tm, rope), lambda i: (0, i, 0))],
        out_shape=[jax.ShapeDtypeStruct((heads, t, kv), BF16),
                   jax.ShapeDtypeStruct((heads, t, rope), BF16)],
        compiler_params=_params("parallel"),
        name="q_proj",
    )(cq, wqn, wqr, wabs, cos, sin)


def _gmlp_kernel(u_ref, v_ref, m_ref, b_ref, a_ref, *, groups, gd):
    for g in range(groups):
        sl = slice(g * gd, (g + 1) * gd)
        mixed = _dot(m_ref[0, g], v_ref[:, sl].astype(BF16)) + b_ref[0, :, sl]
        a_ref[:, sl] = (u_ref[:, sl].astype(F32) * mixed).astype(a_ref.dtype)


def _gmlp(u, v, mats, bias, n_first):
    t, a = u.shape
    _, groups, c, _ = mats.shape
    sel = lambda n: (jnp.where(n < n_first, 0, 1), 0, 0)
    return pl.pallas_call(
        functools.partial(_gmlp_kernel, groups=groups, gd=a // groups),
        grid=(t // c,),
        in_specs=[pl.BlockSpec((c, a), lambda n: (n, 0)),
                  pl.BlockSpec((c, a), lambda n: (n, 0)),
                  pl.BlockSpec((1, groups, c, c), lambda n: (jnp.where(n < n_first, 0, 1), 0, 0, 0)),
                  pl.BlockSpec((1, c, a), sel)],
        out_specs=pl.BlockSpec((c, a), lambda n: (n, 0)),
        out_shape=jax.ShapeDtypeStruct((t, a), BF16),
        compiler_params=_params("parallel"),
        name="gmlp",
    )(u, v, mats, bias)


def _softmax_rows(s_ref, p_ref, a_ref, m_ref, l_ref, rc, mask_fn=None):
    rows, width = s_ref.shape

    def body(r, carry):
        rs = pl.ds(pl.multiple_of(r * rc, rc), rc)
        s = s_ref[rs, :]
        if mask_fn is not None:
            s = mask_fn(s, r * rc)
        m_prev = m_ref[rs, :]
        m_new = jnp.maximum(m_prev, jnp.max(s, axis=-1, keepdims=True))
        alpha = jnp.exp(m_prev - m_new)
        p = jnp.exp(s - jnp.tile(m_new, (1, width // LANES)))
        l_ref[rs, :] = alpha * l_ref[rs, :] + jnp.sum(p, axis=-1, keepdims=True)
        m_ref[rs, :] = m_new
        a_ref[rs, :] = alpha
        p_ref[rs, :] = p.astype(p_ref.dtype)
        return carry

    n = rows // rc
    lax.fori_loop(0, n, body, 0, unroll=math.gcd(n, SOFTMAX_UNROLL))


def _rescale_add(acc_ref, a_ref, pv):
    acc_ref[...] = acc_ref[...] * jnp.tile(a_ref[...], (1, acc_ref.shape[1] // LANES)) + pv


def _attn_prompt_kernel(ql_ref, qp_ref, kc_ref, kp_ref, wv_ref, o_ref, m_ref, l_ref, acc_ref, s_ref, p_ref, a_ref,
                        *, heads, bq, bk, vd):
    i = pl.program_id(0)
    j = pl.program_id(1)
    rows = heads * bq

    @pl.when(j == 0)
    def _():
        m_ref[...] = jnp.full(m_ref.shape, NEG_INF, F32)
        l_ref[...] = jnp.zeros(l_ref.shape, F32)
        acc_ref[...] = jnp.zeros(acc_ref.shape, F32)

    def causal(s, row0):
        q_pos = i * bq + (row0 + lax.broadcasted_iota(jnp.int32, s.shape, 0)) % bq
        k_pos = j * bk + lax.broadcasted_iota(jnp.int32, s.shape, 1)
        return jnp.where(k_pos <= q_pos, s, NEG_INF)

    def step(mask_fn):
        q = ql_ref[...].reshape(rows, ql_ref.shape[-1])
        qp = qp_ref[...].reshape(rows, qp_ref.shape[-1])
        s_ref[...] = _dot_nt(q, kc_ref[...]) + _dot_nt(qp, kp_ref[...])
        _softmax_rows(s_ref, p_ref, a_ref, m_ref, l_ref, SOFTMAX_ROWS, mask_fn)
        _rescale_add(acc_ref, a_ref, _dot(p_ref[...], kc_ref[...]))

    needed = j * bk <= i * bq + bq - 1
    crosses_diagonal = j * bk + bk - 1 > i * bq

    @pl.when(needed & crosses_diagonal)
    def _():
        step(causal)

    @pl.when(needed & jnp.logical_not(crosses_diagonal))
    def _():
        step(None)

    @pl.when(j == pl.num_programs(1) - 1)
    def _():
        o = (acc_ref[...] / l_ref[...][:, :1]).astype(BF16)
        for h in range(heads):
            o_ref[:, h * vd:(h + 1) * vd] = _dot(o[h * bq:(h + 1) * bq], wv_ref[h]).astype(o_ref.dtype)


def _attn_prompt(ql, qp, kc, kp, wv, n_prompt):
    heads, _, kv = ql.shape
    rope = qp.shape[-1]
    vd = wv.shape[-1]
    bq = _tile(n_prompt, ATTN_BQ, SOFTMAX_ROWS)
    bk = _tile(n_prompt, ATTN_BK, LANES)
    nq, nk = n_prompt // bq, n_prompt // bk
    rows = heads * bq

    def kmap(i, j):
        return (jnp.minimum(j, (i * bq + bq - 1) // bk), 0)

    return pl.pallas_call(
        functools.partial(_attn_prompt_kernel, heads=heads, bq=bq, bk=bk, vd=vd),
        grid=(nq, nk),
        in_specs=[pl.BlockSpec((heads, bq, kv), lambda i, j: (0, i, 0)),
                  pl.BlockSpec((heads, bq, rope), lambda i, j: (0, i, 0)),
                  pl.BlockSpec((bk, kv), kmap),
                  pl.BlockSpec((bk, rope), kmap),
                  pl.BlockSpec(wv.shape, lambda i, j: (0, 0, 0))],
        out_specs=pl.BlockSpec((bq, heads * vd), lambda i, j: (i, 0)),
        out_shape=jax.ShapeDtypeStruct((n_prompt, heads * vd), BF16),
        scratch_shapes=[pltpu.VMEM((rows, LANES), F32), pltpu.VMEM((rows, LANES), F32),
                        pltpu.VMEM((rows, kv), F32), pltpu.VMEM((rows, bk), F32),
                        pltpu.VMEM((rows, bk), BF16), pltpu.VMEM((rows, LANES), F32)],
        compiler_params=_params("parallel", "arbitrary"),
        name="attn_prompt",
    )(ql, qp, kc, kp, wv)


def _attn_sample_kernel(pt_ref, ql_ref, qp_ref, kcn_ref, kpn_ref, wv_ref, *rest, heads, lq, pages, page, vd):
    ckv_pages = rest[:pages]
    kpe_pages = rest[pages:2 * pages]
    o_ref, kc_s, kp_s, m_ref, l_ref, acc_ref, s_ref, p_ref, a_ref = rest[2 * pages:]
    j = pl.program_id(1)
    q = ql_ref[0]
    qp = qp_ref[0]

    @pl.when(j == 0)
    def _():
        kcn = kcn_ref[0]
        s = _dot_nt(q.astype(F32), kcn) + _dot_nt(qp.astype(F32), kpn_ref[0])
        t_q = lax.broadcasted_iota(jnp.int32, s.shape, 0) % lq
        t_k = lax.broadcasted_iota(jnp.int32, s.shape, 1)
        s = jnp.where(t_k <= t_q, s, NEG_INF)
        m0 = jnp.max(s, axis=-1, keepdims=True)
        p = jnp.exp(s - m0)
        m_ref[...] = jnp.broadcast_to(m0, m_ref.shape)
        l_ref[...] = jnp.broadcast_to(jnp.sum(p, axis=-1, keepdims=True), l_ref.shape)
        acc_ref[...] = _dot(p, kcn)

    for r in range(pages):
        kc_s[r * page:(r + 1) * page, :] = ckv_pages[r][0].astype(BF16)
        kp_s[:, r * page:(r + 1) * page] = kpe_pages[r][0].astype(BF16)
    s_ref[...] = _dot_nt(q, kc_s[...]) + _dot(qp, kp_s[...])
    _softmax_rows(s_ref, p_ref, a_ref, m_ref, l_ref, 2 * SUBLANES)
    _rescale_add(acc_ref, a_ref, _dot(p_ref[...], kc_s[...]))

    @pl.when(j == pl.num_programs(1) - 1)
    def _():
        o = (acc_ref[...] / l_ref[...][:, :1]).astype(BF16)
        for h in range(heads):
            full = _dot(o, wv_ref[h])
            o_ref[0, :, h * vd:(h + 1) * vd] = full[h * lq:(h + 1) * lq].astype(o_ref.dtype)


def _attn_sample(page_table, ql, qp, kc_new, kp_new, wv, cache_ckv, cache_kpe_t):
    nb, rows, kv = ql.shape
    rope = qp.shape[-1]
    heads, _, vd = wv.shape
    lq = rows // heads
    n_pages = page_table.shape[1]
    page = cache_ckv.shape[1]
    pages = _tile(n_pages, 16, 1)
    steps = n_pages // pages

    def page_spec(shape, r):
        return pl.BlockSpec((1,) + shape, lambda b, j, pt: (pt[b, j * pages + r], 0, 0))

    in_specs = [pl.BlockSpec((1, rows, kv), lambda b, j, pt: (b, 0, 0)),
                pl.BlockSpec((1, rows, rope), lambda b, j, pt: (b, 0, 0)),
                pl.BlockSpec((1, lq, kv), lambda b, j, pt: (b, 0, 0)),
                pl.BlockSpec((1, lq, rope), lambda b, j, pt: (b, 0, 0)),
                pl.BlockSpec(wv.shape, lambda b, j, pt: (0, 0, 0))]
    in_specs += [page_spec((page, kv), r) for r in range(pages)]
    in_specs += [page_spec((rope, page), r) for r in range(pages)]
    keys = pages * page
    grid_spec = pltpu.PrefetchScalarGridSpec(
        num_scalar_prefetch=1,
        grid=(nb, steps),
        in_specs=in_specs,
        out_specs=pl.BlockSpec((1, lq, heads * vd), lambda b, j, pt: (b, 0, 0)),
        scratch_shapes=[pltpu.VMEM((keys, kv), BF16), pltpu.VMEM((rope, keys), BF16),
                        pltpu.VMEM((rows, LANES), F32), pltpu.VMEM((rows, LANES), F32),
                        pltpu.VMEM((rows, kv), F32), pltpu.VMEM((rows, keys), F32),
                        pltpu.VMEM((rows, keys), BF16), pltpu.VMEM((rows, LANES), F32)],
    )
    return pl.pallas_call(
        functools.partial(_attn_sample_kernel, heads=heads, lq=lq, pages=pages, page=page, vd=vd),
        grid_spec=grid_spec,
        out_shape=jax.ShapeDtypeStruct((nb, lq, heads * vd), BF16),
        compiler_params=_params("parallel", "arbitrary"),
        name="attn_sample",
    )(page_table, ql, qp, kc_new, kp_new, wv, *([cache_ckv] * pages), *([cache_kpe_t] * pages))


def _pack_halves(y):
    half = y.shape[1] // 2
    hi = lax.bitcast_convert_type(y[:, :half].astype(BF16).astype(F32), U32)
    lo = lax.bitcast_convert_type(y[:, half:].astype(BF16).astype(F32), U32)
    return lax.bitcast_convert_type(hi | (lo >> 16), F32)


def _unpack_halves(words):
    w = lax.bitcast_convert_type(words, U32)
    hi = lax.bitcast_convert_type(w & jnp.uint32(0xFFFF0000), F32).astype(BF16)
    lo = lax.bitcast_convert_type(w << 16, F32).astype(BF16)
    return hi, lo


def _ffn_kernel(be_ref, bv_ref, x_ref, wg_ref, wu_ref, wd_ref, gate_ref, o_ref, wg_s, wu_s, wd_s, *maybe_x_s,
                tm, ts):
    i = pl.program_id(0)
    j = pl.program_id(1)
    valid = bv_ref[i]
    x_src = x_ref
    if maybe_x_s:
        (x_src,) = maybe_x_s
        half = x_ref.shape[1]

    @pl.when(j == 0)
    def _():
        o_ref[...] = jnp.zeros(o_ref.shape, F32)
        if maybe_x_s:
            hi, lo = _unpack_halves(x_ref[...])
            x_src[:, :half] = hi
            x_src[:, half:] = lo

    @pl.when(valid > 0)
    def _():
        wg_s[...] = wg_ref[0].astype(BF16)
        wu_s[...] = wu_ref[0].astype(BF16)
        wd_s[...] = wd_ref[0].astype(BF16)

    def swiglu(rows):
        x = x_src[rows, :]
        h = (jax.nn.silu(_dot(x, wg_s[...])) * _dot(x, wu_s[...])).astype(BF16)
        o_ref[rows, :] += _dot(h, wd_s[...])

    @pl.when(valid == tm)
    def _():
        swiglu(slice(None))

    for sub in range(tm // ts):
        @pl.when((valid > sub * ts) & (valid < tm))
        def _():
            swiglu(slice(sub * ts, (sub + 1) * ts))

    @pl.when(j == pl.num_programs(1) - 1)
    def _():
        o_ref[...] = o_ref[...] * gate_ref[...]


def _ffn(x, wg, wu, wd, gate, blk_expert, blk_valid, *, tm, tf_pref=512, ts_pref=256):
    s = x.shape[0]
    d, f = wg.shape[1], wg.shape[2]
    packed = x.dtype != BF16
    tf = _tile(f, tf_pref, LANES)
    ts = _tile(tm, ts_pref, 16)
    nj = f // tf

    def jj(i, j, bv):
        return jnp.where(bv[i] > 0, j, nj - 1)

    once = pl.Buffered(1)
    scratch = [pltpu.VMEM((d, tf), BF16), pltpu.VMEM((d, tf), BF16), pltpu.VMEM((tf, d), BF16)]
    if packed:
        scratch.append(pltpu.VMEM((tm, d), BF16))
    grid_spec = pltpu.PrefetchScalarGridSpec(
        num_scalar_prefetch=2,
        grid=(s // tm, nj),
        in_specs=[pl.BlockSpec((tm, x.shape[1]), lambda i, j, be, bv: (i, 0), pipeline_mode=once),
                  pl.BlockSpec((1, d, tf), lambda i, j, be, bv: (be[i], 0, jj(i, j, bv))),
                  pl.BlockSpec((1, d, tf), lambda i, j, be, bv: (be[i], 0, jj(i, j, bv))),
                  pl.BlockSpec((1, tf, d), lambda i, j, be, bv: (be[i], jj(i, j, bv), 0)),
                  pl.BlockSpec((tm, 1), lambda i, j, be, bv: (i, 0), pipeline_mode=once)],
        out_specs=pl.BlockSpec((tm, d), lambda i, j, be, bv: (i, 0), pipeline_mode=once),
        scratch_shapes=scratch,
    )
    return pl.pallas_call(
        functools.partial(_ffn_kernel, tm=tm, ts=ts),
        grid_spec=grid_spec,
        out_shape=jax.ShapeDtypeStruct((s, d), F32),
        compiler_params=_params("arbitrary", "arbitrary"),
        name="ffn",
    )(blk_expert, blk_valid, x, wg, wu, wd, gate)


def _split3(a):
    a1 = a.astype(BF16)
    r1 = a - a1.astype(F32)
    a2 = r1.astype(BF16)
    a3 = (r1 - a2.astype(F32)).astype(BF16)
    return a1, a2, a3


def _router_kernel(h_ref, d_ref, g_ref, wr_ref, hs_ref, hn_ref, comb_ref, sel_ref):
    x = h_ref[...] + d_ref[...]
    hs_ref[...] = x
    y = _rms(x, g_ref[...])
    hn_ref[...] = _pack_halves(y)
    ys = _split3(y)
    ws = _split3(wr_ref[...])
    logits = jnp.zeros((x.shape[0], wr_ref.shape[1]), F32)
    for a in range(3):
        for b in range(3 - a):
            logits = logits + _dot(ys[a], ws[b])
    n_e = logits.shape[1]
    lane = lax.broadcasted_iota(jnp.int32, logits.shape, 1).astype(F32)
    m1 = jnp.max(logits, axis=-1, keepdims=True)
    i1 = jnp.min(jnp.where(logits == m1, lane, n_e), axis=-1, keepdims=True)
    first = lane == i1
    rest = jnp.where(first, NEG_INF, logits)
    m2 = jnp.max(rest, axis=-1, keepdims=True)
    i2 = jnp.min(jnp.where(rest == m2, lane, n_e), axis=-1, keepdims=True)
    second = lane == i2
    e = jnp.exp(m2 - m1)
    den = 1.0 + e
    comb_ref[...] = jnp.where(first, 1.0 / den, 0.0) + jnp.where(second, e / den, 0.0)
    sel_ref[...] = jnp.where(first | second, 1.0, 0.0)


def _router(h, delta, gain, w_router):
    t, d = h.shape
    n_e = w_router.shape[1]
    tm = _tile(t, 256)
    row = pl.BlockSpec((tm, d), lambda i: (i, 0))
    small = pl.BlockSpec((tm, n_e), lambda i: (i, 0))
    return pl.pallas_call(
        _router_kernel,
        grid=(t // tm,),
        in_specs=[row, row, pl.BlockSpec((1, d), lambda i: (0, 0)), pl.BlockSpec((d, n_e), lambda i: (0, 0))],
        out_specs=[row, pl.BlockSpec((tm, d // 2), lambda i: (i, 0)), small, small],
        out_shape=[jax.ShapeDtypeStruct((t, d), F32), jax.ShapeDtypeStruct((t, d // 2), F32),
                   jax.ShapeDtypeStruct((t, n_e), F32), jax.ShapeDtypeStruct((t, n_e), F32)],
        compiler_params=_params("parallel"),
        name="router",
    )(h, delta, gain.reshape(1, d), w_router)


def _roll_rows(x, shift):
    n = x.shape[0]
    shift = shift % n
    return x if shift == 0 else pltpu.roll(x, shift, 0)


def _hgrn_inputs(zq, zf, lb):
    q = jax.nn.silu(zq)
    lf = jnp.log(lb + (1.0 - lb) * jax.nn.sigmoid(zf))
    k = (1.0 - lb) * jax.nn.sigmoid(-zf)
    return q, k, lf


def _prefix8(lf, r8):
    p = lf
    for sh in (1, 2, 4):
        p = p + jnp.where(r8 >= sh, _roll_rows(p, sh), 0.0)
    return p


def _block8_intra(q, k, v, p8, r8):
    o = jnp.sum(q * k, axis=-1, keepdims=True) * v
    for d in range(1, SUBLANES):
        e = jnp.exp(jnp.where(r8 >= d, p8 - _roll_rows(p8, d), NEG_INF))
        w = jnp.sum(q * _roll_rows(k, d) * e, axis=-1, keepdims=True)
        o = o + w * _roll_rows(v, d)
    return o


def _hgrn_finish(o, zg, gw):
    return _rms(o, gw) * jax.nn.silu(zg)


def _hgrn_prompt_kernel(zq_ref, zf_ref, zi_ref, zg_ref, lb_ref, gw_ref, o_ref, st_ref, st_t, *, c, n_chunks, hp, kd):
    @pl.when(pl.program_id(1) == 0)
    def _():
        st_t[...] = jnp.zeros(st_t.shape, F32)

    gw = gw_ref[...]
    row = lax.broadcasted_iota(jnp.int32, (c, kd), 0)
    r8 = row % SUBLANES
    ri = lax.broadcasted_iota(jnp.int32, (c, c), 0)
    ci = lax.broadcasted_iota(jnp.int32, (c, c), 1)

    def one_head(rows, hh):
        cols = slice(hh * kd, (hh + 1) * kd)
        q, k, lf = _hgrn_inputs(zq_ref[rows, cols], zf_ref[rows, cols], lb_ref[:, cols])
        v = zi_ref[rows, cols]
        p = _prefix8(lf, r8)
        o = _block8_intra(q, k, v, p, r8)
        tot = jnp.where(r8 == SUBLANES - 1, p, 0.0)
        for sh in (1, 2, 4):
            tot = tot + _roll_rows(tot, -sh)
        a = jnp.zeros((c, c), F32)
        s = SUBLANES
        while s < c:
            second = row % (2 * s) >= s
            qs = (q * jnp.exp(jnp.where(second, p, NEG_INF))).astype(BF16)
            ks = (k * jnp.exp(jnp.where(second, NEG_INF, tot - p))).astype(BF16)
            a_s = _dot_nt(qs, ks)
            a = a + (a_s if 2 * s == c else jnp.where(ri // (2 * s) == ci // (2 * s), a_s, 0.0))
            prev = _roll_rows(tot, s)
            p = p + jnp.where(second, prev, 0.0)
            tot = tot + jnp.where(second, prev, _roll_rows(tot, -s))
            s *= 2
        st = st_t[hh]
        o = o + _dot(a.astype(BF16), v.astype(BF16)) + _dot_nt((q * jnp.exp(p)).astype(BF16), st.astype(BF16))
        kb = (k * jnp.exp(tot - p)).astype(BF16)
        st_t[hh] = st * jnp.exp(tot[0:1, :]) + _dot(v.T.astype(BF16), kb)
        o_ref[rows, cols] = _hgrn_finish(o, zg_ref[rows, cols], gw).astype(o_ref.dtype)

    def chunk(n, carry):
        rows = pl.ds(pl.multiple_of(n * c, c), c)
        for hh in range(hp):
            one_head(rows, hh)
        return carry

    lax.fori_loop(0, n_chunks, chunk, 0)

    @pl.when(pl.program_id(1) == pl.num_programs(1) - 1)
    def _():
        for hh in range(hp):
            st_ref[hh] = st_t[hh].T


def _hgrn_prompt(z, lb, gw, n_prompt, heads):
    kd = lb.shape[1] // heads
    c = HGRN_CHUNK
    hp = math.gcd(heads, HGRN_HEADS_PER_STEP)
    lbk = _tile(n_prompt, 1024, c)
    groups = heads // hp

    def zspec(seg):
        return pl.BlockSpec((lbk, hp * kd), lambda h, l: (l, seg * groups + h))

    return pl.pallas_call(
        functools.partial(_hgrn_prompt_kernel, c=c, n_chunks=lbk // c, hp=hp, kd=kd),
        grid=(groups, n_prompt // lbk),
        in_specs=[zspec(0), zspec(1), zspec(2), zspec(3),
                  pl.BlockSpec((1, hp * kd), lambda h, l: (0, h)),
                  pl.BlockSpec((1, kd), lambda h, l: (0, 0))],
        out_specs=[pl.BlockSpec((lbk, hp * kd), lambda h, l: (l, h)),
                   pl.BlockSpec((hp, kd, kd), lambda h, l: (h, 0, 0))],
        out_shape=[jax.ShapeDtypeStruct((n_prompt, heads * kd), BF16),
                   jax.ShapeDtypeStruct((heads, kd, kd), F32)],
        scratch_shapes=[pltpu.VMEM((hp, kd, kd), F32)],
        compiler_params=_params("parallel", "arbitrary"),
        name="hgrn_prompt",
    )(z, z, z, z, lb, gw)


def _hgrn_sample_kernel(z_ref, s0_ref, lb_ref, gw_ref, o_ref, s1_ref, *, heads, kd):
    lq = z_ref.shape[0]
    gw = gw_ref[...]
    r8 = lax.broadcasted_iota(jnp.int32, (lq, kd), 0)
    for h in range(heads):
        col = lambda seg: slice((seg * heads + h) * kd, (seg * heads + h + 1) * kd)
        q, k, lf = _hgrn_inputs(z_ref[:, col(0)], z_ref[:, col(1)], lb_ref[:, h * kd:(h + 1) * kd])
        v = z_ref[:, col(2)]
        p = _prefix8(lf, r8)
        last = p[lq - 1:lq, :]
        st = s0_ref[0, h].T
        o = _block8_intra(q, k, v, p, r8) + _dot_nt((q * jnp.exp(p)).astype(BF16), st.astype(BF16))
        kb = (k * jnp.exp(last - p)).astype(BF16)
        st = st * jnp.exp(last) + _dot(v.T.astype(BF16), kb)
        s1_ref[0, h] = st.T
        o_ref[:, h * kd:(h + 1) * kd] = _hgrn_finish(o, z_ref[:, col(3)], gw).astype(o_ref.dtype)


def _hgrn_sample(z, state, lb, gw, row0):
    nb, heads, kd, _ = state.shape
    lq = SUBLANES
    blk0 = row0 // lq
    return pl.pallas_call(
        functools.partial(_hgrn_sample_kernel, heads=heads, kd=kd),
        grid=(nb,),
        in_specs=[pl.BlockSpec((lq, z.shape[1]), lambda b: (blk0 + b, 0)),
                  pl.BlockSpec((1, heads, kd, kd), lambda b: (b, 0, 0, 0)),
                  pl.BlockSpec((1, heads * kd), lambda b: (0, 0)),
                  pl.BlockSpec((1, kd), lambda b: (0, 0))],
        out_specs=[pl.BlockSpec((lq, heads * kd), lambda b: (b, 0)),
                   pl.BlockSpec((1, heads, kd, kd), lambda b: (b, 0, 0, 0))],
        out_shape=[jax.ShapeDtypeStruct((nb * lq, heads * kd), F32),
                   jax.ShapeDtypeStruct(state.shape, F32)],
        compiler_params=_params("parallel"),
        name="hgrn_sample",
    )(z, state, lb, gw)


def _route_plan(sel, comb, tm):
    t, n_e = sel.shape
    n_blk = -(-(TOP_K * t) // tm) + n_e
    n_slot = n_blk * tm
    picked = sel > 0
    seli = picked.astype(jnp.int32)
    incl = jnp.cumsum(seli, axis=0)
    counts = incl[-1]
    blocks_per = (counts + tm - 1) // tm
    blk_end = jnp.cumsum(blocks_per)
    start = (blk_end - blocks_per) * tm
    slot = start[None, :] + incl - seli
    slot_a = jnp.min(jnp.where(picked, slot, n_slot), axis=1)
    slot_b = jnp.max(jnp.where(picked, slot, -1), axis=1)
    tok_slots = jnp.stack([slot_a, slot_b], axis=1)
    gates = jnp.stack([jnp.sum(jnp.where(picked & (slot == s[:, None]), comb, 0.0), axis=1)
                       for s in (slot_a, slot_b)], axis=1).reshape(-1)
    pair = jnp.arange(TOP_K * t, dtype=jnp.int32)
    pair_of_slot = jnp.full((n_slot,), -1, jnp.int32).at[tok_slots.reshape(-1)].set(pair, unique_indices=True)
    filled = pair_of_slot >= 0
    safe = jnp.maximum(pair_of_slot, 0)
    tok_of_slot = safe // TOP_K
    gate_of_slot = jnp.where(filled, gates.at[safe].get(mode="promise_in_bounds"), 0.0)
    blk = jnp.arange(n_blk, dtype=jnp.int32)
    used = blk_end[-1]
    e_of_blk = jnp.sum((jnp.minimum(blk, used - 1)[:, None] >= blk_end[None, :]).astype(jnp.int32), axis=1)
    e_of_blk = jnp.minimum(e_of_blk, n_e - 1)
    valid = jnp.clip(counts[e_of_blk] - (blk * tm - start[e_of_blk]), 0, tm)
    valid = jnp.where(blk < used, valid, 0).astype(jnp.int32)
    return tok_of_slot, gate_of_slot, e_of_blk, valid, tok_slots


def kernel(x_prompt, x_sample, cache_ckv, cache_kpe, state_hgrn, page_table, norm_mix0, w_in_ab, ln_v_w, ln_v_b, w_s, b_s, q_norm_w, w_q_b, kv_norm_w, w_kv_b, w_out_ab, norm_ffn0, w_ffn_gate, w_ffn_up, w_ffn_down, norm_mix1, w_in_c, lb_logits, g_norm_w, w_out_c, norm_ffn1, w_router, w_exp_gate, w_exp_up, w_exp_down, norm_final):
    bp, lp, d = x_prompt.shape
    nb, lq, _ = x_sample.shape
    assert bp == 1, "the prompt group is one sequence"
    n_p, n_s = bp * lp, nb * lq
    t = n_p + n_s
    groups, chunk, _ = w_s.shape
    a_w = ln_v_w.shape[0]
    q_lora, heads, qk = w_q_b.shape
    kv_lora = kv_norm_w.shape[0]
    rope = cache_kpe.shape[-1]
    nope = qk - rope
    vd = w_kv_b.shape[-1] - nope
    page = cache_ckv.shape[1]
    past = page_table.shape[1] * page
    c_heads, c_k = state_hgrn.shape[1], state_hgrn.shape[2]
    c_f = c_heads * c_k
    assert lq == SUBLANES and chunk % lq == 0 and n_p % chunk == 0 and n_s % chunk == 0
    off_q, off_kv = 2 * a_w, 2 * a_w + q_lora
    scale = float(qk) ** -0.5

    x = jnp.concatenate([x_prompt.reshape(n_p, d), x_sample.reshape(n_s, d)], axis=0)

    pos = jnp.concatenate([jnp.arange(lp, dtype=F32), jnp.tile(past + jnp.arange(lq, dtype=F32), nb)])
    inv_freq = ROPE_THETA ** (-jnp.arange(rope // 2, dtype=F32) / (rope // 2))
    ang = pos[:, None] * inv_freq[None, :]
    cos, sin = jnp.cos(ang), jnp.sin(ang)

    hn = _add_norm([x], norm_mix0, want_sum=False, out_dtype=BF16)[0]
    (u,) = _mm(hn, w_in_ab[:, :a_w], _ep_gelu, [(a_w, BF16)], name="in_u")
    (v,) = _mm(hn, w_in_ab[:, a_w:off_q], _ep_gelu_ln, [(a_w, F32)],
               [(ln_v_w.reshape(1, a_w), "col"), (ln_v_b.reshape(1, a_w), "col")], name="in_v")
    (cq,) = _mm(hn, w_in_ab[:, off_q:off_kv], _ep_rms, [(q_lora, BF16)],
                [(q_norm_w.reshape(1, q_lora), "col")], name="in_cq")
    kvw = kv_lora + rope
    w_kvpe = w_in_ab[:, off_kv:]
    t_kv = _tile(t, 512, 16)
    ckv, kpe, ckv_b, kpe_b = pl.pallas_call(
        functools.partial(_mm_kernel, n_extra=3, n_out=4, epilogue=functools.partial(_ep_kv, kv_lora=kv_lora)),
        grid=(1, t // t_kv),
        in_specs=[pl.BlockSpec((t_kv, d), lambda j, i: (i, 0)),
                  pl.BlockSpec((d, kvw), lambda j, i: (0, 0)),
                  pl.BlockSpec((t_kv, rope // 2), lambda j, i: (i, 0)),
                  pl.BlockSpec((t_kv, rope // 2), lambda j, i: (i, 0)),
                  pl.BlockSpec((1, kv_lora), lambda j, i: (0, 0))],
        out_specs=[pl.BlockSpec((t_kv, kv_lora), lambda j, i: (i, 0)),
                   pl.BlockSpec((t_kv, rope), lambda j, i: (i, 0)),
                   pl.BlockSpec((t_kv, kv_lora), lambda j, i: (i, 0)),
                   pl.BlockSpec((t_kv, rope), lambda j, i: (i, 0))],
        out_shape=[jax.ShapeDtypeStruct((t, kv_lora), F32), jax.ShapeDtypeStruct((t, rope), F32),
                   jax.ShapeDtypeStruct((t, kv_lora), BF16), jax.ShapeDtypeStruct((t, rope), BF16)],
        scratch_shapes=[pltpu.VMEM((d, kvw), BF16)],
        compiler_params=_params("arbitrary", "arbitrary"),
        name="in_kv",
    )(hn, w_kvpe, cos, sin, kv_norm_w.reshape(1, kv_lora))

    wqn = w_q_b[:, :, :nope].reshape(q_lora, heads * nope).astype(BF16)
    wqr = w_q_b[:, :, nope:].reshape(q_lora, heads * rope).astype(BF16)
    wabs = jnp.transpose(w_kv_b[:, :, :nope], (1, 2, 0)).astype(BF16)
    wv = jnp.transpose(w_kv_b[:, :, nope:], (1, 0, 2)).astype(BF16)
    ql, qp = _q_proj(cq, wqn, wqr, wabs, cos, sin, scale)

    o_prompt = _attn_prompt(ql, qp, ckv_b, kpe_b, wv, n_p)
    to_rows = lambda a: jnp.transpose(a[:, n_p:].reshape(heads, nb, lq, a.shape[-1]), (1, 0, 2, 3)).reshape(
        nb, heads * lq, a.shape[-1])
    o_sample = _attn_sample(page_table, to_rows(ql), to_rows(qp),
                            ckv[n_p:].reshape(nb, lq, kv_lora), kpe[n_p:].reshape(nb, lq, rope),
                            wv, cache_ckv, jnp.swapaxes(cache_kpe, 1, 2))

    tril = jnp.tril(jnp.ones((chunk, chunk), dtype=bool))
    m_prompt = jnp.where(tril[None], w_s, 0.0)
    eye = jnp.eye(chunk // lq, dtype=F32)
    m_sample = jnp.einsum("ab,gts->gatbs", eye, m_prompt[:, :lq, :lq]).reshape(groups, chunk, chunk)
    mats = jnp.stack([m_prompt, m_sample]).astype(BF16)
    bias_p = jnp.repeat(b_s.T, a_w // groups, axis=1)
    bias = jnp.stack([bias_p, jnp.tile(bias_p[:lq], (chunk // lq, 1))])
    a_mix = _gmlp(u, v, mats, bias, n_p // chunk)

    mix = jnp.concatenate([a_mix, jnp.concatenate([o_prompt, o_sample.reshape(n_s, heads * vd)], axis=0)], axis=1)
    (h1,) = _mm(mix, w_out_ab, _ep_add, [(_tile(d, 1024, LANES), F32)], [(x, "tile")], name="out_ab")

    hn = _add_norm([h1], norm_ffn0, want_sum=False, out_dtype=BF16)[0]
    tm_f = _tile(t, 1024, 16)
    nblk = t // tm_f
    y_ffn = _ffn(hn, w_ffn_gate[None], w_ffn_up[None], w_ffn_down[None], jnp.ones((t, 1), F32),
                 jnp.zeros((nblk,), jnp.int32), jnp.full((nblk,), tm_f, jnp.int32), tm=tm_f)

    h2, hn = _add_norm([h1, y_ffn], norm_mix1, want_sum=True, out_dtype=BF16)
    (z,) = _mm(hn, w_in_c, _ep_id, [(_tile(w_in_c.shape[1], 1024, LANES), F32)], name="in_c")
    lb_cum = jnp.cumsum(jax.nn.softmax(lb_logits.astype(F32), axis=0), axis=0)
    lb = (lb_cum[1] - lb_cum[0]).reshape(1, c_f)
    gw = g_norm_w.reshape(1, -1)
    o_p, state_prompt = _hgrn_prompt(z, lb, gw, n_p, c_heads)
    o_s, state_sample = _hgrn_sample(z, state_hgrn, lb, gw, n_p)
    o_c = jnp.concatenate([o_p, o_s.astype(BF16)], axis=0)
    (d3,) = _mm(o_c, w_out_c, _ep_id, [(_tile(d, 1024, LANES), F32)], name="out_c")

    h3, hn_packed, comb, sel = _router(h2, d3, norm_ffn1, w_router)
    tm_e = _tile(t, 1024, 16)
    tok_of_slot, gate_of_slot, e_of_blk, valid, tok_slots = _route_plan(sel, comb, tm_e)
    rows_of = lambda a, idx: a.at[idx].get(mode="promise_in_bounds")
    y_sorted = _ffn(rows_of(hn_packed, tok_of_slot), w_exp_gate, w_exp_up, w_exp_down,
                    gate_of_slot[:, None], e_of_blk, valid, tm=tm_e)
    parts = [h3] + [rows_of(y_sorted, tok_slots[:, r]) for r in range(TOP_K)]
    y = _add_norm(parts, norm_final, want_sum=False, out_dtype=F32)[0]

    y_prompt = y[:n_p].reshape(bp, lp, d)
    y_sample = y[n_p:].reshape(nb, lq, d)
    return (y_prompt, y_sample,
            ckv[:n_p].reshape(bp, lp, kv_lora), kpe[:n_p].reshape(bp, lp, rope),
            ckv[n_p:].reshape(nb, lq, kv_lora), kpe[n_p:].reshape(nb, lq, rope),
            v[n_p:].reshape(nb, lq, a_w),
            state_prompt.reshape(bp, c_heads, c_k, -1), state_sample)
```

```python
import functools
import math

import jax
import jax.numpy as jnp
from jax import lax
from jax.experimental import pallas as pl
from jax.experimental.pallas import tpu as pltpu

F32 = jnp.float32
BF16 = jnp.bfloat16
EPS = 1e-6
ROPE_THETA = 10000.0
TOP_K = 2
VMEM_LIMIT_BYTES = 56 * 1024 * 1024
LANES = 128
SUBLANES = 8
HGRN_CHUNK = 64
HGRN_HEADS_PER_STEP = 4
ATTN_BQ = 256
ATTN_BK = 512
ATTN_ROW_GROUPS = 4
SAMPLE_SEQS_PER_STEP = 2
SAMPLE_PAGE_OPERANDS = 16
SOFTMAX_ROWS = 32
SOFTMAX_UNROLL = 8
NEG_INF = float("-inf")


def _params(*sem):
    return pltpu.CompilerParams(dimension_semantics=sem, vmem_limit_bytes=VMEM_LIMIT_BYTES)


def _tile(n, pref, mult=SUBLANES):
    if n <= pref:
        return n
    for t in range(pref, 0, -1):
        if n % t == 0 and t % mult == 0:
            return t
    raise ValueError(f"no tile for {n} <= {pref}")


def _dot(a, b):
    return jnp.dot(a, b, preferred_element_type=F32)


def _dot_nt(a, b):
    return lax.dot_general(a, b, (((1,), (1,)), ((), ())), preferred_element_type=F32)


def _rms(x, w):
    return x * lax.rsqrt(jnp.mean(x * x, axis=-1, keepdims=True) + EPS) * w


def _rope_rows(x, cos, sin):
    half = x.shape[-1] // 2
    x1, x2 = x[:, :half], x[:, half:]
    return jnp.concatenate([x1 * cos - x2 * sin, x1 * sin + x2 * cos], axis=-1)


def _add_norm_kernel(*refs, n_in, want_sum):
    x = refs[0][...]
    for r in refs[1:n_in]:
        x = x + r[...]
    g_ref = refs[n_in]
    outs = refs[n_in + 1:]
    k = 0
    if want_sum:
        outs[0][...] = x
        k = 1
    outs[k][...] = _rms(x, g_ref[...]).astype(outs[k].dtype)


def _add_norm(parts, gain, *, want_sum, out_dtype):
    t, d = parts[0].shape
    tm = _tile(t, 256)
    row = pl.BlockSpec((tm, d), lambda i: (i, 0))
    out_shape = []
    if want_sum:
        out_shape.append(jax.ShapeDtypeStruct((t, d), F32))
    out_shape.append(jax.ShapeDtypeStruct((t, d), out_dtype))
    return pl.pallas_call(
        functools.partial(_add_norm_kernel, n_in=len(parts), want_sum=want_sum),
        grid=(t // tm,),
        in_specs=[row] * len(parts) + [pl.BlockSpec((1, d), lambda i: (0, 0))],
        out_specs=[row] * len(out_shape),
        out_shape=out_shape,
        compiler_params=_params("parallel"),
        name="add_norm",
    )(*parts, gain.reshape(1, d))


def _mm_kernel(x_ref, w_ref, *rest, n_extra, n_out, epilogue):
    extras = rest[:n_extra]
    outs = rest[n_extra:n_extra + n_out]
    w_bf = rest[n_extra + n_out]

    @pl.when(pl.program_id(1) == 0)
    def _():
        w_bf[...] = w_ref[...].astype(BF16)

    acc = _dot(x_ref[...], w_bf[...])
    res = epilogue(acc, *[e[...] for e in extras])
    for o, r in zip(outs, res):
        o[...] = r.astype(o.dtype)


def _mm(x, w, epilogue, outs, extras=(), *, tm_pref=1024, tn_pref=1024, name="mm"):
    t, k = x.shape
    n = w.shape[1]
    tm = _tile(t, tm_pref, 16)
    tn = _tile(n, tn_pref, LANES)
    nj = n // tn
    in_specs = [pl.BlockSpec((tm, k), lambda j, i: (i, 0)),
                pl.BlockSpec((k, tn), lambda j, i: (0, j))]
    args = [x, w]
    for arr, kind in extras:
        if kind == "row":
            in_specs.append(pl.BlockSpec((tm, arr.shape[1]), lambda j, i: (i, 0)))
        elif kind == "col":
            in_specs.append(pl.BlockSpec((1, tn), lambda j, i: (0, j)))
        elif kind == "tile":
            in_specs.append(pl.BlockSpec((tm, tn), lambda j, i: (i, j)))
        else:
            raise ValueError(kind)
        args.append(arr)
    out_specs = [pl.BlockSpec((tm, wd), lambda j, i: (i, j)) for wd, _ in outs]
    out_shape = [jax.ShapeDtypeStruct((t, nj * wd), dt) for wd, dt in outs]
    res = pl.pallas_call(
        functools.partial(_mm_kernel, n_extra=len(extras), n_out=len(outs), epilogue=epilogue),
        grid=(nj, t // tm),
        in_specs=in_specs,
        out_specs=out_specs,
        out_shape=out_shape,
        scratch_shapes=[pltpu.VMEM((k, tn), BF16)],
        compiler_params=_params("arbitrary", "arbitrary"),
        name=name,
    )(*args)
    return res


def _ep_gelu(acc):
    return (jax.nn.gelu(acc),)


def _ep_gelu_ln(acc, w, b):
    g = jax.nn.gelu(acc)
    mu = jnp.mean(g, axis=-1, keepdims=True)
    c = g - mu
    var = jnp.mean(c * c, axis=-1, keepdims=True)
    return (c * lax.rsqrt(var + EPS) * w + b,)


def _ep_rms(acc, w):
    return (_rms(acc, w),)


def _ep_kv(acc, cos, sin, w, *, kv_lora):
    ckv = _rms(acc[:, :kv_lora], w)
    kpe = _rope_rows(acc[:, kv_lora:], cos, sin)
    return ckv, kpe, ckv, kpe


def _ep_add(acc, resid):
    return (acc + resid,)


def _ep_id(acc):
    return (acc,)


def _q_proj_kernel(cq_ref, wqn_ref, wqr_ref, wabs_ref, cos_ref, sin_ref, ql_ref, qp_ref, *, heads, nope, rope, scale):
    cq = cq_ref[...]
    qn = _dot(cq, wqn_ref[...])
    qr = _dot(cq, wqr_ref[...])
    cos = cos_ref[...]
    sin = sin_ref[...]
    for h in range(heads):
        lat = _dot(qn[:, h * nope:(h + 1) * nope].astype(BF16), wabs_ref[h])
        ql_ref[h] = (lat * scale).astype(ql_ref.dtype)
        qp_ref[h] = (_rope_rows(qr[:, h * rope:(h + 1) * rope], cos, sin) * scale).astype(qp_ref.dtype)


def _q_proj(cq, wqn, wqr, wabs, cos, sin, scale):
    t, ql = cq.shape
    heads, nope, kv = wabs.shape
    rope = wqr.shape[1] // heads
    tm = _tile(t, 512, 16)
    const2 = lambda i: (0, 0)
    return pl.pallas_call(
        functools.partial(_q_proj_kernel, heads=heads, nope=nope, rope=rope, scale=scale),
        grid=(t // tm,),
        in_specs=[pl.BlockSpec((tm, ql), lambda i: (i, 0)),
                  pl.BlockSpec(wqn.shape, const2),
                  pl.BlockSpec(wqr.shape, const2),
                  pl.BlockSpec(wabs.shape, lambda i: (0, 0, 0)),
                  pl.BlockSpec((tm, rope // 2), lambda i: (i, 0)),
                  pl.BlockSpec((tm, rope // 2), lambda i: (i, 0))],
        out_specs=[pl.BlockSpec((heads, tm, kv), lambda i: (0, i, 0)),
                   pl.BlockSpec((heads, tm, rope), lambda i: (0, i, 0))],
        out_shape=[jax.ShapeDtypeStruct((heads, t, kv), BF16),
                   jax.ShapeDtypeStruct((heads, t, rope), BF16)],
        compiler_params=_params("parallel"),
        name="q_proj",
    )(cq, wqn, wqr, wabs, cos, sin)


def _gmlp_kernel(u_ref, v_ref, m_ref, b_ref, a_ref, *, groups, gd):
    for g in range(groups):
        sl = slice(g * gd, (g + 1) * gd)
        mixed = _dot(m_ref[0, g], v_ref[:, sl].astype(BF16)) + b_ref[0, :, sl]
        a_ref[:, sl] = (u_ref[:, sl].astype(F32) * mixed).astype(a_ref.dtype)


def _gmlp(u, v, mats, bias, n_first):
    t, a = u.shape
    _, groups, c, _ = mats.shape
    sel = lambda n: (jnp.where(n < n_first, 0, 1), 0, 0)
    return pl.pallas_call(
        functools.partial(_gmlp_kernel, groups=groups, gd=a // groups),
        grid=(t // c,),
        in_specs=[pl.BlockSpec((c, a), lambda n: (n, 0)),
                  pl.BlockSpec((c, a), lambda n: (n, 0)),
                  pl.BlockSpec((1, groups, c, c), lambda n: (jnp.where(n < n_first, 0, 1), 0, 0, 0)),
                  pl.BlockSpec((1, c, a), sel)],
        out_specs=pl.BlockSpec((c, a), lambda n: (n, 0)),
        out_shape=jax.ShapeDtypeStruct((t, a), BF16),
        compiler_params=_params("parallel"),
        name="gmlp",
    )(u, v, mats, bias)


def _softmax_rows(s_ref, p_ref, a_ref, m_ref, l_ref, rc, mask_fn=None, static_rows=None):
    width = s_ref.shape[1]

    def chunk(rs, row0):
        s = s_ref[rs, :]
        if mask_fn is not None:
            s = mask_fn(s, row0)
        m_prev = m_ref[rs, :]
        m_new = jnp.maximum(m_prev, jnp.max(s, axis=-1, keepdims=True))
        alpha = jnp.exp(m_prev - m_new)
        p = jnp.exp(s - jnp.tile(m_new, (1, width // LANES)))
        l_ref[rs, :] = alpha * l_ref[rs, :] + jnp.sum(p, axis=-1, keepdims=True)
        m_ref[rs, :] = m_new
        a_ref[rs, :] = alpha
        p_ref[rs, :] = p.astype(p_ref.dtype)

    if static_rows is not None:
        for row0 in range(static_rows[0], static_rows[1], rc):
            chunk(slice(row0, row0 + rc), row0)
        return

    def body(r, carry):
        chunk(pl.ds(pl.multiple_of(r * rc, rc), rc), r * rc)
        return carry

    n = s_ref.shape[0] // rc
    lax.fori_loop(0, n, body, 0, unroll=math.gcd(n, SOFTMAX_UNROLL))


def _rescale_add(acc_ref, a_ref, pv):
    acc_ref[...] = acc_ref[...] * jnp.tile(a_ref[...], (1, acc_ref.shape[1] // LANES)) + pv


def _attn_prompt_kernel(ql_ref, qp_ref, kc_ref, kp_ref, wv_ref, o_ref, m_ref, l_ref, acc_ref, s_ref, p_ref, a_ref,
                        *, heads, bq, bk, vd):
    i = pl.program_id(0)
    j = pl.program_id(1)
    rows = heads * bq

    @pl.when(j == 0)
    def _():
        m_ref[...] = jnp.full(m_ref.shape, NEG_INF, F32)
        l_ref[...] = jnp.zeros(l_ref.shape, F32)
        acc_ref[...] = jnp.zeros(acc_ref.shape, F32)

    def causal(s, row0):
        q_pos = i * bq + (row0 + lax.broadcasted_iota(jnp.int32, s.shape, 0)) % bq
        k_pos = j * bk + lax.broadcasted_iota(jnp.int32, s.shape, 1)
        return jnp.where(k_pos <= q_pos, s, NEG_INF)

    def step(mask_fn):
        hg = heads // math.gcd(heads, ATTN_ROW_GROUPS)
        gr = hg * bq
        kc = kc_ref[...]
        for g in range(heads // hg):
            q = ql_ref[g * hg:(g + 1) * hg].reshape(gr, ql_ref.shape[-1])
            qp = qp_ref[g * hg:(g + 1) * hg].reshape(gr, qp_ref.shape[-1])
            s_ref[g * gr:(g + 1) * gr, :] = _dot_nt(q, kc) + _dot_nt(qp, kp_ref[...])
        for g in range(heads // hg):
            sl = slice(g * gr, (g + 1) * gr)
            _softmax_rows(s_ref, p_ref, a_ref, m_ref, l_ref, SOFTMAX_ROWS, mask_fn, static_rows=(g * gr, (g + 1) * gr))
            acc_ref[sl, :] = (acc_ref[sl, :] * jnp.tile(a_ref[sl, :], (1, acc_ref.shape[1] // LANES))
                              + _dot(p_ref[sl, :], kc))

    needed = j * bk <= i * bq + bq - 1
    crosses_diagonal = j * bk + bk - 1 > i * bq

    @pl.when(needed & crosses_diagonal)
    def _():
        step(causal)

    @pl.when(needed & jnp.logical_not(crosses_diagonal))
    def _():
        step(None)

    @pl.when(j == pl.num_programs(1) - 1)
    def _():
        o = (acc_ref[...] / l_ref[...][:, :1]).astype(BF16)
        for h in range(heads):
            o_ref[:, h * vd:(h + 1) * vd] = _dot(o[h * bq:(h + 1) * bq], wv_ref[h]).astype(o_ref.dtype)


def _attn_prompt(ql, qp, kc, kp, wv, n_prompt):
    heads, _, kv = ql.shape
    rope = qp.shape[-1]
    vd = wv.shape[-1]
    bq = _tile(n_prompt, ATTN_BQ, SOFTMAX_ROWS)
    bk = _tile(n_prompt, ATTN_BK, LANES)
    nq, nk = n_prompt // bq, n_prompt // bk
    rows = heads * bq

    def kmap(i, j):
        return (jnp.minimum(j, (i * bq + bq - 1) // bk), 0)

    return pl.pallas_call(
        functools.partial(_attn_prompt_kernel, heads=heads, bq=bq, bk=bk, vd=vd),
        grid=(nq, nk),
        in_specs=[pl.BlockSpec((heads, bq, kv), lambda i, j: (0, i, 0)),
                  pl.BlockSpec((heads, bq, rope), lambda i, j: (0, i, 0)),
                  pl.BlockSpec((bk, kv), kmap),
                  pl.BlockSpec((bk, rope), kmap),
                  pl.BlockSpec(wv.shape, lambda i, j: (0, 0, 0))],
        out_specs=pl.BlockSpec((bq, heads * vd), lambda i, j: (i, 0)),
        out_shape=jax.ShapeDtypeStruct((n_prompt, heads * vd), BF16),
        scratch_shapes=[pltpu.VMEM((rows, LANES), F32), pltpu.VMEM((rows, LANES), F32),
                        pltpu.VMEM((rows, kv), F32), pltpu.VMEM((rows, bk), F32),
                        pltpu.VMEM((rows, bk), BF16), pltpu.VMEM((rows, LANES), F32)],
        compiler_params=_params("parallel", "arbitrary"),
        name="attn_prompt",
    )(ql, qp, kc, kp, wv)


def _attn_sample_kernel(pt_ref, ql_ref, qp_ref, kcn_ref, kpn_ref, wv_ref, *rest, heads, lq, seqs, pages, page, vd):
    n_ops = seqs * pages
    ckv_pages = rest[:n_ops]
    kpe_pages = rest[n_ops:2 * n_ops]
    o_ref, kc_s, kp_s, m_ref, l_ref, acc_ref, s_ref, p_ref, a_ref = rest[2 * n_ops:]
    j = pl.program_id(1)

    @pl.when(j == 0)
    def _():
        for e in range(seqs):
            kcn = kcn_ref[e]
            s = _dot_nt(ql_ref[e].astype(F32), kcn) + _dot_nt(qp_ref[e].astype(F32), kpn_ref[e])
            t_q = lax.broadcasted_iota(jnp.int32, s.shape, 0) % lq
            t_k = lax.broadcasted_iota(jnp.int32, s.shape, 1)
            s = jnp.where(t_k <= t_q, s, NEG_INF)
            m0 = jnp.max(s, axis=-1, keepdims=True)
            p = jnp.exp(s - m0)
            m_ref[e] = jnp.broadcast_to(m0, m_ref.shape[1:])
            l_ref[e] = jnp.broadcast_to(jnp.sum(p, axis=-1, keepdims=True), l_ref.shape[1:])
            acc_ref[e] = _dot(p, kcn)

    for e in range(seqs):
        for r in range(pages):
            kc_s[e, r * page:(r + 1) * page, :] = ckv_pages[e * pages + r][0].astype(BF16)
            kp_s[e, :, r * page:(r + 1) * page] = kpe_pages[e * pages + r][0].astype(BF16)
        s_ref[e] = _dot_nt(ql_ref[e], kc_s[e]) + _dot(qp_ref[e], kp_s[e])
    for e in range(seqs):
        _softmax_rows(s_ref.at[e], p_ref.at[e], a_ref.at[e], m_ref.at[e], l_ref.at[e], 2 * SUBLANES,
                      static_rows=(0, s_ref.shape[1]))
        _rescale_add(acc_ref.at[e], a_ref.at[e], _dot(p_ref[e], kc_s[e]))

    @pl.when(j == pl.num_programs(1) - 1)
    def _():
        for e in range(seqs):
            o = (acc_ref[e] / l_ref[e][:, :1]).astype(BF16)
            for h in range(heads):
                full = _dot(o, wv_ref[h])
                o_ref[e, :, h * vd:(h + 1) * vd] = full[h * lq:(h + 1) * lq].astype(o_ref.dtype)


def _attn_sample(page_table, ql, qp, kc_new, kp_new, wv, cache_ckv, cache_kpe_t):
    nb, rows, kv = ql.shape
    rope = qp.shape[-1]
    heads, _, vd = wv.shape
    lq = rows // heads
    n_pages = page_table.shape[1]
    page = cache_ckv.shape[1]
    seqs = math.gcd(nb, SAMPLE_SEQS_PER_STEP)
    pages = _tile(n_pages, SAMPLE_PAGE_OPERANDS // seqs, 1)
    steps = n_pages // pages

    def page_spec(shape, e, r):
        return pl.BlockSpec((1,) + shape, lambda b, j, pt: (pt[b * seqs + e, j * pages + r], 0, 0))

    per_seq = lambda *shape: pl.BlockSpec((seqs,) + shape, lambda b, j, pt: (b, 0, 0))
    in_specs = [per_seq(rows, kv), per_seq(rows, rope), per_seq(lq, kv), per_seq(lq, rope),
                pl.BlockSpec(wv.shape, lambda b, j, pt: (0, 0, 0))]
    in_specs += [page_spec((page, kv), e, r) for e in range(seqs) for r in range(pages)]
    in_specs += [page_spec((rope, page), e, r) for e in range(seqs) for r in range(pages)]
    keys = pages * page
    grid_spec = pltpu.PrefetchScalarGridSpec(
        num_scalar_prefetch=1,
        grid=(nb // seqs, steps),
        in_specs=in_specs,
        out_specs=per_seq(lq, heads * vd),
        scratch_shapes=[pltpu.VMEM((seqs, keys, kv), BF16), pltpu.VMEM((seqs, rope, keys), BF16),
                        pltpu.VMEM((seqs, rows, LANES), F32), pltpu.VMEM((seqs, rows, LANES), F32),
                        pltpu.VMEM((seqs, rows, kv), F32), pltpu.VMEM((seqs, rows, keys), F32),
                        pltpu.VMEM((seqs, rows, keys), BF16), pltpu.VMEM((seqs, rows, LANES), F32)],
    )
    n_ops = seqs * pages
    return pl.pallas_call(
        functools.partial(_attn_sample_kernel, heads=heads, lq=lq, seqs=seqs, pages=pages, page=page, vd=vd),
        grid_spec=grid_spec,
        out_shape=jax.ShapeDtypeStruct((nb, lq, heads * vd), BF16),
        compiler_params=_params("parallel", "arbitrary"),
        name="attn_sample",
    )(page_table, ql, qp, kc_new, kp_new, wv, *([cache_ckv] * n_ops), *([cache_kpe_t] * n_ops))


def _cast_kernel(x_ref, o_ref):
    o_ref[...] = x_ref[...].astype(o_ref.dtype)


def _cast_rows(x, dtype):
    s, d = x.shape
    tm = _tile(s, 512, 16)
    row = pl.BlockSpec((tm, d), lambda i: (i, 0))
    return pl.pallas_call(
        _cast_kernel, grid=(s // tm,), in_specs=[row], out_specs=row,
        out_shape=jax.ShapeDtypeStruct((s, d), dtype), compiler_params=_params("parallel"), name="cast_rows",
    )(x)


def _ffn_kernel(be_ref, bv_ref, x_ref, wg_ref, wu_ref, wd_ref, gate_ref, o_ref, wg_s, wu_s, wd_s, *, tm, ts):
    i = pl.program_id(0)
    j = pl.program_id(1)
    valid = bv_ref[i]

    @pl.when(j == 0)
    def _():
        o_ref[...] = jnp.zeros(o_ref.shape, F32)

    @pl.when(valid > 0)
    def _():
        wg_s[...] = wg_ref[0].astype(BF16)
        wu_s[...] = wu_ref[0].astype(BF16)
        wd_s[...] = wd_ref[0].astype(BF16)

    def swiglu(rows):
        x = x_ref[rows, :]
        h = (jax.nn.silu(_dot(x, wg_s[...])) * _dot(x, wu_s[...])).astype(BF16)
        o_ref[rows, :] += _dot(h, wd_s[...])

    @pl.when(valid == tm)
    def _():
        swiglu(slice(None))

    for sub in range(tm // ts):
        @pl.when((valid > sub * ts) & (valid < tm))
        def _():
            swiglu(slice(sub * ts, (sub + 1) * ts))

    @pl.when(j == pl.num_programs(1) - 1)
    def _():
        o_ref[...] = o_ref[...] * gate_ref[...]


def _ffn(x, wg, wu, wd, gate, blk_expert, blk_valid, *, tm, tf_pref=512, ts_pref=256):
    s, d = x.shape
    f = wg.shape[2]
    tf = _tile(f, tf_pref, LANES)
    ts = _tile(tm, ts_pref, 16)
    nj = f // tf

    def jj(i, j, bv):
        return jnp.where(bv[i] > 0, j, nj - 1)

    once = pl.Buffered(1)
    scratch = [pltpu.VMEM((d, tf), BF16), pltpu.VMEM((d, tf), BF16), pltpu.VMEM((tf, d), BF16)]
    grid_spec = pltpu.PrefetchScalarGridSpec(
        num_scalar_prefetch=2,
        grid=(s // tm, nj),
        in_specs=[pl.BlockSpec((tm, d), lambda i, j, be, bv: (i, 0), pipeline_mode=once),
                  pl.BlockSpec((1, d, tf), lambda i, j, be, bv: (be[i], 0, jj(i, j, bv))),
                  pl.BlockSpec((1, d, tf), lambda i, j, be, bv: (be[i], 0, jj(i, j, bv))),
                  pl.BlockSpec((1, tf, d), lambda i, j, be, bv: (be[i], jj(i, j, bv), 0)),
                  pl.BlockSpec((tm, 1), lambda i, j, be, bv: (i, 0), pipeline_mode=once)],
        out_specs=pl.BlockSpec((tm, d), lambda i, j, be, bv: (i, 0), pipeline_mode=once),
        scratch_shapes=scratch,
    )
    return pl.pallas_call(
        functools.partial(_ffn_kernel, tm=tm, ts=ts),
        grid_spec=grid_spec,
        out_shape=jax.ShapeDtypeStruct((s, d), F32),
        compiler_params=_params("arbitrary", "arbitrary"),
        name="ffn",
    )(blk_expert, blk_valid, x, wg, wu, wd, gate)


def _split3(a):
    a1 = a.astype(BF16)
    r1 = a - a1.astype(F32)
    a2 = r1.astype(BF16)
    a3 = (r1 - a2.astype(F32)).astype(BF16)
    return a1, a2, a3


def _router_kernel(h_ref, d_ref, g_ref, wr_ref, hs_ref, hn_ref, comb_ref, sel_ref):
    x = h_ref[...] + d_ref[...]
    hs_ref[...] = x
    y = _rms(x, g_ref[...])
    hn_ref[...] = y
    ys = _split3(y)
    ws = _split3(wr_ref[...])
    logits = jnp.zeros((x.shape[0], wr_ref.shape[1]), F32)
    for a in range(3):
        for b in range(3 - a):
            logits = logits + _dot(ys[a], ws[b])
    n_e = logits.shape[1]
    lane = lax.broadcasted_iota(jnp.int32, logits.shape, 1).astype(F32)
    m1 = jnp.max(logits, axis=-1, keepdims=True)
    i1 = jnp.min(jnp.where(logits == m1, lane, n_e), axis=-1, keepdims=True)
    first = lane == i1
    rest = jnp.where(first, NEG_INF, logits)
    m2 = jnp.max(rest, axis=-1, keepdims=True)
    i2 = jnp.min(jnp.where(rest == m2, lane, n_e), axis=-1, keepdims=True)
    second = lane == i2
    e = jnp.exp(m2 - m1)
    den = 1.0 + e
    comb_ref[...] = jnp.where(first, 1.0 / den, 0.0) + jnp.where(second, e / den, 0.0)
    sel_ref[...] = jnp.where(first | second, 1.0, 0.0)


def _router(h, delta, gain, w_router):
    t, d = h.shape
    n_e = w_router.shape[1]
    tm = _tile(t, 256)
    row = pl.BlockSpec((tm, d), lambda i: (i, 0))
    small = pl.BlockSpec((tm, n_e), lambda i: (i, 0))
    return pl.pallas_call(
        _router_kernel,
        grid=(t // tm,),
        in_specs=[row, row, pl.BlockSpec((1, d), lambda i: (0, 0)), pl.BlockSpec((d, n_e), lambda i: (0, 0))],
        out_specs=[row, row, small, small],
        out_shape=[jax.ShapeDtypeStruct((t, d), F32), jax.ShapeDtypeStruct((t, d), F32),
                   jax.ShapeDtypeStruct((t, n_e), F32), jax.ShapeDtypeStruct((t, n_e), F32)],
        compiler_params=_params("parallel"),
        name="router",
    )(h, delta, gain.reshape(1, d), w_router)


def _roll_rows(x, shift):
    n = x.shape[0]
    shift = shift % n
    return x if shift == 0 else pltpu.roll(x, shift, 0)


def _hgrn_inputs(zq, zf, lb):
    q = jax.nn.silu(zq)
    lf = jnp.log(lb + (1.0 - lb) * jax.nn.sigmoid(zf))
    k = (1.0 - lb) * jax.nn.sigmoid(-zf)
    return q, k, lf


def _prefix8(lf, r8):
    p = lf
    for sh in (1, 2, 4):
        p = p + jnp.where(r8 >= sh, _roll_rows(p, sh), 0.0)
    return p


def _block8_intra(q, k, v, p8, r8):
    o = jnp.sum(q * k, axis=-1, keepdims=True) * v
    for d in range(1, SUBLANES):
        e = jnp.exp(jnp.where(r8 >= d, p8 - _roll_rows(p8, d), NEG_INF))
        w = jnp.sum(q * _roll_rows(k, d) * e, axis=-1, keepdims=True)
        o = o + w * _roll_rows(v, d)
    return o


def _hgrn_finish(o, zg, gw):
    return _rms(o, gw) * jax.nn.silu(zg)


def _hgrn_prompt_kernel(zq_ref, zf_ref, zi_ref, zg_ref, lb_ref, gw_ref, o_ref, st_ref, st_t, *, c, n_chunks, hp, kd):
    @pl.when(pl.program_id(1) == 0)
    def _():
        st_t[...] = jnp.zeros(st_t.shape, F32)

    gw = gw_ref[...]
    row = lax.broadcasted_iota(jnp.int32, (c, kd), 0)
    r8 = row % SUBLANES
    ri = lax.broadcasted_iota(jnp.int32, (c, c), 0)
    ci = lax.broadcasted_iota(jnp.int32, (c, c), 1)

    def one_head(rows, hh):
        cols = slice(hh * kd, (hh + 1) * kd)
        q, k, lf = _hgrn_inputs(zq_ref[rows, cols], zf_ref[rows, cols], lb_ref[:, cols])
        v = zi_ref[rows, cols]
        p = _prefix8(lf, r8)
        o = _block8_intra(q, k, v, p, r8)
        tot = jnp.where(r8 == SUBLANES - 1, p, 0.0)
        for sh in (1, 2, 4):
            tot = tot + _roll_rows(tot, -sh)
        a = jnp.zeros((c, c), F32)
        s = SUBLANES
        while s < c:
            second = row % (2 * s) >= s
            qs = (q * jnp.exp(jnp.where(second, p, NEG_INF))).astype(BF16)
            ks = (k * jnp.exp(jnp.where(second, NEG_INF, tot - p))).astype(BF16)
            a_s = _dot_nt(qs, ks)
            a = a + (a_s if 2 * s == c else jnp.where(ri // (2 * s) == ci // (2 * s), a_s, 0.0))
            prev = _roll_rows(tot, s)
            p = p + jnp.where(second, prev, 0.0)
            tot = tot + jnp.where(second, prev, _roll_rows(tot, -s))
            s *= 2
        st = st_t[hh]
        o = o + _dot(a.astype(BF16), v.astype(BF16)) + _dot_nt((q * jnp.exp(p)).astype(BF16), st.astype(BF16))
        kb = (k * jnp.exp(tot - p)).astype(BF16)
        st_t[hh] = st * jnp.exp(tot[0:1, :]) + _dot(v.T.astype(BF16), kb)
        o_ref[rows, cols] = _hgrn_finish(o, zg_ref[rows, cols], gw).astype(o_ref.dtype)

    def chunk(n, carry):
        rows = pl.ds(pl.multiple_of(n * c, c), c)
        for hh in range(hp):
            one_head(rows, hh)
        return carry

    lax.fori_loop(0, n_chunks, chunk, 0)

    @pl.when(pl.program_id(1) == pl.num_programs(1) - 1)
    def _():
        for hh in range(hp):
            st_ref[hh] = st_t[hh].T


def _hgrn_prompt(z, lb, gw, n_prompt, heads):
    kd = lb.shape[1] // heads
    c = HGRN_CHUNK
    hp = math.gcd(heads, HGRN_HEADS_PER_STEP)
    lbk = _tile(n_prompt, 1024, c)
    groups = heads // hp

    def zspec(seg):
        return pl.BlockSpec((lbk, hp * kd), lambda h, l: (l, seg * groups + h))

    return pl.pallas_call(
        functools.partial(_hgrn_prompt_kernel, c=c, n_chunks=lbk // c, hp=hp, kd=kd),
        grid=(groups, n_prompt // lbk),
        in_specs=[zspec(0), zspec(1), zspec(2), zspec(3),
                  pl.BlockSpec((1, hp * kd), lambda h, l: (0, h)),
                  pl.BlockSpec((1, kd), lambda h, l: (0, 0))],
        out_specs=[pl.BlockSpec((lbk, hp * kd), lambda h, l: (l, h)),
                   pl.BlockSpec((hp, kd, kd), lambda h, l: (h, 0, 0))],
        out_shape=[jax.ShapeDtypeStruct((n_prompt, heads * kd), BF16),
                   jax.ShapeDtypeStruct((heads, kd, kd), F32)],
        scratch_shapes=[pltpu.VMEM((hp, kd, kd), F32)],
        compiler_params=_params("parallel", "arbitrary"),
        name="hgrn_prompt",
    )(z, z, z, z, lb, gw)


def _hgrn_sample_kernel(z_ref, s0_ref, lb_ref, gw_ref, o_ref, s1_ref, *, heads, kd):
    lq = z_ref.shape[0]
    gw = gw_ref[...]
    r8 = lax.broadcasted_iota(jnp.int32, (lq, kd), 0)
    for h in range(heads):
        col = lambda seg: slice((seg * heads + h) * kd, (seg * heads + h + 1) * kd)
        q, k, lf = _hgrn_inputs(z_ref[:, col(0)], z_ref[:, col(1)], lb_ref[:, h * kd:(h + 1) * kd])
        v = z_ref[:, col(2)]
        p = _prefix8(lf, r8)
        last = p[lq - 1:lq, :]
        st = s0_ref[0, h]
        o = _block8_intra(q, k, v, p, r8) + _dot((q * jnp.exp(p)).astype(BF16), st.astype(BF16))
        cols = jnp.concatenate([k * jnp.exp(last - p), p], axis=0).T
        v_pad = jnp.concatenate([v, jnp.zeros_like(v)], axis=0)
        s1_ref[0, h] = st * jnp.exp(cols[:, 2 * lq - 1:2 * lq]) + _dot(cols.astype(BF16), v_pad.astype(BF16))
        o_ref[:, h * kd:(h + 1) * kd] = _hgrn_finish(o, z_ref[:, col(3)], gw).astype(o_ref.dtype)


def _hgrn_sample(z, state, lb, gw, row0):
    nb, heads, kd, _ = state.shape
    lq = SUBLANES
    blk0 = row0 // lq
    return pl.pallas_call(
        functools.partial(_hgrn_sample_kernel, heads=heads, kd=kd),
        grid=(nb,),
        in_specs=[pl.BlockSpec((lq, z.shape[1]), lambda b: (blk0 + b, 0)),
                  pl.BlockSpec((1, heads, kd, kd), lambda b: (b, 0, 0, 0)),
                  pl.BlockSpec((1, heads * kd), lambda b: (0, 0)),
                  pl.BlockSpec((1, kd), lambda b: (0, 0))],
        out_specs=[pl.BlockSpec((lq, heads * kd), lambda b: (b, 0)),
                   pl.BlockSpec((1, heads, kd, kd), lambda b: (b, 0, 0, 0))],
        out_shape=[jax.ShapeDtypeStruct((nb * lq, heads * kd), F32),
                   jax.ShapeDtypeStruct(state.shape, F32)],
        compiler_params=_params("parallel"),
        name="hgrn_sample",
    )(z, state, lb, gw)


def _route_plan(sel, comb, tm):
    t, n_e = sel.shape
    n_blk = -(-(TOP_K * t) // tm) + n_e
    n_slot = n_blk * tm
    picked = sel > 0
    seli = picked.astype(jnp.int32)
    incl = jnp.cumsum(seli, axis=0)
    counts = incl[-1]
    blocks_per = (counts + tm - 1) // tm
    blk_end = jnp.cumsum(blocks_per)
    start = (blk_end - blocks_per) * tm
    slot = start[None, :] + incl - seli
    slot_a = jnp.min(jnp.where(picked, slot, n_slot), axis=1)
    slot_b = jnp.max(jnp.where(picked, slot, -1), axis=1)
    tok_slots = jnp.stack([slot_a, slot_b], axis=1)
    gates = jnp.stack([jnp.sum(jnp.where(picked & (slot == s[:, None]), comb, 0.0), axis=1)
                       for s in (slot_a, slot_b)], axis=1).reshape(-1)
    pair = jnp.arange(TOP_K * t, dtype=jnp.int32)
    pair_of_slot = jnp.full((n_slot,), -1, jnp.int32).at[tok_slots.reshape(-1)].set(pair, unique_indices=True)
    filled = pair_of_slot >= 0
    safe = jnp.maximum(pair_of_slot, 0)
    tok_of_slot = safe // TOP_K
    gate_of_slot = jnp.where(filled, gates.at[safe].get(mode="promise_in_bounds"), 0.0)
    blk = jnp.arange(n_blk, dtype=jnp.int32)
    used = blk_end[-1]
    e_of_blk = jnp.sum((jnp.minimum(blk, used - 1)[:, None] >= blk_end[None, :]).astype(jnp.int32), axis=1)
    e_of_blk = jnp.minimum(e_of_blk, n_e - 1)
    valid = jnp.clip(counts[e_of_blk] - (blk * tm - start[e_of_blk]), 0, tm)
    valid = jnp.where(blk < used, valid, 0).astype(jnp.int32)
    return tok_of_slot, gate_of_slot, e_of_blk, valid, tok_slots


def kernel(x_prompt, x_sample, cache_ckv, cache_kpe, state_hgrn, page_table, norm_mix0, w_in_ab, ln_v_w, ln_v_b, w_s, b_s, q_norm_w, w_q_b, kv_norm_w, w_kv_b, w_out_ab, norm_ffn0, w_ffn_gate, w_ffn_up, w_ffn_down, norm_mix1, w_in_c, lb_logits, g_norm_w, w_out_c, norm_ffn1, w_router, w_exp_gate, w_exp_up, w_exp_down, norm_final):
    bp, lp, d = x_prompt.shape
    nb, lq, _ = x_sample.shape
    assert bp == 1, "the prompt group is one sequence"
    n_p, n_s = bp * lp, nb * lq
    t = n_p + n_s
    groups, chunk, _ = w_s.shape
    a_w = ln_v_w.shape[0]
    q_lora, heads, qk = w_q_b.shape
    kv_lora = kv_norm_w.shape[0]
    rope = cache_kpe.shape[-1]
    nope = qk - rope
    vd = w_kv_b.shape[-1] - nope
    page = cache_ckv.shape[1]
    past = page_table.shape[1] * page
    c_heads, c_k = state_hgrn.shape[1], state_hgrn.shape[2]
    c_f = c_heads * c_k
    assert lq == SUBLANES and chunk % lq == 0 and n_p % chunk == 0 and n_s % chunk == 0
    off_q, off_kv = 2 * a_w, 2 * a_w + q_lora
    scale = float(qk) ** -0.5

    x = jnp.concatenate([x_prompt.reshape(n_p, d), x_sample.reshape(n_s, d)], axis=0)

    pos = jnp.concatenate([jnp.arange(lp, dtype=F32), jnp.tile(past + jnp.arange(lq, dtype=F32), nb)])
    inv_freq = ROPE_THETA ** (-jnp.arange(rope // 2, dtype=F32) / (rope // 2))
    ang = pos[:, None] * inv_freq[None, :]
    cos, sin = jnp.cos(ang), jnp.sin(ang)

    hn = _add_norm([x], norm_mix0, want_sum=False, out_dtype=BF16)[0]
    (u,) = _mm(hn, w_in_ab[:, :a_w], _ep_gelu, [(a_w, BF16)], name="in_u")
    (v,) = _mm(hn, w_in_ab[:, a_w:off_q], _ep_gelu_ln, [(a_w, F32)],
               [(ln_v_w.reshape(1, a_w), "col"), (ln_v_b.reshape(1, a_w), "col")], name="in_v")
    (cq,) = _mm(hn, w_in_ab[:, off_q:off_kv], _ep_rms, [(q_lora, BF16)],
                [(q_norm_w.reshape(1, q_lora), "col")], name="in_cq")
    kvw = kv_lora + rope
    w_kvpe = w_in_ab[:, off_kv:]
    t_kv = _tile(t, 512, 16)
    ckv, kpe, ckv_b, kpe_b = pl.pallas_call(
        functools.partial(_mm_kernel, n_extra=3, n_out=4, epilogue=functools.partial(_ep_kv, kv_lora=kv_lora)),
        grid=(1, t // t_kv),
        in_specs=[pl.BlockSpec((t_kv, d), lambda j, i: (i, 0)),
                  pl.BlockSpec((d, kvw), lambda j, i: (0, 0)),
                  pl.BlockSpec((t_kv, rope // 2), lambda j, i: (i, 0)),
                  pl.BlockSpec((t_kv, rope // 2), lambda j, i: (i, 0)),
                  pl.BlockSpec((1, kv_lora), lambda j, i: (0, 0))],
        out_specs=[pl.BlockSpec((t_kv, kv_lora), lambda j, i: (i, 0)),
                   pl.BlockSpec((t_kv, rope), lambda j, i: (i, 0)),
                   pl.BlockSpec((t_kv, kv_lora), lambda j, i: (i, 0)),
                   pl.BlockSpec((t_kv, rope), lambda j, i: (i, 0))],
        out_shape=[jax.ShapeDtypeStruct((t, kv_lora), F32), jax.ShapeDtypeStruct((t, rope), F32),
                   jax.ShapeDtypeStruct((t, kv_lora), BF16), jax.ShapeDtypeStruct((t, rope), BF16)],
        scratch_shapes=[pltpu.VMEM((d, kvw), BF16)],
        compiler_params=_params("arbitrary", "arbitrary"),
        name="in_kv",
    )(hn, w_kvpe, cos, sin, kv_norm_w.reshape(1, kv_lora))

    wqn = w_q_b[:, :, :nope].reshape(q_lora, heads * nope).astype(BF16)
    wqr = w_q_b[:, :, nope:].reshape(q_lora, heads * rope).astype(BF16)
    wabs = jnp.transpose(w_kv_b[:, :, :nope], (1, 2, 0)).astype(BF16)
    wv = jnp.transpose(w_kv_b[:, :, nope:], (1, 0, 2)).astype(BF16)
    ql, qp = _q_proj(cq, wqn, wqr, wabs, cos, sin, scale)

    o_prompt = _attn_prompt(ql, qp, ckv_b, kpe_b, wv, n_p)
    to_rows = lambda a: jnp.transpose(a[:, n_p:].reshape(heads, nb, lq, a.shape[-1]), (1, 0, 2, 3)).reshape(
        nb, heads * lq, a.shape[-1])
    o_sample = _attn_sample(page_table, to_rows(ql), to_rows(qp),
                            ckv[n_p:].reshape(nb, lq, kv_lora), kpe[n_p:].reshape(nb, lq, rope),
                            wv, cache_ckv, jnp.swapaxes(cache_kpe, 1, 2))

    tril = jnp.tril(jnp.ones((chunk, chunk), dtype=bool))
    m_prompt = jnp.where(tril[None], w_s, 0.0)
    eye = jnp.eye(chunk // lq, dtype=F32)
    m_sample = jnp.einsum("ab,gts->gatbs", eye, m_prompt[:, :lq, :lq]).reshape(groups, chunk, chunk)
    mats = jnp.stack([m_prompt, m_sample]).astype(BF16)
    bias_p = jnp.repeat(b_s.T, a_w // groups, axis=1)
    bias = jnp.stack([bias_p, jnp.tile(bias_p[:lq], (chunk // lq, 1))])
    a_mix = _gmlp(u, v, mats, bias, n_p // chunk)

    mix = jnp.concatenate([a_mix, jnp.concatenate([o_prompt, o_sample.reshape(n_s, heads * vd)], axis=0)], axis=1)
    (h1,) = _mm(mix, w_out_ab, _ep_add, [(_tile(d, 1024, LANES), F32)], [(x, "tile")], name="out_ab")

    hn = _add_norm([h1], norm_ffn0, want_sum=False, out_dtype=BF16)[0]
    tm_f = _tile(t, 1024, 16)
    nblk = t // tm_f
    y_ffn = _ffn(hn, w_ffn_gate[None], w_ffn_up[None], w_ffn_down[None], jnp.ones((t, 1), F32),
                 jnp.zeros((nblk,), jnp.int32), jnp.full((nblk,), tm_f, jnp.int32), tm=tm_f)

    h2, hn = _add_norm([h1, y_ffn], norm_mix1, want_sum=True, out_dtype=BF16)
    (z,) = _mm(hn, w_in_c, _ep_id, [(_tile(w_in_c.shape[1], 1024, LANES), F32)], name="in_c")
    lb_cum = jnp.cumsum(jax.nn.softmax(lb_logits.astype(F32), axis=0), axis=0)
    lb = (lb_cum[1] - lb_cum[0]).reshape(1, c_f)
    gw = g_norm_w.reshape(1, -1)
    o_p, state_prompt = _hgrn_prompt(z, lb, gw, n_p, c_heads)
    o_s, state_sample = _hgrn_sample(z, state_hgrn, lb, gw, n_p)
    o_c = jnp.concatenate([o_p, o_s.astype(BF16)], axis=0)
    (d3,) = _mm(o_c, w_out_c, _ep_id, [(_tile(d, 1024, LANES), F32)], name="out_c")

    h3, hn_f32, comb, sel = _router(h2, d3, norm_ffn1, w_router)
    tm_e = _tile(t, 1024, 16)
    tok_of_slot, gate_of_slot, e_of_blk, valid, tok_slots = _route_plan(sel, comb, tm_e)
    rows_of = lambda a, idx: a.at[idx].get(mode="promise_in_bounds")
    y_sorted = _ffn(_cast_rows(rows_of(hn_f32, tok_of_slot), BF16), w_exp_gate, w_exp_up, w_exp_down,
                    gate_of_slot[:, None], e_of_blk, valid, tm=tm_e)
    parts = [h3] + [rows_of(y_sorted, tok_slots[:, r]) for r in range(TOP_K)]
    y = _add_norm(parts, norm_final, want_sum=False, out_dtype=F32)[0]

    y_prompt = y[:n_p].reshape(bp, lp, d)
    y_sample = y[n_p:].reshape(nb, lq, d)
    return (y_prompt, y_sample,
            ckv[:n_p].reshape(bp, lp, kv_lora), kpe[:n_p].reshape(bp, lp, rope),
            ckv[n_p:].reshape(nb, lq, kv_lora), kpe[n_p:].reshape(nb, lq, rope),
            v[n_p:].reshape(nb, lq, a_w),
            state_prompt.reshape(bp, c_heads, c_k, -1), state_sample)
```

```python
import functools
import math

import jax
import jax.numpy as jnp
from jax import lax
from jax.experimental import pallas as pl
from jax.experimental.pallas import tpu as pltpu

F32 = jnp.float32
BF16 = jnp.bfloat16
EPS = 1e-6
ROPE_THETA = 10000.0
TOP_K = 2
VMEM_LIMIT_BYTES = 56 * 1024 * 1024
LANES = 128
SUBLANES = 8
HGRN_CHUNK = 64
HGRN_HEADS_PER_STEP = 4
ATTN_BQ = 256
ATTN_BK = 512
ATTN_ROW_GROUPS = 4
SAMPLE_SEQS_PER_STEP = 2
SAMPLE_PAGE_OPERANDS = 16
DISPATCH_ROW_PARTS = 2
SOFTMAX_ROWS = 32
SOFTMAX_UNROLL = 8
NEG_INF = float("-inf")


def _params(*sem):
    return pltpu.CompilerParams(dimension_semantics=sem, vmem_limit_bytes=VMEM_LIMIT_BYTES)


def _tile(n, pref, mult=SUBLANES):
    if n <= pref:
        return n
    for t in range(pref, 0, -1):
        if n % t == 0 and t % mult == 0:
            return t
    raise ValueError(f"no tile for {n} <= {pref}")


def _dot(a, b):
    return jnp.dot(a, b, preferred_element_type=F32)


def _dot_nt(a, b):
    return lax.dot_general(a, b, (((1,), (1,)), ((), ())), preferred_element_type=F32)


def _rms(x, w):
    return x * lax.rsqrt(jnp.mean(x * x, axis=-1, keepdims=True) + EPS) * w


def _rope_rows(x, cos, sin):
    half = x.shape[-1] // 2
    x1, x2 = x[:, :half], x[:, half:]
    return jnp.concatenate([x1 * cos - x2 * sin, x1 * sin + x2 * cos], axis=-1)


def _add_norm_kernel(*refs, n_in, want_sum):
    x = refs[0][...]
    for r in refs[1:n_in]:
        x = x + r[...]
    g_ref = refs[n_in]
    outs = refs[n_in + 1:]
    k = 0
    if want_sum:
        outs[0][...] = x
        k = 1
    outs[k][...] = _rms(x, g_ref[...]).astype(outs[k].dtype)


def _add_norm(parts, gain, *, want_sum, out_dtype):
    t, d = parts[0].shape
    tm = _tile(t, 256)
    row = pl.BlockSpec((tm, d), lambda i: (i, 0))
    out_shape = []
    if want_sum:
        out_shape.append(jax.ShapeDtypeStruct((t, d), F32))
    out_shape.append(jax.ShapeDtypeStruct((t, d), out_dtype))
    return pl.pallas_call(
        functools.partial(_add_norm_kernel, n_in=len(parts), want_sum=want_sum),
        grid=(t // tm,),
        in_specs=[row] * len(parts) + [pl.BlockSpec((1, d), lambda i: (0, 0))],
        out_specs=[row] * len(out_shape),
        out_shape=out_shape,
        compiler_params=_params("parallel"),
        name="add_norm",
    )(*parts, gain.reshape(1, d))


def _mm_kernel(x_ref, w_ref, *rest, n_extra, n_out, epilogue):
    extras = rest[:n_extra]
    outs = rest[n_extra:n_extra + n_out]
    w_bf = rest[n_extra + n_out]

    @pl.when(pl.program_id(1) == 0)
    def _():
        w_bf[...] = w_ref[...].astype(BF16)

    acc = _dot(x_ref[...], w_bf[...])
    res = epilogue(acc, *[e[...] for e in extras])
    for o, r in zip(outs, res):
        o[...] = r.astype(o.dtype)


def _mm(x, w, epilogue, outs, extras=(), *, tm_pref=1024, tn_pref=1024, name="mm"):
    t, k = x.shape
    n = w.shape[1]
    tm = _tile(t, tm_pref, 16)
    tn = _tile(n, tn_pref, LANES)
    nj = n // tn
    in_specs = [pl.BlockSpec((tm, k), lambda j, i: (i, 0)),
                pl.BlockSpec((k, tn), lambda j, i: (0, j))]
    args = [x, w]
    for arr, kind in extras:
        if kind == "row":
            in_specs.append(pl.BlockSpec((tm, arr.shape[1]), lambda j, i: (i, 0)))
        elif kind == "col":
            in_specs.append(pl.BlockSpec((1, tn), lambda j, i: (0, j)))
        elif kind == "tile":
            in_specs.append(pl.BlockSpec((tm, tn), lambda j, i: (i, j)))
        else:
            raise ValueError(kind)
        args.append(arr)
    out_specs = [pl.BlockSpec((tm, wd), lambda j, i: (i, j)) for wd, _ in outs]
    out_shape = [jax.ShapeDtypeStruct((t, nj * wd), dt) for wd, dt in outs]
    res = pl.pallas_call(
        functools.partial(_mm_kernel, n_extra=len(extras), n_out=len(outs), epilogue=epilogue),
        grid=(nj, t // tm),
        in_specs=in_specs,
        out_specs=out_specs,
        out_shape=out_shape,
        scratch_shapes=[pltpu.VMEM((k, tn), BF16)],
        compiler_params=_params("arbitrary", "arbitrary"),
        name=name,
    )(*args)
    return res


def _ep_gelu(acc):
    return (jax.nn.gelu(acc),)


def _ep_gelu_ln(acc, w, b):
    g = jax.nn.gelu(acc)
    mu = jnp.mean(g, axis=-1, keepdims=True)
    c = g - mu
    var = jnp.mean(c * c, axis=-1, keepdims=True)
    return (c * lax.rsqrt(var + EPS) * w + b,)


def _ep_rms(acc, w):
    return (_rms(acc, w),)


def _ep_kv(acc, cos, sin, w, *, kv_lora):
    ckv = _rms(acc[:, :kv_lora], w)
    kpe = _rope_rows(acc[:, kv_lora:], cos, sin)
    return ckv, kpe, ckv, kpe


def _ep_add(acc, resid):
    return (acc + resid,)


def _ep_id(acc):
    return (acc,)


def _q_proj_kernel(cq_ref, wqn_ref, wqr_ref, wabs_ref, cos_ref, sin_ref, ql_ref, qp_ref, *, heads, nope, rope, scale):
    cq = cq_ref[...]
    qn = _dot(cq, wqn_ref[...])
    qr = _dot(cq, wqr_ref[...])
    cos = cos_ref[...]
    sin = sin_ref[...]
    for h in range(heads):
        lat = _dot(qn[:, h * nope:(h + 1) * nope].astype(BF16), wabs_ref[h])
        ql_ref[h] = (lat * scale).astype(ql_ref.dtype)
        qp_ref[h] = (_rope_rows(qr[:, h * rope:(h + 1) * rope], cos, sin) * scale).astype(qp_ref.dtype)


def _q_proj(cq, wqn, wqr, wabs, cos, sin, scale):
    t, ql = cq.shape
    heads, nope, kv = wabs.shape
    rope = wqr.shape[1] // heads
    tm = _tile(t, 512, 16)
    const2 = lambda i: (0, 0)
    return pl.pallas_call(
        functools.partial(_q_proj_kernel, heads=heads, nope=nope, rope=rope, scale=scale),
        grid=(t // tm,),
        in_specs=[pl.BlockSpec((tm, ql), lambda i: (i, 0)),
                  pl.BlockSpec(wqn.shape, const2),
                  pl.BlockSpec(wqr.shape, const2),
                  pl.BlockSpec(wabs.shape, lambda i: (0, 0, 0)),
                  pl.BlockSpec((tm, rope // 2), lambda i: (i, 0)),
                  pl.BlockSpec((tm, rope // 2), lambda i: (i, 0))],
        out_specs=[pl.BlockSpec((heads, tm, kv), lambda i: (0, i, 0)),
                   pl.BlockSpec((heads, tm, rope), lambda i: (0, i, 0))],
        out_shape=[jax.ShapeDtypeStruct((heads, t, kv), BF16),
                   jax.ShapeDtypeStruct((heads, t, rope), BF16)],
        compiler_params=_params("parallel"),
        name="q_proj",
    )(cq, wqn, wqr, wabs, cos, sin)


def _gmlp_kernel(u_ref, v_ref, m_ref, b_ref, a_ref, *, groups, gd):
    for g in range(groups):
        sl = slice(g * gd, (g + 1) * gd)
        mixed = _dot(m_ref[0, g], v_ref[:, sl].astype(BF16)) + b_ref[0, :, sl]
        a_ref[:, sl] = (u_ref[:, sl].astype(F32) * mixed).astype(a_ref.dtype)


def _gmlp(u, v, mats, bias, n_first):
    t, a = u.shape
    _, groups, c, _ = mats.shape
    sel = lambda n: (jnp.where(n < n_first, 0, 1), 0, 0)
    return pl.pallas_call(
        functools.partial(_gmlp_kernel, groups=groups, gd=a // groups),
        grid=(t // c,),
        in_specs=[pl.BlockSpec((c, a), lambda n: (n, 0)),
                  pl.BlockSpec((c, a), lambda n: (n, 0)),
                  pl.BlockSpec((1, groups, c, c), lambda n: (jnp.where(n < n_first, 0, 1), 0, 0, 0)),
                  pl.BlockSpec((1, c, a), sel)],
        out_specs=pl.BlockSpec((c, a), lambda n: (n, 0)),
        out_shape=jax.ShapeDtypeStruct((t, a), BF16),
        compiler_params=_params("parallel"),
        name="gmlp",
    )(u, v, mats, bias)


def _softmax_rows(s_ref, p_ref, a_ref, m_ref, l_ref, rc, mask_fn=None, static_rows=None):
    width = s_ref.shape[1]

    def chunk(rs, row0):
        s = s_ref[rs, :]
        if mask_fn is not None:
            s = mask_fn(s, row0)
        m_prev = m_ref[rs, :]
        m_new = jnp.maximum(m_prev, jnp.max(s, axis=-1, keepdims=True))
        alpha = jnp.exp(m_prev - m_new)
        p = jnp.exp(s - jnp.tile(m_new, (1, width // LANES)))
        l_ref[rs, :] = alpha * l_ref[rs, :] + jnp.sum(p, axis=-1, keepdims=True)
        m_ref[rs, :] = m_new
        a_ref[rs, :] = alpha
        p_ref[rs, :] = p.astype(p_ref.dtype)

    if static_rows is not None:
        for row0 in range(static_rows[0], static_rows[1], rc):
            chunk(slice(row0, row0 + rc), row0)
        return

    def body(r, carry):
        chunk(pl.ds(pl.multiple_of(r * rc, rc), rc), r * rc)
        return carry

    n = s_ref.shape[0] // rc
    lax.fori_loop(0, n, body, 0, unroll=math.gcd(n, SOFTMAX_UNROLL))


def _rescale_add(acc_ref, a_ref, pv):
    acc_ref[...] = acc_ref[...] * jnp.tile(a_ref[...], (1, acc_ref.shape[1] // LANES)) + pv


def _attn_prompt_kernel(ql_ref, qp_ref, kc_ref, kp_ref, wv_ref, o_ref, m_ref, l_ref, acc_ref, s_ref, p_ref, a_ref,
                        *, heads, bq, bk, vd):
    i = pl.program_id(0)
    j = pl.program_id(1)
    rows = heads * bq

    @pl.when(j == 0)
    def _():
        m_ref[...] = jnp.full(m_ref.shape, NEG_INF, F32)
        l_ref[...] = jnp.zeros(l_ref.shape, F32)
        acc_ref[...] = jnp.zeros(acc_ref.shape, F32)

    def causal(s, row0):
        q_pos = i * bq + (row0 + lax.broadcasted_iota(jnp.int32, s.shape, 0)) % bq
        k_pos = j * bk + lax.broadcasted_iota(jnp.int32, s.shape, 1)
        return jnp.where(k_pos <= q_pos, s, NEG_INF)

    def step(mask_fn):
        hg = heads // math.gcd(heads, ATTN_ROW_GROUPS)
        gr = hg * bq
        kc = kc_ref[...]
        for g in range(heads // hg):
            q = ql_ref[g * hg:(g + 1) * hg].reshape(gr, ql_ref.shape[-1])
            qp = qp_ref[g * hg:(g + 1) * hg].reshape(gr, qp_ref.shape[-1])
            s_ref[g * gr:(g + 1) * gr, :] = _dot_nt(q, kc) + _dot_nt(qp, kp_ref[...])
        for g in range(heads // hg):
            sl = slice(g * gr, (g + 1) * gr)
            _softmax_rows(s_ref, p_ref, a_ref, m_ref, l_ref, SOFTMAX_ROWS, mask_fn, static_rows=(g * gr, (g + 1) * gr))
            acc_ref[sl, :] = (acc_ref[sl, :] * jnp.tile(a_ref[sl, :], (1, acc_ref.shape[1] // LANES))
                              + _dot(p_ref[sl, :], kc))

    needed = j * bk <= i * bq + bq - 1
    crosses_diagonal = j * bk + bk - 1 > i * bq

    @pl.when(needed & crosses_diagonal)
    def _():
        step(causal)

    @pl.when(needed & jnp.logical_not(crosses_diagonal))
    def _():
        step(None)

    @pl.when(j == pl.num_programs(1) - 1)
    def _():
        o = (acc_ref[...] / l_ref[...][:, :1]).astype(BF16)
        for h in range(heads):
            o_ref[:, h * vd:(h + 1) * vd] = _dot(o[h * bq:(h + 1) * bq], wv_ref[h]).astype(o_ref.dtype)


def _attn_prompt(ql, qp, kc, kp, wv, n_prompt):
    heads, _, kv = ql.shape
    rope = qp.shape[-1]
    vd = wv.shape[-1]
    bq = _tile(n_prompt, ATTN_BQ, SOFTMAX_ROWS)
    bk = _tile(n_prompt, ATTN_BK, LANES)
    nq, nk = n_prompt // bq, n_prompt // bk
    rows = heads * bq

    def kmap(i, j):
        return (jnp.minimum(j, (i * bq + bq - 1) // bk), 0)

    return pl.pallas_call(
        functools.partial(_attn_prompt_kernel, heads=heads, bq=bq, bk=bk, vd=vd),
        grid=(nq, nk),
        in_specs=[pl.BlockSpec((heads, bq, kv), lambda i, j: (0, i, 0)),
                  pl.BlockSpec((heads, bq, rope), lambda i, j: (0, i, 0)),
                  pl.BlockSpec((bk, kv), kmap),
                  pl.BlockSpec((bk, rope), kmap),
                  pl.BlockSpec(wv.shape, lambda i, j: (0, 0, 0))],
        out_specs=pl.BlockSpec((bq, heads * vd), lambda i, j: (i, 0)),
        out_shape=jax.ShapeDtypeStruct((n_prompt, heads * vd), BF16),
        scratch_shapes=[pltpu.VMEM((rows, LANES), F32), pltpu.VMEM((rows, LANES), F32),
                        pltpu.VMEM((rows, kv), F32), pltpu.VMEM((rows, bk), F32),
                        pltpu.VMEM((rows, bk), BF16), pltpu.VMEM((rows, LANES), F32)],
        compiler_params=_params("parallel", "arbitrary"),
        name="attn_prompt",
    )(ql, qp, kc, kp, wv)


def _attn_sample_kernel(pt_ref, ql_ref, qp_ref, kcn_ref, kpn_ref, wv_ref, ckv_hbm, kpe_hbm, o_ref,
                        ckv_buf, kpe_buf, sem, kc_s, kp_s, m_ref, l_ref, acc_ref, s_ref, p_ref, a_ref,
                        *, heads, lq, seqs, pages, page, vd):
    b = pl.program_id(0)
    j = pl.program_id(1)
    steps = pl.num_programs(1)
    g = b * steps + j
    slot = g % 2

    def page_copies(bb, jj, to_slot):
        copies = []
        for e in range(seqs):
            for r in range(pages):
                pg = pt_ref[bb * seqs + e, jj * pages + r]
                k = e * pages + r
                copies.append(pltpu.make_async_copy(ckv_hbm.at[pg], ckv_buf.at[to_slot, k], sem.at[to_slot]))
                copies.append(pltpu.make_async_copy(kpe_hbm.at[pg], kpe_buf.at[to_slot, k], sem.at[to_slot]))
        return copies

    @pl.when(g == 0)
    def _():
        for cp in page_copies(0, 0, 0):
            cp.start()

    last_j = j + 1 == steps

    @pl.when(g + 1 < pl.num_programs(0) * steps)
    def _():
        for cp in page_copies(jnp.where(last_j, b + 1, b), jnp.where(last_j, 0, j + 1), 1 - slot):
            cp.start()

    for cp in page_copies(b, j, slot):
        cp.wait()
    ckv_pages = [ckv_buf.at[slot, k] for k in range(seqs * pages)]
    kpe_pages = [kpe_buf.at[slot, k] for k in range(seqs * pages)]

    @pl.when(j == 0)
    def _():
        for e in range(seqs):
            kcn = kcn_ref[e]
            s = _dot_nt(ql_ref[e].astype(F32), kcn) + _dot_nt(qp_ref[e].astype(F32), kpn_ref[e])
            t_q = lax.broadcasted_iota(jnp.int32, s.shape, 0) % lq
            t_k = lax.broadcasted_iota(jnp.int32, s.shape, 1)
            s = jnp.where(t_k <= t_q, s, NEG_INF)
            m0 = jnp.max(s, axis=-1, keepdims=True)
            p = jnp.exp(s - m0)
            m_ref[e] = jnp.broadcast_to(m0, m_ref.shape[1:])
            l_ref[e] = jnp.broadcast_to(jnp.sum(p, axis=-1, keepdims=True), l_ref.shape[1:])
            acc_ref[e] = _dot(p, kcn)

    for e in range(seqs):
        for r in range(pages):
            kc_s[e, r * page:(r + 1) * page, :] = ckv_pages[e * pages + r][...].astype(BF16)
            kp_s[e, :, r * page:(r + 1) * page] = kpe_pages[e * pages + r][...].astype(BF16)
        s_ref[e] = _dot_nt(ql_ref[e], kc_s[e]) + _dot(qp_ref[e], kp_s[e])
    for e in range(seqs):
        _softmax_rows(s_ref.at[e], p_ref.at[e], a_ref.at[e], m_ref.at[e], l_ref.at[e], 2 * SUBLANES,
                      static_rows=(0, s_ref.shape[1]))
        _rescale_add(acc_ref.at[e], a_ref.at[e], _dot(p_ref[e], kc_s[e]))

    @pl.when(j == pl.num_programs(1) - 1)
    def _():
        for e in range(seqs):
            o = (acc_ref[e] / l_ref[e][:, :1]).astype(BF16)
            for h in range(heads):
                full = _dot(o, wv_ref[h])
                o_ref[e, :, h * vd:(h + 1) * vd] = full[h * lq:(h + 1) * lq].astype(o_ref.dtype)


def _attn_sample(page_table, ql, qp, kc_new, kp_new, wv, cache_ckv, cache_kpe_t):
    nb, rows, kv = ql.shape
    rope = qp.shape[-1]
    heads, _, vd = wv.shape
    lq = rows // heads
    n_pages = page_table.shape[1]
    page = cache_ckv.shape[1]
    seqs = math.gcd(nb, SAMPLE_SEQS_PER_STEP)
    pages = _tile(n_pages, SAMPLE_PAGE_OPERANDS // seqs, 1)
    steps = n_pages // pages

    per_seq = lambda *shape: pl.BlockSpec((seqs,) + shape, lambda b, j, pt: (b, 0, 0))
    in_hbm = pl.BlockSpec(memory_space=pl.ANY)
    keys = pages * page
    n_ops = seqs * pages
    grid_spec = pltpu.PrefetchScalarGridSpec(
        num_scalar_prefetch=1,
        grid=(nb // seqs, steps),
        in_specs=[per_seq(rows, kv), per_seq(rows, rope), per_seq(lq, kv), per_seq(lq, rope),
                  pl.BlockSpec(wv.shape, lambda b, j, pt: (0, 0, 0)), in_hbm, in_hbm],
        out_specs=per_seq(lq, heads * vd),
        scratch_shapes=[pltpu.VMEM((2, n_ops, page, kv), cache_ckv.dtype),
                        pltpu.VMEM((2, n_ops, rope, page), cache_kpe_t.dtype),
                        pltpu.SemaphoreType.DMA((2,)),
                        pltpu.VMEM((seqs, keys, kv), BF16), pltpu.VMEM((seqs, rope, keys), BF16),
                        pltpu.VMEM((seqs, rows, LANES), F32), pltpu.VMEM((seqs, rows, LANES), F32),
                        pltpu.VMEM((seqs, rows, kv), F32), pltpu.VMEM((seqs, rows, keys), F32),
                        pltpu.VMEM((seqs, rows, keys), BF16), pltpu.VMEM((seqs, rows, LANES), F32)],
    )
    return pl.pallas_call(
        functools.partial(_attn_sample_kernel, heads=heads, lq=lq, seqs=seqs, pages=pages, page=page, vd=vd),
        grid_spec=grid_spec,
        out_shape=jax.ShapeDtypeStruct((nb, lq, heads * vd), BF16),
        compiler_params=_params("arbitrary", "arbitrary"),
        name="attn_sample",
    )(page_table, ql, qp, kc_new, kp_new, wv, cache_ckv, cache_kpe_t)


def _cast_kernel(*refs):
    o_ref = refs[-1]
    col = 0
    for x_ref in refs[:-1]:
        o_ref[:, col:col + x_ref.shape[1]] = x_ref[...].astype(o_ref.dtype)
        col += x_ref.shape[1]


def _cast_rows(parts, dtype):
    s = parts[0].shape[0]
    d = sum(p.shape[1] for p in parts)
    tm = _tile(s, 512, 16)
    return pl.pallas_call(
        _cast_kernel, grid=(s // tm,),
        in_specs=[pl.BlockSpec((tm, p.shape[1]), lambda i: (i, 0)) for p in parts],
        out_specs=pl.BlockSpec((tm, d), lambda i: (i, 0)),
        out_shape=jax.ShapeDtypeStruct((s, d), dtype), compiler_params=_params("parallel"), name="cast_rows",
    )(*parts)


def _ffn_kernel(be_ref, bv_ref, x_ref, wg_ref, wu_ref, wd_ref, gate_ref, o_ref, wg_s, wu_s, wd_s, *, tm, ts):
    i = pl.program_id(0)
    j = pl.program_id(1)
    valid = bv_ref[i]

    @pl.when(j == 0)
    def _():
        o_ref[...] = jnp.zeros(o_ref.shape, F32)

    @pl.when(valid > 0)
    def _():
        wg_s[...] = wg_ref[0].astype(BF16)
        wu_s[...] = wu_ref[0].astype(BF16)
        wd_s[...] = wd_ref[0].astype(BF16)

    def swiglu(rows):
        x = x_ref[rows, :]
        h = (jax.nn.silu(_dot(x, wg_s[...])) * _dot(x, wu_s[...])).astype(BF16)
        o_ref[rows, :] += _dot(h, wd_s[...])

    @pl.when(valid == tm)
    def _():
        swiglu(slice(None))

    for sub in range(tm // ts):
        @pl.when((valid > sub * ts) & (valid < tm))
        def _():
            swiglu(slice(sub * ts, (sub + 1) * ts))

    @pl.when(j == pl.num_programs(1) - 1)
    def _():
        o_ref[...] = o_ref[...] * gate_ref[...]


def _ffn(x, wg, wu, wd, gate, blk_expert, blk_valid, *, tm, tf_pref=512, ts_pref=256):
    s, d = x.shape
    f = wg.shape[2]
    tf = _tile(f, tf_pref, LANES)
    ts = _tile(tm, ts_pref, 16)
    nj = f // tf

    def jj(i, j, bv):
        return jnp.where(bv[i] > 0, j, nj - 1)

    once = pl.Buffered(1)
    scratch = [pltpu.VMEM((d, tf), BF16), pltpu.VMEM((d, tf), BF16), pltpu.VMEM((tf, d), BF16)]
    grid_spec = pltpu.PrefetchScalarGridSpec(
        num_scalar_prefetch=2,
        grid=(s // tm, nj),
        in_specs=[pl.BlockSpec((tm, d), lambda i, j, be, bv: (i, 0), pipeline_mode=once),
                  pl.BlockSpec((1, d, tf), lambda i, j, be, bv: (be[i], 0, jj(i, j, bv))),
                  pl.BlockSpec((1, d, tf), lambda i, j, be, bv: (be[i], 0, jj(i, j, bv))),
                  pl.BlockSpec((1, tf, d), lambda i, j, be, bv: (be[i], jj(i, j, bv), 0)),
                  pl.BlockSpec((tm, 1), lambda i, j, be, bv: (i, 0), pipeline_mode=once)],
        out_specs=pl.BlockSpec((tm, d), lambda i, j, be, bv: (i, 0), pipeline_mode=once),
        scratch_shapes=scratch,
    )
    return pl.pallas_call(
        functools.partial(_ffn_kernel, tm=tm, ts=ts),
        grid_spec=grid_spec,
        out_shape=jax.ShapeDtypeStruct((s, d), F32),
        compiler_params=_params("arbitrary", "arbitrary"),
        name="ffn",
    )(blk_expert, blk_valid, x, wg, wu, wd, gate)


def _split3(a):
    a1 = a.astype(BF16)
    r1 = a - a1.astype(F32)
    a2 = r1.astype(BF16)
    a3 = (r1 - a2.astype(F32)).astype(BF16)
    return a1, a2, a3


def _router_kernel(h_ref, d_ref, g_ref, wr_ref, hs_ref, comb_ref, sel_ref, *hn_refs):
    x = h_ref[...] + d_ref[...]
    hs_ref[...] = x
    y = _rms(x, g_ref[...])
    width = y.shape[1] // len(hn_refs)
    for n, hn_ref in enumerate(hn_refs):
        hn_ref[...] = y[:, n * width:(n + 1) * width]
    ys = _split3(y)
    ws = _split3(wr_ref[...])
    logits = jnp.zeros((x.shape[0], wr_ref.shape[1]), F32)
    for a in range(3):
        for b in range(3 - a):
            logits = logits + _dot(ys[a], ws[b])
    n_e = logits.shape[1]
    lane = lax.broadcasted_iota(jnp.int32, logits.shape, 1).astype(F32)
    m1 = jnp.max(logits, axis=-1, keepdims=True)
    i1 = jnp.min(jnp.where(logits == m1, lane, n_e), axis=-1, keepdims=True)
    first = lane == i1
    rest = jnp.where(first, NEG_INF, logits)
    m2 = jnp.max(rest, axis=-1, keepdims=True)
    i2 = jnp.min(jnp.where(rest == m2, lane, n_e), axis=-1, keepdims=True)
    second = lane == i2
    e = jnp.exp(m2 - m1)
    den = 1.0 + e
    comb_ref[...] = jnp.where(first, 1.0 / den, 0.0) + jnp.where(second, e / den, 0.0)
    sel_ref[...] = jnp.where(first | second, 1.0, 0.0)


def _router(h, delta, gain, w_router, n_parts):
    t, d = h.shape
    n_e = w_router.shape[1]
    tm = _tile(t, 256)
    row = pl.BlockSpec((tm, d), lambda i: (i, 0))
    small = pl.BlockSpec((tm, n_e), lambda i: (i, 0))
    part = pl.BlockSpec((tm, d // n_parts), lambda i: (i, 0))
    h_sum, comb, sel, *hn_parts = pl.pallas_call(
        _router_kernel,
        grid=(t // tm,),
        in_specs=[row, row, pl.BlockSpec((1, d), lambda i: (0, 0)), pl.BlockSpec((d, n_e), lambda i: (0, 0))],
        out_specs=[row, small, small] + [part] * n_parts,
        out_shape=[jax.ShapeDtypeStruct((t, d), F32),
                   jax.ShapeDtypeStruct((t, n_e), F32), jax.ShapeDtypeStruct((t, n_e), F32)]
                  + [jax.ShapeDtypeStruct((t, d // n_parts), F32)] * n_parts,
        compiler_params=_params("parallel"),
        name="router",
    )(h, delta, gain.reshape(1, d), w_router)
    return h_sum, hn_parts, comb, sel


def _roll_rows(x, shift):
    n = x.shape[0]
    shift = shift % n
    return x if shift == 0 else pltpu.roll(x, shift, 0)


def _hgrn_inputs(zq, zf, lb):
    q = jax.nn.silu(zq)
    lf = jnp.log(lb + (1.0 - lb) * jax.nn.sigmoid(zf))
    k = (1.0 - lb) * jax.nn.sigmoid(-zf)
    return q, k, lf


def _prefix8(lf, r8):
    p = lf
    for sh in (1, 2, 4):
        p = p + jnp.where(r8 >= sh, _roll_rows(p, sh), 0.0)
    return p


def _block8_intra(q, k, v, p8, r8):
    o = jnp.sum(q * k, axis=-1, keepdims=True) * v
    for d in range(1, SUBLANES):
        e = jnp.exp(jnp.where(r8 >= d, p8 - _roll_rows(p8, d), NEG_INF))
        w = jnp.sum(q * _roll_rows(k, d) * e, axis=-1, keepdims=True)
        o = o + w * _roll_rows(v, d)
    return o


def _hgrn_finish(o, zg, gw):
    return _rms(o, gw) * jax.nn.silu(zg)


def _hgrn_prompt_kernel(zq_ref, zf_ref, zi_ref, zg_ref, lb_ref, gw_ref, o_ref, st_ref, st_t, *, c, n_chunks, hp, kd):
    @pl.when(pl.program_id(1) == 0)
    def _():
        st_t[...] = jnp.zeros(st_t.shape, F32)

    gw = gw_ref[...]
    row = lax.broadcasted_iota(jnp.int32, (c, kd), 0)
    r8 = row % SUBLANES
    ri = lax.broadcasted_iota(jnp.int32, (c, c), 0)
    ci = lax.broadcasted_iota(jnp.int32, (c, c), 1)

    def one_head(rows, hh):
        cols = slice(hh * kd, (hh + 1) * kd)
        q, k, lf = _hgrn_inputs(zq_ref[rows, cols], zf_ref[rows, cols], lb_ref[:, cols])
        v = zi_ref[rows, cols]
        p = _prefix8(lf, r8)
        o = _block8_intra(q, k, v, p, r8)
        tot = jnp.where(r8 == SUBLANES - 1, p, 0.0)
        for sh in (1, 2, 4):
            tot = tot + _roll_rows(tot, -sh)
        a = jnp.zeros((c, c), F32)
        s = SUBLANES
        while s < c:
            second = row % (2 * s) >= s
            qs = (q * jnp.exp(jnp.where(second, p, NEG_INF))).astype(BF16)
            ks = (k * jnp.exp(jnp.where(second, NEG_INF, tot - p))).astype(BF16)
            a_s = _dot_nt(qs, ks)
            a = a + (a_s if 2 * s == c else jnp.where(ri // (2 * s) == ci // (2 * s), a_s, 0.0))
            prev = _roll_rows(tot, s)
            p = p + jnp.where(second, prev, 0.0)
            tot = tot + jnp.where(second, prev, _roll_rows(tot, -s))
            s *= 2
        st = st_t[hh]
        o = o + _dot(a.astype(BF16), v.astype(BF16)) + _dot_nt((q * jnp.exp(p)).astype(BF16), st.astype(BF16))
        kb = (k * jnp.exp(tot - p)).astype(BF16)
        st_t[hh] = st * jnp.exp(tot[0:1, :]) + _dot(v.T.astype(BF16), kb)
        o_ref[rows, cols] = _hgrn_finish(o, zg_ref[rows, cols], gw).astype(o_ref.dtype)

    def chunk(n, carry):
        rows = pl.ds(pl.multiple_of(n * c, c), c)
        for hh in range(hp):
            one_head(rows, hh)
        return carry

    lax.fori_loop(0, n_chunks, chunk, 0)

    @pl.when(pl.program_id(1) == pl.num_programs(1) - 1)
    def _():
        for hh in range(hp):
            st_ref[hh] = st_t[hh].T


def _hgrn_prompt(z, lb, gw, n_prompt, heads):
    kd = lb.shape[1] // heads
    c = HGRN_CHUNK
    hp = math.gcd(heads, HGRN_HEADS_PER_STEP)
    lbk = _tile(n_prompt, 1024, c)
    groups = heads // hp

    def zspec(seg):
        return pl.BlockSpec((lbk, hp * kd), lambda h, l: (l, seg * groups + h))

    return pl.pallas_call(
        functools.partial(_hgrn_prompt_kernel, c=c, n_chunks=lbk // c, hp=hp, kd=kd),
        grid=(groups, n_prompt // lbk),
        in_specs=[zspec(0), zspec(1), zspec(2), zspec(3),
                  pl.BlockSpec((1, hp * kd), lambda h, l: (0, h)),
                  pl.BlockSpec((1, kd), lambda h, l: (0, 0))],
        out_specs=[pl.BlockSpec((lbk, hp * kd), lambda h, l: (l, h)),
                   pl.BlockSpec((hp, kd, kd), lambda h, l: (h, 0, 0))],
        out_shape=[jax.ShapeDtypeStruct((n_prompt, heads * kd), BF16),
                   jax.ShapeDtypeStruct((heads, kd, kd), F32)],
        scratch_shapes=[pltpu.VMEM((hp, kd, kd), F32)],
        compiler_params=_params("parallel", "arbitrary"),
        name="hgrn_prompt",
    )(z, z, z, z, lb, gw)


def _hgrn_sample_kernel(z_ref, s0_ref, lb_ref, gw_ref, o_ref, s1_ref, *, heads, kd):
    lq = z_ref.shape[0]
    gw = gw_ref[...]
    r8 = lax.broadcasted_iota(jnp.int32, (lq, kd), 0)
    for h in range(heads):
        col = lambda seg: slice((seg * heads + h) * kd, (seg * heads + h + 1) * kd)
        q, k, lf = _hgrn_inputs(z_ref[:, col(0)], z_ref[:, col(1)], lb_ref[:, h * kd:(h + 1) * kd])
        v = z_ref[:, col(2)]
        p = _prefix8(lf, r8)
        last = p[lq - 1:lq, :]
        st = s0_ref[0, h]
        o = _block8_intra(q, k, v, p, r8) + _dot((q * jnp.exp(p)).astype(BF16), st.astype(BF16))
        cols = jnp.concatenate([k * jnp.exp(last - p), p], axis=0).T
        v_pad = jnp.concatenate([v, jnp.zeros_like(v)], axis=0)
        s1_ref[0, h] = st * jnp.exp(cols[:, 2 * lq - 1:2 * lq]) + _dot(cols.astype(BF16), v_pad.astype(BF16))
        o_ref[:, h * kd:(h + 1) * kd] = _hgrn_finish(o, z_ref[:, col(3)], gw).astype(o_ref.dtype)


def _hgrn_sample(z, state, lb, gw, row0):
    nb, heads, kd, _ = state.shape
    lq = SUBLANES
    blk0 = row0 // lq
    return pl.pallas_call(
        functools.partial(_hgrn_sample_kernel, heads=heads, kd=kd),
        grid=(nb,),
        in_specs=[pl.BlockSpec((lq, z.shape[1]), lambda b: (blk0 + b, 0)),
                  pl.BlockSpec((1, heads, kd, kd), lambda b: (b, 0, 0, 0)),
                  pl.BlockSpec((1, heads * kd), lambda b: (0, 0)),
                  pl.BlockSpec((1, kd), lambda b: (0, 0))],
        out_specs=[pl.BlockSpec((lq, heads * kd), lambda b: (b, 0)),
                   pl.BlockSpec((1, heads, kd, kd), lambda b: (b, 0, 0, 0))],
        out_shape=[jax.ShapeDtypeStruct((nb * lq, heads * kd), F32),
                   jax.ShapeDtypeStruct(state.shape, F32)],
        compiler_params=_params("parallel"),
        name="hgrn_sample",
    )(z, state, lb, gw)


def _route_plan(sel, comb, tm):
    t, n_e = sel.shape
    n_blk = -(-(TOP_K * t) // tm) + n_e
    n_slot = n_blk * tm
    picked = sel > 0
    seli = picked.astype(jnp.int32)
    incl = jnp.cumsum(seli, axis=0)
    counts = incl[-1]
    blocks_per = (counts + tm - 1) // tm
    blk_end = jnp.cumsum(blocks_per)
    start = (blk_end - blocks_per) * tm
    slot = start[None, :] + incl - seli
    slot_a = jnp.min(jnp.where(picked, slot, n_slot), axis=1)
    slot_b = jnp.max(jnp.where(picked, slot, -1), axis=1)
    tok_slots = jnp.stack([slot_a, slot_b], axis=1)
    gates = jnp.stack([jnp.sum(jnp.where(picked & (slot == s[:, None]), comb, 0.0), axis=1)
                       for s in (slot_a, slot_b)], axis=1).reshape(-1)
    pair = jnp.arange(TOP_K * t, dtype=jnp.int32)
    pair_of_slot = jnp.full((n_slot,), -1, jnp.int32).at[tok_slots.reshape(-1)].set(pair, unique_indices=True)
    filled = pair_of_slot >= 0
    safe = jnp.maximum(pair_of_slot, 0)
    tok_of_slot = safe // TOP_K
    gate_of_slot = jnp.where(filled, gates.at[safe].get(mode="promise_in_bounds"), 0.0)
    blk = jnp.arange(n_blk, dtype=jnp.int32)
    used = blk_end[-1]
    e_of_blk = jnp.sum((jnp.minimum(blk, used - 1)[:, None] >= blk_end[None, :]).astype(jnp.int32), axis=1)
    e_of_blk = jnp.minimum(e_of_blk, n_e - 1)
    valid = jnp.clip(counts[e_of_blk] - (blk * tm - start[e_of_blk]), 0, tm)
    valid = jnp.where(blk < used, valid, 0).astype(jnp.int32)
    return tok_of_slot, gate_of_slot, e_of_blk, valid, tok_slots


def kernel(x_prompt, x_sample, cache_ckv, cache_kpe, state_hgrn, page_table, norm_mix0, w_in_ab, ln_v_w, ln_v_b, w_s, b_s, q_norm_w, w_q_b, kv_norm_w, w_kv_b, w_out_ab, norm_ffn0, w_ffn_gate, w_ffn_up, w_ffn_down, norm_mix1, w_in_c, lb_logits, g_norm_w, w_out_c, norm_ffn1, w_router, w_exp_gate, w_exp_up, w_exp_down, norm_final):
    bp, lp, d = x_prompt.shape
    nb, lq, _ = x_sample.shape
    assert bp == 1, "the prompt group is one sequence"
    n_p, n_s = bp * lp, nb * lq
    t = n_p + n_s
    groups, chunk, _ = w_s.shape
    a_w = ln_v_w.shape[0]
    q_lora, heads, qk = w_q_b.shape
    kv_lora = kv_norm_w.shape[0]
    rope = cache_kpe.shape[-1]
    nope = qk - rope
    vd = w_kv_b.shape[-1] - nope
    page = cache_ckv.shape[1]
    past = page_table.shape[1] * page
    c_heads, c_k = state_hgrn.shape[1], state_hgrn.shape[2]
    c_f = c_heads * c_k
    assert lq == SUBLANES and chunk % lq == 0 and n_p % chunk == 0 and n_s % chunk == 0
    off_q, off_kv = 2 * a_w, 2 * a_w + q_lora
    scale = float(qk) ** -0.5

    x = jnp.concatenate([x_prompt.reshape(n_p, d), x_sample.reshape(n_s, d)], axis=0)

    pos = jnp.concatenate([jnp.arange(lp, dtype=F32), jnp.tile(past + jnp.arange(lq, dtype=F32), nb)])
    inv_freq = ROPE_THETA ** (-jnp.arange(rope // 2, dtype=F32) / (rope // 2))
    ang = pos[:, None] * inv_freq[None, :]
    cos, sin = jnp.cos(ang), jnp.sin(ang)

    hn = _add_norm([x], norm_mix0, want_sum=False, out_dtype=BF16)[0]
    (u,) = _mm(hn, w_in_ab[:, :a_w], _ep_gelu, [(a_w, BF16)], name="in_u")
    (v,) = _mm(hn, w_in_ab[:, a_w:off_q], _ep_gelu_ln, [(a_w, F32)],
               [(ln_v_w.reshape(1, a_w), "col"), (ln_v_b.reshape(1, a_w), "col")], name="in_v")
    (cq,) = _mm(hn, w_in_ab[:, off_q:off_kv], _ep_rms, [(q_lora, BF16)],
                [(q_norm_w.reshape(1, q_lora), "col")], name="in_cq")
    kvw = kv_lora + rope
    w_kvpe = w_in_ab[:, off_kv:]
    t_kv = _tile(t, 512, 16)
    ckv, kpe, ckv_b, kpe_b = pl.pallas_call(
        functools.partial(_mm_kernel, n_extra=3, n_out=4, epilogue=functools.partial(_ep_kv, kv_lora=kv_lora)),
        grid=(1, t // t_kv),
        in_specs=[pl.BlockSpec((t_kv, d), lambda j, i: (i, 0)),
                  pl.BlockSpec((d, kvw), lambda j, i: (0, 0)),
                  pl.BlockSpec((t_kv, rope // 2), lambda j, i: (i, 0)),
                  pl.BlockSpec((t_kv, rope // 2), lambda j, i: (i, 0)),
                  pl.BlockSpec((1, kv_lora), lambda j, i: (0, 0))],
        out_specs=[pl.BlockSpec((t_kv, kv_lora), lambda j, i: (i, 0)),
                   pl.BlockSpec((t_kv, rope), lambda j, i: (i, 0)),
                   pl.BlockSpec((t_kv, kv_lora), lambda j, i: (i, 0)),
                   pl.BlockSpec((t_kv, rope), lambda j, i: (i, 0))],
        out_shape=[jax.ShapeDtypeStruct((t, kv_lora), F32), jax.ShapeDtypeStruct((t, rope), F32),
                   jax.ShapeDtypeStruct((t, kv_lora), BF16), jax.ShapeDtypeStruct((t, rope), BF16)],
        scratch_shapes=[pltpu.VMEM((d, kvw), BF16)],
        compiler_params=_params("arbitrary", "arbitrary"),
        name="in_kv",
    )(hn, w_kvpe, cos, sin, kv_norm_w.reshape(1, kv_lora))

    wqn = w_q_b[:, :, :nope].reshape(q_lora, heads * nope).astype(BF16)
    wqr = w_q_b[:, :, nope:].reshape(q_lora, heads * rope).astype(BF16)
    wabs = jnp.transpose(w_kv_b[:, :, :nope], (1, 2, 0)).astype(BF16)
    wv = jnp.transpose(w_kv_b[:, :, nope:], (1, 0, 2)).astype(BF16)
    ql, qp = _q_proj(cq, wqn, wqr, wabs, cos, sin, scale)

    o_prompt = _attn_prompt(ql, qp, ckv_b, kpe_b, wv, n_p)
    to_rows = lambda a: jnp.transpose(a[:, n_p:].reshape(heads, nb, lq, a.shape[-1]), (1, 0, 2, 3)).reshape(
        nb, heads * lq, a.shape[-1])
    o_sample = _attn_sample(page_table, to_rows(ql), to_rows(qp),
                            ckv[n_p:].reshape(nb, lq, kv_lora), kpe[n_p:].reshape(nb, lq, rope),
                            wv, cache_ckv, jnp.swapaxes(cache_kpe, 1, 2))

    tril = jnp.tril(jnp.ones((chunk, chunk), dtype=bool))
    m_prompt = jnp.where(tril[None], w_s, 0.0)
    eye = jnp.eye(chunk // lq, dtype=F32)
    m_sample = jnp.einsum("ab,gts->gatbs", eye, m_prompt[:, :lq, :lq]).reshape(groups, chunk, chunk)
    mats = jnp.stack([m_prompt, m_sample]).astype(BF16)
    bias_p = jnp.repeat(b_s.T, a_w // groups, axis=1)
    bias = jnp.stack([bias_p, jnp.tile(bias_p[:lq], (chunk // lq, 1))])
    a_mix = _gmlp(u, v, mats, bias, n_p // chunk)

    mix = jnp.concatenate([a_mix, jnp.concatenate([o_prompt, o_sample.reshape(n_s, heads * vd)], axis=0)], axis=1)
    (h1,) = _mm(mix, w_out_ab, _ep_add, [(_tile(d, 1024, LANES), F32)], [(x, "tile")], name="out_ab")

    hn = _add_norm([h1], norm_ffn0, want_sum=False, out_dtype=BF16)[0]
    tm_f = _tile(t, 1024, 16)
    nblk = t // tm_f
    y_ffn = _ffn(hn, w_ffn_gate[None], w_ffn_up[None], w_ffn_down[None], jnp.ones((t, 1), F32),
                 jnp.zeros((nblk,), jnp.int32), jnp.full((nblk,), tm_f, jnp.int32), tm=tm_f)

    h2, hn = _add_norm([h1, y_ffn], norm_mix1, want_sum=True, out_dtype=BF16)
    (z,) = _mm(hn, w_in_c, _ep_id, [(_tile(w_in_c.shape[1], 1024, LANES), F32)], name="in_c")
    lb_cum = jnp.cumsum(jax.nn.softmax(lb_logits.astype(F32), axis=0), axis=0)
    lb = (lb_cum[1] - lb_cum[0]).reshape(1, c_f)
    gw = g_norm_w.reshape(1, -1)
    o_p, state_prompt = _hgrn_prompt(z, lb, gw, n_p, c_heads)
    o_s, state_sample = _hgrn_sample(z, state_hgrn, lb, gw, n_p)
    o_c = jnp.concatenate([o_p, o_s.astype(BF16)], axis=0)
    (d3,) = _mm(o_c, w_out_c, _ep_id, [(_tile(d, 1024, LANES), F32)], name="out_c")

    h3, hn_parts, comb, sel = _router(h2, d3, norm_ffn1, w_router, DISPATCH_ROW_PARTS)
    tm_e = _tile(t, 1024, 16)
    tok_of_slot, gate_of_slot, e_of_blk, valid, tok_slots = _route_plan(sel, comb, tm_e)
    rows_of = lambda a, idx: a.at[idx].get(mode="promise_in_bounds")
    x_sorted = _cast_rows([rows_of(part, tok_of_slot) for part in hn_parts], BF16)
    y_sorted = _ffn(x_sorted, w_exp_gate, w_exp_up, w_exp_down,
                    gate_of_slot[:, None], e_of_blk, valid, tm=tm_e)
    parts = [h3] + [rows_of(y_sorted, tok_slots[:, r]) for r in range(TOP_K)]
    y = _add_norm(parts, norm_final, want_sum=False, out_dtype=F32)[0]

    y_prompt = y[:n_p].reshape(bp, lp, d)
    y_sample = y[n_p:].reshape(nb, lq, d)
    return (y_prompt, y_sample,
            ckv[:n_p].reshape(bp, lp, kv_lora), kpe[:n_p].reshape(bp, lp, rope),
            ckv[n_p:].reshape(nb, lq, kv_lora), kpe[n_p:].reshape(nb, lq, rope),
            v[n_p:].reshape(nb, lq, a_w),
            state_prompt.reshape(bp, c_heads, c_k, -1), state_sample)
```

```python
import functools
import math

import jax
import jax.numpy as jnp
from jax import lax
from jax.experimental import pallas as pl
from jax.experimental.pallas import tpu as pltpu

F32 = jnp.float32
BF16 = jnp.bfloat16
EPS = 1e-6
ROPE_THETA = 10000.0
TOP_K = 2
VMEM_LIMIT_BYTES = 56 * 1024 * 1024
LANES = 128
SUBLANES = 8
HGRN_CHUNK = 64
HGRN_HEADS_PER_STEP = 4
ATTN_BQ = 256
ATTN_BK = 512
ATTN_ROW_GROUPS = 4
SAMPLE_SEQS_PER_STEP = 2
SAMPLE_PAGE_OPERANDS = 16
SAMPLE_PAGE_SLOTS = 3
DISPATCH_ROW_PARTS = 2
FFN_FULL_BLOCK_CHUNKS = 2
DENSE_BLOCK_ROWS = 1152
MOE_BLOCK_ROWS = 1280
SOFTMAX_ROWS = 32
SOFTMAX_UNROLL = 8
NEG_INF = float("-inf")


def _params(*sem):
    return pltpu.CompilerParams(dimension_semantics=sem, vmem_limit_bytes=VMEM_LIMIT_BYTES)


def _tile(n, pref, mult=SUBLANES):
    if n <= pref:
        return n
    for t in range(pref, 0, -1):
        if n % t == 0 and t % mult == 0:
            return t
    raise ValueError(f"no tile for {n} <= {pref}")


def _dot(a, b):
    return jnp.dot(a, b, preferred_element_type=F32)


def _dot_nt(a, b):
    return lax.dot_general(a, b, (((1,), (1,)), ((), ())), preferred_element_type=F32)


def _rms(x, w):
    return x * lax.rsqrt(jnp.mean(x * x, axis=-1, keepdims=True) + EPS) * w


def _rope_rows(x, cos, sin):
    half = x.shape[-1] // 2
    x1, x2 = x[:, :half], x[:, half:]
    return jnp.concatenate([x1 * cos - x2 * sin, x1 * sin + x2 * cos], axis=-1)


def _add_norm_kernel(*refs, n_in, want_sum):
    x = refs[0][...]
    for r in refs[1:n_in]:
        x = x + r[...]
    g_ref = refs[n_in]
    outs = refs[n_in + 1:]
    k = 0
    if want_sum:
        outs[0][...] = x
        k = 1
    outs[k][...] = _rms(x, g_ref[...]).astype(outs[k].dtype)


def _add_norm(parts, gain, *, want_sum, out_dtype):
    t, d = parts[0].shape
    tm = _tile(t, 256)
    row = pl.BlockSpec((tm, d), lambda i: (i, 0))
    out_shape = []
    if want_sum:
        out_shape.append(jax.ShapeDtypeStruct((t, d), F32))
    out_shape.append(jax.ShapeDtypeStruct((t, d), out_dtype))
    return pl.pallas_call(
        functools.partial(_add_norm_kernel, n_in=len(parts), want_sum=want_sum),
        grid=(t // tm,),
        in_specs=[row] * len(parts) + [pl.BlockSpec((1, d), lambda i: (0, 0))],
        out_specs=[row] * len(out_shape),
        out_shape=out_shape,
        compiler_params=_params("parallel"),
        name="add_norm",
    )(*parts, gain.reshape(1, d))


def _mm_kernel(x_ref, w_ref, *rest, n_extra, n_out, epilogue):
    extras = rest[:n_extra]
    outs = rest[n_extra:n_extra + n_out]
    w_bf = rest[n_extra + n_out]

    @pl.when(pl.program_id(1) == 0)
    def _():
        w_bf[...] = w_ref[...].astype(BF16)

    acc = _dot(x_ref[...], w_bf[...])
    res = epilogue(acc, *[e[...] for e in extras])
    for o, r in zip(outs, res):
        o[...] = r.astype(o.dtype)


def _mm(x, w, epilogue, outs, extras=(), *, tm_pref=1024, tn_pref=1024, name="mm"):
    t, k = x.shape
    n = w.shape[1]
    tm = _tile(t, tm_pref, 16)
    tn = _tile(n, tn_pref, LANES)
    nj = n // tn
    in_specs = [pl.BlockSpec((tm, k), lambda j, i: (i, 0)),
                pl.BlockSpec((k, tn), lambda j, i: (0, j))]
    args = [x, w]
    for arr, kind in extras:
        if kind == "row":
            in_specs.append(pl.BlockSpec((tm, arr.shape[1]), lambda j, i: (i, 0)))
        elif kind == "col":
            in_specs.append(pl.BlockSpec((1, tn), lambda j, i: (0, j)))
        elif kind == "tile":
            in_specs.append(pl.BlockSpec((tm, tn), lambda j, i: (i, j)))
        else:
            raise ValueError(kind)
        args.append(arr)
    out_specs = [pl.BlockSpec((tm, wd), lambda j, i: (i, j)) for wd, _ in outs]
    out_shape = [jax.ShapeDtypeStruct((t, nj * wd), dt) for wd, dt in outs]
    res = pl.pallas_call(
        functools.partial(_mm_kernel, n_extra=len(extras), n_out=len(outs), epilogue=epilogue),
        grid=(nj, t // tm),
        in_specs=in_specs,
        out_specs=out_specs,
        out_shape=out_shape,
        scratch_shapes=[pltpu.VMEM((k, tn), BF16)],
        compiler_params=_params("arbitrary", "arbitrary"),
        name=name,
    )(*args)
    return res


def _ep_gelu(acc):
    return (jax.nn.gelu(acc),)


def _ep_gelu_ln(acc, w, b):
    g = jax.nn.gelu(acc)
    mu = jnp.mean(g, axis=-1, keepdims=True)
    c = g - mu
    var = jnp.mean(c * c, axis=-1, keepdims=True)
    return (c * lax.rsqrt(var + EPS) * w + b,)


def _ep_rms(acc, w):
    return (_rms(acc, w),)


def _ep_kv(acc, cos, sin, w, *, kv_lora):
    ckv = _rms(acc[:, :kv_lora], w)
    kpe = _rope_rows(acc[:, kv_lora:], cos, sin)
    return ckv, kpe, ckv, kpe


def _ep_add(acc, resid):
    return (acc + resid,)


def _ep_id(acc):
    return (acc,)


def _q_proj_kernel(cq_ref, wqn_ref, wqr_ref, wabs_ref, cos_ref, sin_ref, ql_ref, qp_ref, *, heads, nope, rope, scale):
    cq = cq_ref[...]
    qn = _dot(cq, wqn_ref[...])
    qr = _dot(cq, wqr_ref[...])
    cos = cos_ref[...]
    sin = sin_ref[...]
    for h in range(heads):
        lat = _dot(qn[:, h * nope:(h + 1) * nope].astype(BF16), wabs_ref[h])
        ql_ref[h] = (lat * scale).astype(ql_ref.dtype)
        qp_ref[h] = (_rope_rows(qr[:, h * rope:(h + 1) * rope], cos, sin) * scale).astype(qp_ref.dtype)


def _q_proj(cq, wqn, wqr, wabs, cos, sin, scale):
    t, ql = cq.shape
    heads, nope, kv = wabs.shape
    rope = wqr.shape[1] // heads
    tm = _tile(t, 512, 16)
    const2 = lambda i: (0, 0)
    return pl.pallas_call(
        functools.partial(_q_proj_kernel, heads=heads, nope=nope, rope=rope, scale=scale),
        grid=(t // tm,),
        in_specs=[pl.BlockSpec((tm, ql), lambda i: (i, 0)),
                  pl.BlockSpec(wqn.shape, const2),
                  pl.BlockSpec(wqr.shape, const2),
                  pl.BlockSpec(wabs.shape, lambda i: (0, 0, 0)),
                  pl.BlockSpec((tm, rope // 2), lambda i: (i, 0)),
                  pl.BlockSpec((tm, rope // 2), lambda i: (i, 0))],
        out_specs=[pl.BlockSpec((heads, tm, kv), lambda i: (0, i, 0)),
                   pl.BlockSpec((heads, tm, rope), lambda i: (0, i, 0))],
        out_shape=[jax.ShapeDtypeStruct((heads, t, kv), BF16),
                   jax.ShapeDtypeStruct((heads, t, rope), BF16)],
        compiler_params=_params("parallel"),
        name="q_proj",
    )(cq, wqn, wqr, wabs, cos, sin)


def _gmlp_kernel(u_ref, v_ref, m_ref, b_ref, a_ref, *, groups, gd):
    for g in range(groups):
        sl = slice(g * gd, (g + 1) * gd)
        mixed = _dot(m_ref[0, g], v_ref[:, sl].astype(BF16)) + b_ref[0, :, sl]
        a_ref[:, sl] = (u_ref[:, sl].astype(F32) * mixed).astype(a_ref.dtype)


def _gmlp(u, v, mats, bias, n_first):
    t, a = u.shape
    _, groups, c, _ = mats.shape
    sel = lambda n: (jnp.where(n < n_first, 0, 1), 0, 0)
    return pl.pallas_call(
        functools.partial(_gmlp_kernel, groups=groups, gd=a // groups),
        grid=(t // c,),
        in_specs=[pl.BlockSpec((c, a), lambda n: (n, 0)),
                  pl.BlockSpec((c, a), lambda n: (n, 0)),
                  pl.BlockSpec((1, groups, c, c), lambda n: (jnp.where(n < n_first, 0, 1), 0, 0, 0)),
                  pl.BlockSpec((1, c, a), sel)],
        out_specs=pl.BlockSpec((c, a), lambda n: (n, 0)),
        out_shape=jax.ShapeDtypeStruct((t, a), BF16),
        compiler_params=_params("parallel"),
        name="gmlp",
    )(u, v, mats, bias)


def _softmax_rows(s_ref, p_ref, a_ref, m_ref, l_ref, rc, mask_fn=None, static_rows=None):
    width = s_ref.shape[1]

    def chunk(rs, row0):
        s = s_ref[rs, :]
        if mask_fn is not None:
            s = mask_fn(s, row0)
        m_prev = m_ref[rs, :]
        m_new = jnp.maximum(m_prev, jnp.max(s, axis=-1, keepdims=True))
        alpha = jnp.exp(m_prev - m_new)
        p = jnp.exp(s - jnp.tile(m_new, (1, width // LANES)))
        l_ref[rs, :] = alpha * l_ref[rs, :] + jnp.sum(p, axis=-1, keepdims=True)
        m_ref[rs, :] = m_new
        a_ref[rs, :] = alpha
        p_ref[rs, :] = p.astype(p_ref.dtype)

    if static_rows is not None:
        for row0 in range(static_rows[0], static_rows[1], rc):
            chunk(slice(row0, row0 + rc), row0)
        return

    def body(r, carry):
        chunk(pl.ds(pl.multiple_of(r * rc, rc), rc), r * rc)
        return carry

    n = s_ref.shape[0] // rc
    lax.fori_loop(0, n, body, 0, unroll=math.gcd(n, SOFTMAX_UNROLL))


def _rescale_add(acc_ref, a_ref, pv):
    acc_ref[...] = acc_ref[...] * jnp.tile(a_ref[...], (1, acc_ref.shape[1] // LANES)) + pv


def _attn_prompt_kernel(ql_ref, qp_ref, kc_ref, kp_ref, wv_ref, o_ref, m_ref, l_ref, acc_ref, s_ref, p_ref, a_ref,
                        *, heads, bq, bk, vd):
    i = pl.program_id(0)
    j = pl.program_id(1)
    rows = heads * bq

    @pl.when(j == 0)
    def _():
        m_ref[...] = jnp.full(m_ref.shape, NEG_INF, F32)
        l_ref[...] = jnp.zeros(l_ref.shape, F32)
        acc_ref[...] = jnp.zeros(acc_ref.shape, F32)

    def causal(s, row0):
        q_pos = i * bq + (row0 + lax.broadcasted_iota(jnp.int32, s.shape, 0)) % bq
        k_pos = j * bk + lax.broadcasted_iota(jnp.int32, s.shape, 1)
        return jnp.where(k_pos <= q_pos, s, NEG_INF)

    def step(mask_fn):
        hg = heads // math.gcd(heads, ATTN_ROW_GROUPS)
        gr = hg * bq
        kc = kc_ref[...]
        for g in range(heads // hg):
            q = ql_ref[g * hg:(g + 1) * hg].reshape(gr, ql_ref.shape[-1])
            qp = qp_ref[g * hg:(g + 1) * hg].reshape(gr, qp_ref.shape[-1])
            s_ref[g * gr:(g + 1) * gr, :] = _dot_nt(q, kc) + _dot_nt(qp, kp_ref[...])
        for g in range(heads // hg):
            sl = slice(g * gr, (g + 1) * gr)
            _softmax_rows(s_ref, p_ref, a_ref, m_ref, l_ref, SOFTMAX_ROWS, mask_fn, static_rows=(g * gr, (g + 1) * gr))
            acc_ref[sl, :] = (acc_ref[sl, :] * jnp.tile(a_ref[sl, :], (1, acc_ref.shape[1] // LANES))
                              + _dot(p_ref[sl, :], kc))

    needed = j * bk <= i * bq + bq - 1
    crosses_diagonal = j * bk + bk - 1 > i * bq

    @pl.when(needed & crosses_diagonal)
    def _():
        step(causal)

    @pl.when(needed & jnp.logical_not(crosses_diagonal))
    def _():
        step(None)

    @pl.when(j == pl.num_programs(1) - 1)
    def _():
        o = (acc_ref[...] / l_ref[...][:, :1]).astype(BF16)
        for h in range(heads):
            o_ref[:, h * vd:(h + 1) * vd] = _dot(o[h * bq:(h + 1) * bq], wv_ref[h]).astype(o_ref.dtype)


def _attn_prompt(ql, qp, kc, kp, wv, n_prompt):
    heads, _, kv = ql.shape
    rope = qp.shape[-1]
    vd = wv.shape[-1]
    bq = _tile(n_prompt, ATTN_BQ, SOFTMAX_ROWS)
    bk = _tile(n_prompt, ATTN_BK, LANES)
    nq, nk = n_prompt // bq, n_prompt // bk
    rows = heads * bq

    def kmap(i, j):
        return (jnp.minimum(j, (i * bq + bq - 1) // bk), 0)

    return pl.pallas_call(
        functools.partial(_attn_prompt_kernel, heads=heads, bq=bq, bk=bk, vd=vd),
        grid=(nq, nk),
        in_specs=[pl.BlockSpec((heads, bq, kv), lambda i, j: (0, i, 0)),
                  pl.BlockSpec((heads, bq, rope), lambda i, j: (0, i, 0)),
                  pl.BlockSpec((bk, kv), kmap),
                  pl.BlockSpec((bk, rope), kmap),
                  pl.BlockSpec(wv.shape, lambda i, j: (0, 0, 0))],
        out_specs=pl.BlockSpec((bq, heads * vd), lambda i, j: (i, 0)),
        out_shape=jax.ShapeDtypeStruct((n_prompt, heads * vd), BF16),
        scratch_shapes=[pltpu.VMEM((rows, LANES), F32), pltpu.VMEM((rows, LANES), F32),
                        pltpu.VMEM((rows, kv), F32), pltpu.VMEM((rows, bk), F32),
                        pltpu.VMEM((rows, bk), BF16), pltpu.VMEM((rows, LANES), F32)],
        compiler_params=_params("parallel", "arbitrary"),
        name="attn_prompt",
    )(ql, qp, kc, kp, wv)


def _attn_sample_kernel(pt_ref, ql_ref, qp_ref, kcn_ref, kpn_ref, wv_ref, ckv_hbm, kpe_hbm, o_ref,
                        ckv_buf, kpe_buf, sem, kc_s, kp_s, m_ref, l_ref, acc_ref, s_ref, p_ref, a_ref,
                        *, heads, lq, seqs, pages, page, vd):
    b = pl.program_id(0)
    j = pl.program_id(1)
    steps = pl.num_programs(1)
    g = b * steps + j
    total = pl.num_programs(0) * steps
    n_slots = ckv_buf.shape[0]
    ahead = n_slots - 1
    slot = g % n_slots

    def page_copies(gg, to_slot):
        bb, jj = gg // steps, gg % steps
        copies = []
        for e in range(seqs):
            for r in range(pages):
                pg = pt_ref[bb * seqs + e, jj * pages + r]
                k = e * pages + r
                copies.append(pltpu.make_async_copy(ckv_hbm.at[pg], ckv_buf.at[to_slot, k], sem.at[to_slot]))
                copies.append(pltpu.make_async_copy(kpe_hbm.at[pg], kpe_buf.at[to_slot, k], sem.at[to_slot]))
        return copies

    @pl.when(g == 0)
    def _():
        for first in range(ahead):
            @pl.when(first < total)
            def _():
                for cp in page_copies(first, first):
                    cp.start()

    @pl.when(g + ahead < total)
    def _():
        for cp in page_copies(g + ahead, (g + ahead) % n_slots):
            cp.start()

    for cp in page_copies(g, slot):
        cp.wait()
    ckv_pages = [ckv_buf.at[slot, k] for k in range(seqs * pages)]
    kpe_pages = [kpe_buf.at[slot, k] for k in range(seqs * pages)]

    @pl.when(j == 0)
    def _():
        for e in range(seqs):
            kcn = kcn_ref[e]
            s = _dot_nt(ql_ref[e].astype(F32), kcn) + _dot_nt(qp_ref[e].astype(F32), kpn_ref[e])
            t_q = lax.broadcasted_iota(jnp.int32, s.shape, 0) % lq
            t_k = lax.broadcasted_iota(jnp.int32, s.shape, 1)
            s = jnp.where(t_k <= t_q, s, NEG_INF)
            m0 = jnp.max(s, axis=-1, keepdims=True)
            p = jnp.exp(s - m0)
            m_ref[e] = jnp.broadcast_to(m0, m_ref.shape[1:])
            l_ref[e] = jnp.broadcast_to(jnp.sum(p, axis=-1, keepdims=True), l_ref.shape[1:])
            acc_ref[e] = _dot(p, kcn)

    for e in range(seqs):
        for r in range(pages):
            kc_s[e, r * page:(r + 1) * page, :] = ckv_pages[e * pages + r][...].astype(BF16)
            kp_s[e, :, r * page:(r + 1) * page] = kpe_pages[e * pages + r][...].astype(BF16)
        s_ref[e] = _dot_nt(ql_ref[e], kc_s[e]) + _dot(qp_ref[e], kp_s[e])
    for e in range(seqs):
        _softmax_rows(s_ref.at[e], p_ref.at[e], a_ref.at[e], m_ref.at[e], l_ref.at[e], 2 * SUBLANES,
                      static_rows=(0, s_ref.shape[1]))
        _rescale_add(acc_ref.at[e], a_ref.at[e], _dot(p_ref[e], kc_s[e]))

    @pl.when(j == pl.num_programs(1) - 1)
    def _():
        for e in range(seqs):
            o = (acc_ref[e] / l_ref[e][:, :1]).astype(BF16)
            for h in range(heads):
                full = _dot(o, wv_ref[h])
                o_ref[e, :, h * vd:(h + 1) * vd] = full[h * lq:(h + 1) * lq].astype(o_ref.dtype)


def _attn_sample(page_table, ql, qp, kc_new, kp_new, wv, cache_ckv, cache_kpe_t):
    nb, rows, kv = ql.shape
    rope = qp.shape[-1]
    heads, _, vd = wv.shape
    lq = rows // heads
    n_pages = page_table.shape[1]
    page = cache_ckv.shape[1]
    seqs = math.gcd(nb, SAMPLE_SEQS_PER_STEP)
    pages = _tile(n_pages, SAMPLE_PAGE_OPERANDS // seqs, 1)
    steps = n_pages // pages

    per_seq = lambda *shape: pl.BlockSpec((seqs,) + shape, lambda b, j, pt: (b, 0, 0))
    in_hbm = pl.BlockSpec(memory_space=pl.ANY)
    keys = pages * page
    n_ops = seqs * pages
    grid_spec = pltpu.PrefetchScalarGridSpec(
        num_scalar_prefetch=1,
        grid=(nb // seqs, steps),
        in_specs=[per_seq(rows, kv), per_seq(rows, rope), per_seq(lq, kv), per_seq(lq, rope),
                  pl.BlockSpec(wv.shape, lambda b, j, pt: (0, 0, 0)), in_hbm, in_hbm],
        out_specs=per_seq(lq, heads * vd),
        scratch_shapes=[pltpu.VMEM((SAMPLE_PAGE_SLOTS, n_ops, page, kv), cache_ckv.dtype),
                        pltpu.VMEM((SAMPLE_PAGE_SLOTS, n_ops, rope, page), cache_kpe_t.dtype),
                        pltpu.SemaphoreType.DMA((SAMPLE_PAGE_SLOTS,)),
                        pltpu.VMEM((seqs, keys, kv), BF16), pltpu.VMEM((seqs, rope, keys), BF16),
                        pltpu.VMEM((seqs, rows, LANES), F32), pltpu.VMEM((seqs, rows, LANES), F32),
                        pltpu.VMEM((seqs, rows, kv), F32), pltpu.VMEM((seqs, rows, keys), F32),
                        pltpu.VMEM((seqs, rows, keys), BF16), pltpu.VMEM((seqs, rows, LANES), F32)],
    )
    return pl.pallas_call(
        functools.partial(_attn_sample_kernel, heads=heads, lq=lq, seqs=seqs, pages=pages, page=page, vd=vd),
        grid_spec=grid_spec,
        out_shape=jax.ShapeDtypeStruct((nb, lq, heads * vd), BF16),
        compiler_params=_params("arbitrary", "arbitrary"),
        name="attn_sample",
    )(page_table, ql, qp, kc_new, kp_new, wv, cache_ckv, cache_kpe_t)


def _cast_kernel(*refs):
    o_ref = refs[-1]
    col = 0
    for x_ref in refs[:-1]:
        o_ref[:, col:col + x_ref.shape[1]] = x_ref[...].astype(o_ref.dtype)
        col += x_ref.shape[1]


def _cast_rows(parts, dtype):
    s = parts[0].shape[0]
    d = sum(p.shape[1] for p in parts)
    tm = _tile(s, 512, 16)
    return pl.pallas_call(
        _cast_kernel, grid=(s // tm,),
        in_specs=[pl.BlockSpec((tm, p.shape[1]), lambda i: (i, 0)) for p in parts],
        out_specs=pl.BlockSpec((tm, d), lambda i: (i, 0)),
        out_shape=jax.ShapeDtypeStruct((s, d), dtype), compiler_params=_params("parallel"), name="cast_rows",
    )(*parts)


def _ffn_kernel(be_ref, bv_ref, x_ref, wg_ref, wu_ref, wd_ref, gate_ref, o_ref, wg_s, wu_s, wd_s, *, tm, ts):
    i = pl.program_id(0)
    j = pl.program_id(1)
    valid = bv_ref[i]

    @pl.when(j == 0)
    def _():
        o_ref[...] = jnp.zeros(o_ref.shape, F32)

    @pl.when(valid > 0)
    def _():
        wg_s[...] = wg_ref[0].astype(BF16)
        wu_s[...] = wu_ref[0].astype(BF16)
        wd_s[...] = wd_ref[0].astype(BF16)

    def swiglu(rows):
        x = x_ref[rows, :]
        h = (jax.nn.silu(_dot(x, wg_s[...])) * _dot(x, wu_s[...])).astype(BF16)
        o_ref[rows, :] += _dot(h, wd_s[...])

    @pl.when(valid == tm)
    def _():
        chunk = tm // math.gcd(tm // 16, FFN_FULL_BLOCK_CHUNKS)
        for c in range(tm // chunk):
            swiglu(slice(c * chunk, (c + 1) * chunk))

    for sub in range(tm // ts):
        @pl.when((valid > sub * ts) & (valid < tm))
        def _():
            swiglu(slice(sub * ts, (sub + 1) * ts))

    @pl.when(j == pl.num_programs(1) - 1)
    def _():
        o_ref[...] = o_ref[...] * gate_ref[...]


def _ffn(x, wg, wu, wd, gate, blk_expert, blk_valid, *, tm, tf_pref=512, ts_pref=256):
    s, d = x.shape
    f = wg.shape[2]
    tf = _tile(f, tf_pref, LANES)
    ts = _tile(tm, ts_pref, 16)
    nj = f // tf

    def jj(i, j, bv):
        return jnp.where(bv[i] > 0, j, nj - 1)

    once = pl.Buffered(1)
    scratch = [pltpu.VMEM((d, tf), BF16), pltpu.VMEM((d, tf), BF16), pltpu.VMEM((tf, d), BF16)]
    grid_spec = pltpu.PrefetchScalarGridSpec(
        num_scalar_prefetch=2,
        grid=(s // tm, nj),
        in_specs=[pl.BlockSpec((tm, d), lambda i, j, be, bv: (i, 0), pipeline_mode=once),
                  pl.BlockSpec((1, d, tf), lambda i, j, be, bv: (be[i], 0, jj(i, j, bv))),
                  pl.BlockSpec((1, d, tf), lambda i, j, be, bv: (be[i], 0, jj(i, j, bv))),
                  pl.BlockSpec((1, tf, d), lambda i, j, be, bv: (be[i], jj(i, j, bv), 0)),
                  pl.BlockSpec((tm, 1), lambda i, j, be, bv: (i, 0), pipeline_mode=once)],
        out_specs=pl.BlockSpec((tm, d), lambda i, j, be, bv: (i, 0), pipeline_mode=once),
        scratch_shapes=scratch,
    )
    return pl.pallas_call(
        functools.partial(_ffn_kernel, tm=tm, ts=ts),
        grid_spec=grid_spec,
        out_shape=jax.ShapeDtypeStruct((s, d), F32),
        compiler_params=_params("arbitrary", "arbitrary"),
        name="ffn",
    )(blk_expert, blk_valid, x, wg, wu, wd, gate)


def _split3(a):
    a1 = a.astype(BF16)
    r1 = a - a1.astype(F32)
    a2 = r1.astype(BF16)
    a3 = (r1 - a2.astype(F32)).astype(BF16)
    return a1, a2, a3


def _router_kernel(h_ref, d_ref, g_ref, wr_ref, hs_ref, comb_ref, sel_ref, *hn_refs):
    x = h_ref[...] + d_ref[...]
    hs_ref[...] = x
    y = _rms(x, g_ref[...])
    width = y.shape[1] // len(hn_refs)
    for n, hn_ref in enumerate(hn_refs):
        hn_ref[...] = y[:, n * width:(n + 1) * width]
    ys = _split3(y)
    ws = _split3(wr_ref[...])
    logits = jnp.zeros((x.shape[0], wr_ref.shape[1]), F32)
    for a in range(3):
        for b in range(3 - a):
            logits = logits + _dot(ys[a], ws[b])
    n_e = logits.shape[1]
    lane = lax.broadcasted_iota(jnp.int32, logits.shape, 1).astype(F32)
    m1 = jnp.max(logits, axis=-1, keepdims=True)
    i1 = jnp.min(jnp.where(logits == m1, lane, n_e), axis=-1, keepdims=True)
    first = lane == i1
    rest = jnp.where(first, NEG_INF, logits)
    m2 = jnp.max(rest, axis=-1, keepdims=True)
    i2 = jnp.min(jnp.where(rest == m2, lane, n_e), axis=-1, keepdims=True)
    second = lane == i2
    e = jnp.exp(m2 - m1)
    den = 1.0 + e
    comb_ref[...] = jnp.where(first, 1.0 / den, 0.0) + jnp.where(second, e / den, 0.0)
    sel_ref[...] = jnp.where(first | second, 1.0, 0.0)


def _router(h, delta, gain, w_router, n_parts):
    t, d = h.shape
    n_e = w_router.shape[1]
    tm = _tile(t, 256)
    row = pl.BlockSpec((tm, d), lambda i: (i, 0))
    small = pl.BlockSpec((tm, n_e), lambda i: (i, 0))
    part = pl.BlockSpec((tm, d // n_parts), lambda i: (i, 0))
    h_sum, comb, sel, *hn_parts = pl.pallas_call(
        _router_kernel,
        grid=(t // tm,),
        in_specs=[row, row, pl.BlockSpec((1, d), lambda i: (0, 0)), pl.BlockSpec((d, n_e), lambda i: (0, 0))],
        out_specs=[row, small, small] + [part] * n_parts,
        out_shape=[jax.ShapeDtypeStruct((t, d), F32),
                   jax.ShapeDtypeStruct((t, n_e), F32), jax.ShapeDtypeStruct((t, n_e), F32)]
                  + [jax.ShapeDtypeStruct((t, d // n_parts), F32)] * n_parts,
        compiler_params=_params("parallel"),
        name="router",
    )(h, delta, gain.reshape(1, d), w_router)
    return h_sum, hn_parts, comb, sel


def _roll_rows(x, shift):
    n = x.shape[0]
    shift = shift % n
    return x if shift == 0 else pltpu.roll(x, shift, 0)


def _hgrn_inputs(zq, zf, lb):
    q = jax.nn.silu(zq)
    lf = jnp.log(lb + (1.0 - lb) * jax.nn.sigmoid(zf))
    k = (1.0 - lb) * jax.nn.sigmoid(-zf)
    return q, k, lf


def _prefix8(lf, r8):
    p = lf
    for sh in (1, 2, 4):
        p = p + jnp.where(r8 >= sh, _roll_rows(p, sh), 0.0)
    return p


def _block8_intra(q, k, v, p8, r8):
    o = jnp.sum(q * k, axis=-1, keepdims=True) * v
    for d in range(1, SUBLANES):
        e = jnp.exp(jnp.where(r8 >= d, p8 - _roll_rows(p8, d), NEG_INF))
        w = jnp.sum(q * _roll_rows(k, d) * e, axis=-1, keepdims=True)
        o = o + w * _roll_rows(v, d)
    return o


def _hgrn_finish(o, zg, gw):
    return _rms(o, gw) * jax.nn.silu(zg)


def _hgrn_prompt_kernel(zq_ref, zf_ref, zi_ref, zg_ref, lb_ref, gw_ref, o_ref, st_ref, st_t, *, c, n_chunks, hp, kd):
    @pl.when(pl.program_id(1) == 0)
    def _():
        st_t[...] = jnp.zeros(st_t.shape, F32)

    gw = gw_ref[...]
    row = lax.broadcasted_iota(jnp.int32, (c, kd), 0)
    r8 = row % SUBLANES
    ri = lax.broadcasted_iota(jnp.int32, (c, c), 0)
    ci = lax.broadcasted_iota(jnp.int32, (c, c), 1)

    def one_head(rows, hh):
        cols = slice(hh * kd, (hh + 1) * kd)
        q, k, lf = _hgrn_inputs(zq_ref[rows, cols], zf_ref[rows, cols], lb_ref[:, cols])
        v = zi_ref[rows, cols]
        p = _prefix8(lf, r8)
        o = _block8_intra(q, k, v, p, r8)
        tot = jnp.where(r8 == SUBLANES - 1, p, 0.0)
        for sh in (1, 2, 4):
            tot = tot + _roll_rows(tot, -sh)
        a = jnp.zeros((c, c), F32)
        s = SUBLANES
        while s < c:
            second = row % (2 * s) >= s
            qs = (q * jnp.exp(jnp.where(second, p, NEG_INF))).astype(BF16)
            ks = (k * jnp.exp(jnp.where(second, NEG_INF, tot - p))).astype(BF16)
            a_s = _dot_nt(qs, ks)
            a = a + (a_s if 2 * s == c else jnp.where(ri // (2 * s) == ci // (2 * s), a_s, 0.0))
            prev = _roll_rows(tot, s)
            p = p + jnp.where(second, prev, 0.0)
            tot = tot + jnp.where(second, prev, _roll_rows(tot, -s))
            s *= 2
        st = st_t[hh]
        o = o + _dot(a.astype(BF16), v.astype(BF16)) + _dot_nt((q * jnp.exp(p)).astype(BF16), st.astype(BF16))
        kb = (k * jnp.exp(tot - p)).astype(BF16)
        st_t[hh] = st * jnp.exp(tot[0:1, :]) + _dot(v.T.astype(BF16), kb)
        o_ref[rows, cols] = _hgrn_finish(o, zg_ref[rows, cols], gw).astype(o_ref.dtype)

    def chunk(n, carry):
        rows = pl.ds(pl.multiple_of(n * c, c), c)
        for hh in range(hp):
            one_head(rows, hh)
        return carry

    lax.fori_loop(0, n_chunks, chunk, 0)

    @pl.when(pl.program_id(1) == pl.num_programs(1) - 1)
    def _():
        for hh in range(hp):
            st_ref[hh] = st_t[hh].T


def _hgrn_prompt(z, lb, gw, n_prompt, heads):
    kd = lb.shape[1] // heads
    c = HGRN_CHUNK
    hp = math.gcd(heads, HGRN_HEADS_PER_STEP)
    lbk = _tile(n_prompt, 1024, c)
    groups = heads // hp

    def zspec(seg):
        return pl.BlockSpec((lbk, hp * kd), lambda h, l: (l, seg * groups + h))

    return pl.pallas_call(
        functools.partial(_hgrn_prompt_kernel, c=c, n_chunks=lbk // c, hp=hp, kd=kd),
        grid=(groups, n_prompt // lbk),
        in_specs=[zspec(0), zspec(1), zspec(2), zspec(3),
                  pl.BlockSpec((1, hp * kd), lambda h, l: (0, h)),
                  pl.BlockSpec((1, kd), lambda h, l: (0, 0))],
        out_specs=[pl.BlockSpec((lbk, hp * kd), lambda h, l: (l, h)),
                   pl.BlockSpec((hp, kd, kd), lambda h, l: (h, 0, 0))],
        out_shape=[jax.ShapeDtypeStruct((n_prompt, heads * kd), BF16),
                   jax.ShapeDtypeStruct((heads, kd, kd), F32)],
        scratch_shapes=[pltpu.VMEM((hp, kd, kd), F32)],
        compiler_params=_params("parallel", "arbitrary"),
        name="hgrn_prompt",
    )(z, z, z, z, lb, gw)


def _hgrn_sample_kernel(z_ref, s0_ref, lb_ref, gw_ref, o_ref, s1_ref, *, heads, kd):
    lq = z_ref.shape[0]
    gw = gw_ref[...]
    r8 = lax.broadcasted_iota(jnp.int32, (lq, kd), 0)
    for h in range(heads):
        col = lambda seg: slice((seg * heads + h) * kd, (seg * heads + h + 1) * kd)
        q, k, lf = _hgrn_inputs(z_ref[:, col(0)], z_ref[:, col(1)], lb_ref[:, h * kd:(h + 1) * kd])
        v = z_ref[:, col(2)]
        p = _prefix8(lf, r8)
        last = p[lq - 1:lq, :]
        st = s0_ref[0, h]
        o = _block8_intra(q, k, v, p, r8) + _dot((q * jnp.exp(p)).astype(BF16), st.astype(BF16))
        cols = jnp.concatenate([k * jnp.exp(last - p), p], axis=0).T
        v_pad = jnp.concatenate([v, jnp.zeros_like(v)], axis=0)
        s1_ref[0, h] = st * jnp.exp(cols[:, 2 * lq - 1:2 * lq]) + _dot(cols.astype(BF16), v_pad.astype(BF16))
        o_ref[:, h * kd:(h + 1) * kd] = _hgrn_finish(o, z_ref[:, col(3)], gw).astype(o_ref.dtype)


def _hgrn_sample(z, state, lb, gw, row0):
    nb, heads, kd, _ = state.shape
    lq = SUBLANES
    blk0 = row0 // lq
    return pl.pallas_call(
        functools.partial(_hgrn_sample_kernel, heads=heads, kd=kd),
        grid=(nb,),
        in_specs=[pl.BlockSpec((lq, z.shape[1]), lambda b: (blk0 + b, 0)),
                  pl.BlockSpec((1, heads, kd, kd), lambda b: (b, 0, 0, 0)),
                  pl.BlockSpec((1, heads * kd), lambda b: (0, 0)),
                  pl.BlockSpec((1, kd), lambda b: (0, 0))],
        out_specs=[pl.BlockSpec((lq, heads * kd), lambda b: (b, 0)),
                   pl.BlockSpec((1, heads, kd, kd), lambda b: (b, 0, 0, 0))],
        out_shape=[jax.ShapeDtypeStruct((nb * lq, heads * kd), F32),
                   jax.ShapeDtypeStruct(state.shape, F32)],
        compiler_params=_params("parallel"),
        name="hgrn_sample",
    )(z, state, lb, gw)


def _route_plan(sel, comb, tm):
    t, n_e = sel.shape
    n_blk = -(-(TOP_K * t) // tm) + n_e
    n_slot = n_blk * tm
    picked = sel > 0
    seli = picked.astype(jnp.int32)
    incl = jnp.cumsum(seli, axis=0)
    counts = incl[-1]
    blocks_per = (counts + tm - 1) // tm
    blk_end = jnp.cumsum(blocks_per)
    start = (blk_end - blocks_per) * tm
    slot = start[None, :] + incl - seli
    slot_a = jnp.min(jnp.where(picked, slot, n_slot), axis=1)
    slot_b = jnp.max(jnp.where(picked, slot, -1), axis=1)
    tok_slots = jnp.stack([slot_a, slot_b], axis=1)
    gates = jnp.stack([jnp.sum(jnp.where(picked & (slot == s[:, None]), comb, 0.0), axis=1)
                       for s in (slot_a, slot_b)], axis=1).reshape(-1)
    pair = jnp.arange(TOP_K * t, dtype=jnp.int32)
    pair_of_slot = jnp.full((n_slot,), -1, jnp.int32).at[tok_slots.reshape(-1)].set(pair, unique_indices=True)
    filled = pair_of_slot >= 0
    safe = jnp.maximum(pair_of_slot, 0)
    tok_of_slot = safe // TOP_K
    gate_of_slot = jnp.where(filled, gates.at[safe].get(mode="promise_in_bounds"), 0.0)
    blk = jnp.arange(n_blk, dtype=jnp.int32)
    used = blk_end[-1]
    e_of_blk = jnp.sum((jnp.minimum(blk, used - 1)[:, None] >= blk_end[None, :]).astype(jnp.int32), axis=1)
    e_of_blk = jnp.minimum(e_of_blk, n_e - 1)
    valid = jnp.clip(counts[e_of_blk] - (blk * tm - start[e_of_blk]), 0, tm)
    valid = jnp.where(blk < used, valid, 0).astype(jnp.int32)
    return tok_of_slot, gate_of_slot, e_of_blk, valid, tok_slots


def kernel(x_prompt, x_sample, cache_ckv, cache_kpe, state_hgrn, page_table, norm_mix0, w_in_ab, ln_v_w, ln_v_b, w_s, b_s, q_norm_w, w_q_b, kv_norm_w, w_kv_b, w_out_ab, norm_ffn0, w_ffn_gate, w_ffn_up, w_ffn_down, norm_mix1, w_in_c, lb_logits, g_norm_w, w_out_c, norm_ffn1, w_router, w_exp_gate, w_exp_up, w_exp_down, norm_final):
    bp, lp, d = x_prompt.shape
    nb, lq, _ = x_sample.shape
    assert bp == 1, "the prompt group is one sequence"
    n_p, n_s = bp * lp, nb * lq
    t = n_p + n_s
    groups, chunk, _ = w_s.shape
    a_w = ln_v_w.shape[0]
    q_lora, heads, qk = w_q_b.shape
    kv_lora = kv_norm_w.shape[0]
    rope = cache_kpe.shape[-1]
    nope = qk - rope
    vd = w_kv_b.shape[-1] - nope
    page = cache_ckv.shape[1]
    past = page_table.shape[1] * page
    c_heads, c_k = state_hgrn.shape[1], state_hgrn.shape[2]
    c_f = c_heads * c_k
    assert lq == SUBLANES and chunk % lq == 0 and n_p % chunk == 0 and n_s % chunk == 0
    off_q, off_kv = 2 * a_w, 2 * a_w + q_lora
    scale = float(qk) ** -0.5

    x = jnp.concatenate([x_prompt.reshape(n_p, d), x_sample.reshape(n_s, d)], axis=0)

    pos = jnp.concatenate([jnp.arange(lp, dtype=F32), jnp.tile(past + jnp.arange(lq, dtype=F32), nb)])
    inv_freq = ROPE_THETA ** (-jnp.arange(rope // 2, dtype=F32) / (rope // 2))
    ang = pos[:, None] * inv_freq[None, :]
    cos, sin = jnp.cos(ang), jnp.sin(ang)

    hn = _add_norm([x], norm_mix0, want_sum=False, out_dtype=BF16)[0]
    (u,) = _mm(hn, w_in_ab[:, :a_w], _ep_gelu, [(a_w, BF16)], name="in_u")
    (v,) = _mm(hn, w_in_ab[:, a_w:off_q], _ep_gelu_ln, [(a_w, F32)],
               [(ln_v_w.reshape(1, a_w), "col"), (ln_v_b.reshape(1, a_w), "col")], name="in_v")
    (cq,) = _mm(hn, w_in_ab[:, off_q:off_kv], _ep_rms, [(q_lora, BF16)],
                [(q_norm_w.reshape(1, q_lora), "col")], name="in_cq")
    kvw = kv_lora + rope
    w_kvpe = w_in_ab[:, off_kv:]
    t_kv = _tile(t, 512, 16)
    ckv, kpe, ckv_b, kpe_b = pl.pallas_call(
        functools.partial(_mm_kernel, n_extra=3, n_out=4, epilogue=functools.partial(_ep_kv, kv_lora=kv_lora)),
        grid=(1, t // t_kv),
        in_specs=[pl.BlockSpec((t_kv, d), lambda j, i: (i, 0)),
                  pl.BlockSpec((d, kvw), lambda j, i: (0, 0)),
                  pl.BlockSpec((t_kv, rope // 2), lambda j, i: (i, 0)),
                  pl.BlockSpec((t_kv, rope // 2), lambda j, i: (i, 0)),
                  pl.BlockSpec((1, kv_lora), lambda j, i: (0, 0))],
        out_specs=[pl.BlockSpec((t_kv, kv_lora), lambda j, i: (i, 0)),
                   pl.BlockSpec((t_kv, rope), lambda j, i: (i, 0)),
                   pl.BlockSpec((t_kv, kv_lora), lambda j, i: (i, 0)),
                   pl.BlockSpec((t_kv, rope), lambda j, i: (i, 0))],
        out_shape=[jax.ShapeDtypeStruct((t, kv_lora), F32), jax.ShapeDtypeStruct((t, rope), F32),
                   jax.ShapeDtypeStruct((t, kv_lora), BF16), jax.ShapeDtypeStruct((t, rope), BF16)],
        scratch_shapes=[pltpu.VMEM((d, kvw), BF16)],
        compiler_params=_params("arbitrary", "arbitrary"),
        name="in_kv",
    )(hn, w_kvpe, cos, sin, kv_norm_w.reshape(1, kv_lora))

    wqn = w_q_b[:, :, :nope].reshape(q_lora, heads * nope).astype(BF16)
    wqr = w_q_b[:, :, nope:].reshape(q_lora, heads * rope).astype(BF16)
    wabs = jnp.transpose(w_kv_b[:, :, :nope], (1, 2, 0)).astype(BF16)
    wv = jnp.transpose(w_kv_b[:, :, nope:], (1, 0, 2)).astype(BF16)
    ql, qp = _q_proj(cq, wqn, wqr, wabs, cos, sin, scale)

    o_prompt = _attn_prompt(ql, qp, ckv_b, kpe_b, wv, n_p)
    to_rows = lambda a: jnp.transpose(a[:, n_p:].reshape(heads, nb, lq, a.shape[-1]), (1, 0, 2, 3)).reshape(
        nb, heads * lq, a.shape[-1])
    o_sample = _attn_sample(page_table, to_rows(ql), to_rows(qp),
                            ckv[n_p:].reshape(nb, lq, kv_lora), kpe[n_p:].reshape(nb, lq, rope),
                            wv, cache_ckv, jnp.swapaxes(cache_kpe, 1, 2))

    tril = jnp.tril(jnp.ones((chunk, chunk), dtype=bool))
    m_prompt = jnp.where(tril[None], w_s, 0.0)
    eye = jnp.eye(chunk // lq, dtype=F32)
    m_sample = jnp.einsum("ab,gts->gatbs", eye, m_prompt[:, :lq, :lq]).reshape(groups, chunk, chunk)
    mats = jnp.stack([m_prompt, m_sample]).astype(BF16)
    bias_p = jnp.repeat(b_s.T, a_w // groups, axis=1)
    bias = jnp.stack([bias_p, jnp.tile(bias_p[:lq], (chunk // lq, 1))])
    a_mix = _gmlp(u, v, mats, bias, n_p // chunk)

    mix = jnp.concatenate([a_mix, jnp.concatenate([o_prompt, o_sample.reshape(n_s, heads * vd)], axis=0)], axis=1)
    (h1,) = _mm(mix, w_out_ab, _ep_add, [(_tile(d, 1024, LANES), F32)], [(x, "tile")], name="out_ab")

    hn = _add_norm([h1], norm_ffn0, want_sum=False, out_dtype=BF16)[0]
    tm_f = DENSE_BLOCK_ROWS if t % DENSE_BLOCK_ROWS == 0 else _tile(t, 1024, 16)
    nblk = t // tm_f
    y_ffn = _ffn(hn, w_ffn_gate[None], w_ffn_up[None], w_ffn_down[None], jnp.ones((t, 1), F32),
                 jnp.zeros((nblk,), jnp.int32), jnp.full((nblk,), tm_f, jnp.int32), tm=tm_f)

    h2, hn = _add_norm([h1, y_ffn], norm_mix1, want_sum=True, out_dtype=BF16)
    (z,) = _mm(hn, w_in_c, _ep_id, [(_tile(w_in_c.shape[1], 1024, LANES), F32)], name="in_c")
    lb_cum = jnp.cumsum(jax.nn.softmax(lb_logits.astype(F32), axis=0), axis=0)
    lb = (lb_cum[1] - lb_cum[0]).reshape(1, c_f)
    gw = g_norm_w.reshape(1, -1)
    o_p, state_prompt = _hgrn_prompt(z, lb, gw, n_p, c_heads)
    o_s, state_sample = _hgrn_sample(z, state_hgrn, lb, gw, n_p)
    o_c = jnp.concatenate([o_p, o_s.astype(BF16)], axis=0)
    (d3,) = _mm(o_c, w_out_c, _ep_id, [(_tile(d, 1024, LANES), F32)], name="out_c")

    h3, hn_parts, comb, sel = _router(h2, d3, norm_ffn1, w_router, DISPATCH_ROW_PARTS)
    tm_e = MOE_BLOCK_ROWS if TOP_K * t >= 4 * MOE_BLOCK_ROWS else _tile(t, 1024, 16)
    tok_of_slot, gate_of_slot, e_of_blk, valid, tok_slots = _route_plan(sel, comb, tm_e)
    rows_of = lambda a, idx: a.at[idx].get(mode="promise_in_bounds")
    x_sorted = _cast_rows([rows_of(part, tok_of_slot) for part in hn_parts], BF16)
    y_sorted = _ffn(x_sorted, w_exp_gate, w_exp_up, w_exp_down,
                    gate_of_slot[:, None], e_of_blk, valid, tm=tm_e)
    parts = [h3] + [rows_of(y_sorted, tok_slots[:, r]) for r in range(TOP_K)]
    y = _add_norm(parts, norm_final, want_sum=False, out_dtype=F32)[0]

    y_prompt = y[:n_p].reshape(bp, lp, d)
    y_sample = y[n_p:].reshape(nb, lq, d)
    return (y_prompt, y_sample,
            ckv[:n_p].reshape(bp, lp, kv_lora), kpe[:n_p].reshape(bp, lp, rope),
            ckv[n_p:].reshape(nb, lq, kv_lora), kpe[n_p:].reshape(nb, lq, rope),
            v[n_p:].reshape(nb, lq, a_w),
            state_prompt.reshape(bp, c_heads, c_k, -1), state_sample)
```

```python
import functools
import math

import jax
import jax.numpy as jnp
from jax import lax
from jax.experimental import pallas as pl
from jax.experimental.pallas import tpu as pltpu

F32 = jnp.float32
BF16 = jnp.bfloat16
EPS = 1e-6
ROPE_THETA = 10000.0
TOP_K = 2
VMEM_LIMIT_BYTES = 56 * 1024 * 1024
LANES = 128
SUBLANES = 8
HGRN_CHUNK = 64
HGRN_HEADS_PER_STEP = 4
ATTN_BQ = 256
ATTN_BK = 512
ATTN_ROW_GROUPS = 4
SAMPLE_SEQS_PER_STEP = 2
SAMPLE_PAGE_OPERANDS = 16
SAMPLE_PAGE_SLOTS = 3
FFN_FULL_BLOCK_CHUNKS = 2
DENSE_BLOCK_ROWS = 1152
MOE_BLOCK_ROWS = 1280
SOFTMAX_ROWS = 32
SOFTMAX_UNROLL = 8
NEG_INF = float("-inf")


def _params(*sem):
    return pltpu.CompilerParams(dimension_semantics=sem, vmem_limit_bytes=VMEM_LIMIT_BYTES)


def _tile(n, pref, mult=SUBLANES):
    if n <= pref:
        return n
    for t in range(pref, 0, -1):
        if n % t == 0 and t % mult == 0:
            return t
    raise ValueError(f"no tile for {n} <= {pref}")


def _dot(a, b):
    return jnp.dot(a, b, preferred_element_type=F32)


def _dot_nt(a, b):
    return lax.dot_general(a, b, (((1,), (1,)), ((), ())), preferred_element_type=F32)


def _rms(x, w):
    return x * lax.rsqrt(jnp.mean(x * x, axis=-1, keepdims=True) + EPS) * w


def _rope_rows(x, cos, sin):
    half = x.shape[-1] // 2
    x1, x2 = x[:, :half], x[:, half:]
    return jnp.concatenate([x1 * cos - x2 * sin, x1 * sin + x2 * cos], axis=-1)


def _add_norm_kernel(*refs, n_in, want_sum):
    x = refs[0][...]
    for r in refs[1:n_in]:
        x = x + r[...]
    g_ref = refs[n_in]
    outs = refs[n_in + 1:]
    k = 0
    if want_sum:
        outs[0][...] = x
        k = 1
    outs[k][...] = _rms(x, g_ref[...]).astype(outs[k].dtype)


def _add_norm(parts, gain, *, want_sum, out_dtype):
    t, d = parts[0].shape
    tm = _tile(t, 256)
    row = pl.BlockSpec((tm, d), lambda i: (i, 0))
    out_shape = []
    if want_sum:
        out_shape.append(jax.ShapeDtypeStruct((t, d), F32))
    out_shape.append(jax.ShapeDtypeStruct((t, d), out_dtype))
    return pl.pallas_call(
        functools.partial(_add_norm_kernel, n_in=len(parts), want_sum=want_sum),
        grid=(t // tm,),
        in_specs=[row] * len(parts) + [pl.BlockSpec((1, d), lambda i: (0, 0))],
        out_specs=[row] * len(out_shape),
        out_shape=out_shape,
        compiler_params=_params("parallel"),
        name="add_norm",
    )(*parts, gain.reshape(1, d))


def _mm_kernel(x_ref, w_ref, *rest, n_extra, n_out, epilogue):
    extras = rest[:n_extra]
    outs = rest[n_extra:n_extra + n_out]
    w_bf = rest[n_extra + n_out]

    @pl.when(pl.program_id(1) == 0)
    def _():
        w_bf[...] = w_ref[...].astype(BF16)

    acc = _dot(x_ref[...], w_bf[...])
    res = epilogue(acc, *[e[...] for e in extras])
    for o, r in zip(outs, res):
        o[...] = r.astype(o.dtype)


def _mm(x, w, epilogue, outs, extras=(), *, tm_pref=1024, tn_pref=1024, name="mm"):
    t, k = x.shape
    n = w.shape[1]
    tm = _tile(t, tm_pref, 16)
    tn = _tile(n, tn_pref, LANES)
    nj = n // tn
    in_specs = [pl.BlockSpec((tm, k), lambda j, i: (i, 0)),
                pl.BlockSpec((k, tn), lambda j, i: (0, j))]
    args = [x, w]
    for arr, kind in extras:
        if kind == "row":
            in_specs.append(pl.BlockSpec((tm, arr.shape[1]), lambda j, i: (i, 0)))
        elif kind == "col":
            in_specs.append(pl.BlockSpec((1, tn), lambda j, i: (0, j)))
        elif kind == "tile":
            in_specs.append(pl.BlockSpec((tm, tn), lambda j, i: (i, j)))
        else:
            raise ValueError(kind)
        args.append(arr)
    out_specs = [pl.BlockSpec((tm, wd), lambda j, i: (i, j)) for wd, _ in outs]
    out_shape = [jax.ShapeDtypeStruct((t, nj * wd), dt) for wd, dt in outs]
    res = pl.pallas_call(
        functools.partial(_mm_kernel, n_extra=len(extras), n_out=len(outs), epilogue=epilogue),
        grid=(nj, t // tm),
        in_specs=in_specs,
        out_specs=out_specs,
        out_shape=out_shape,
        scratch_shapes=[pltpu.VMEM((k, tn), BF16)],
        compiler_params=_params("arbitrary", "arbitrary"),
        name=name,
    )(*args)
    return res


def _ep_gelu(acc):
    return (jax.nn.gelu(acc),)


def _ep_gelu_ln(acc, w, b):
    g = jax.nn.gelu(acc)
    mu = jnp.mean(g, axis=-1, keepdims=True)
    c = g - mu
    var = jnp.mean(c * c, axis=-1, keepdims=True)
    return (c * lax.rsqrt(var + EPS) * w + b,)


def _ep_rms(acc, w):
    return (_rms(acc, w),)


def _ep_kv(acc, cos, sin, w, *, kv_lora):
    ckv = _rms(acc[:, :kv_lora], w)
    kpe = _rope_rows(acc[:, kv_lora:], cos, sin)
    return ckv, kpe, ckv, kpe


def _ep_add(acc, resid):
    return (acc + resid,)


def _ep_id(acc):
    return (acc,)


def _q_proj_kernel(cq_ref, wqn_ref, wqr_ref, wabs_ref, cos_ref, sin_ref, ql_ref, qp_ref, *, heads, nope, rope, scale):
    cq = cq_ref[...]
    qn = _dot(cq, wqn_ref[...])
    qr = _dot(cq, wqr_ref[...])
    cos = cos_ref[...]
    sin = sin_ref[...]
    for h in range(heads):
        lat = _dot(qn[:, h * nope:(h + 1) * nope].astype(BF16), wabs_ref[h])
        ql_ref[h] = (lat * scale).astype(ql_ref.dtype)
        qp_ref[h] = (_rope_rows(qr[:, h * rope:(h + 1) * rope], cos, sin) * scale).astype(qp_ref.dtype)


def _q_proj(cq, wqn, wqr, wabs, cos, sin, scale):
    t, ql = cq.shape
    heads, nope, kv = wabs.shape
    rope = wqr.shape[1] // heads
    tm = _tile(t, 512, 16)
    const2 = lambda i: (0, 0)
    return pl.pallas_call(
        functools.partial(_q_proj_kernel, heads=heads, nope=nope, rope=rope, scale=scale),
        grid=(t // tm,),
        in_specs=[pl.BlockSpec((tm, ql), lambda i: (i, 0)),
                  pl.BlockSpec(wqn.shape, const2),
                  pl.BlockSpec(wqr.shape, const2),
                  pl.BlockSpec(wabs.shape, lambda i: (0, 0, 0)),
                  pl.BlockSpec((tm, rope // 2), lambda i: (i, 0)),
                  pl.BlockSpec((tm, rope // 2), lambda i: (i, 0))],
        out_specs=[pl.BlockSpec((heads, tm, kv), lambda i: (0, i, 0)),
                   pl.BlockSpec((heads, tm, rope), lambda i: (0, i, 0))],
        out_shape=[jax.ShapeDtypeStruct((heads, t, kv), BF16),
                   jax.ShapeDtypeStruct((heads, t, rope), BF16)],
        compiler_params=_params("parallel"),
        name="q_proj",
    )(cq, wqn, wqr, wabs, cos, sin)


def _gmlp_kernel(u_ref, v_ref, m_ref, b_ref, a_ref, *, groups, gd):
    for g in range(groups):
        sl = slice(g * gd, (g + 1) * gd)
        mixed = _dot(m_ref[0, g], v_ref[:, sl].astype(BF16)) + b_ref[0, :, sl]
        a_ref[:, sl] = (u_ref[:, sl].astype(F32) * mixed).astype(a_ref.dtype)


def _gmlp(u, v, mats, bias, n_first):
    t, a = u.shape
    _, groups, c, _ = mats.shape
    sel = lambda n: (jnp.where(n < n_first, 0, 1), 0, 0)
    return pl.pallas_call(
        functools.partial(_gmlp_kernel, groups=groups, gd=a // groups),
        grid=(t // c,),
        in_specs=[pl.BlockSpec((c, a), lambda n: (n, 0)),
                  pl.BlockSpec((c, a), lambda n: (n, 0)),
                  pl.BlockSpec((1, groups, c, c), lambda n: (jnp.where(n < n_first, 0, 1), 0, 0, 0)),
                  pl.BlockSpec((1, c, a), sel)],
        out_specs=pl.BlockSpec((c, a), lambda n: (n, 0)),
        out_shape=jax.ShapeDtypeStruct((t, a), BF16),
        compiler_params=_params("parallel"),
        name="gmlp",
    )(u, v, mats, bias)


def _softmax_rows(s_ref, p_ref, a_ref, m_ref, l_ref, rc, mask_fn=None, static_rows=None):
    width = s_ref.shape[1]

    def chunk(rs, row0):
        s = s_ref[rs, :]
        if mask_fn is not None:
            s = mask_fn(s, row0)
        m_prev = m_ref[rs, :]
        m_new = jnp.maximum(m_prev, jnp.max(s, axis=-1, keepdims=True))
        alpha = jnp.exp(m_prev - m_new)
        p = jnp.exp(s - jnp.tile(m_new, (1, width // LANES)))
        l_ref[rs, :] = alpha * l_ref[rs, :] + jnp.sum(p, axis=-1, keepdims=True)
        m_ref[rs, :] = m_new
        a_ref[rs, :] = alpha
        p_ref[rs, :] = p.astype(p_ref.dtype)

    if static_rows is not None:
        for row0 in range(static_rows[0], static_rows[1], rc):
            chunk(slice(row0, row0 + rc), row0)
        return

    def body(r, carry):
        chunk(pl.ds(pl.multiple_of(r * rc, rc), rc), r * rc)
        return carry

    n = s_ref.shape[0] // rc
    lax.fori_loop(0, n, body, 0, unroll=math.gcd(n, SOFTMAX_UNROLL))


def _rescale_add(acc_ref, a_ref, pv):
    acc_ref[...] = acc_ref[...] * jnp.tile(a_ref[...], (1, acc_ref.shape[1] // LANES)) + pv


def _attn_prompt_kernel(ql_ref, qp_ref, kc_ref, kp_ref, wv_ref, o_ref, m_ref, l_ref, acc_ref, s_ref, p_ref, a_ref,
                        *, heads, bq, bk, vd):
    i = pl.program_id(0)
    j = pl.program_id(1)
    rows = heads * bq

    @pl.when(j == 0)
    def _():
        m_ref[...] = jnp.full(m_ref.shape, NEG_INF, F32)
        l_ref[...] = jnp.zeros(l_ref.shape, F32)
        acc_ref[...] = jnp.zeros(acc_ref.shape, F32)

    def causal(s, row0):
        q_pos = i * bq + (row0 + lax.broadcasted_iota(jnp.int32, s.shape, 0)) % bq
        k_pos = j * bk + lax.broadcasted_iota(jnp.int32, s.shape, 1)
        return jnp.where(k_pos <= q_pos, s, NEG_INF)

    def step(mask_fn):
        hg = heads // math.gcd(heads, ATTN_ROW_GROUPS)
        gr = hg * bq
        kc = kc_ref[...]
        for g in range(heads // hg):
            q = ql_ref[g * hg:(g + 1) * hg].reshape(gr, ql_ref.shape[-1])
            qp = qp_ref[g * hg:(g + 1) * hg].reshape(gr, qp_ref.shape[-1])
            s_ref[g * gr:(g + 1) * gr, :] = _dot_nt(q, kc) + _dot_nt(qp, kp_ref[...])
        for g in range(heads // hg):
            sl = slice(g * gr, (g + 1) * gr)
            _softmax_rows(s_ref, p_ref, a_ref, m_ref, l_ref, SOFTMAX_ROWS, mask_fn, static_rows=(g * gr, (g + 1) * gr))
            acc_ref[sl, :] = (acc_ref[sl, :] * jnp.tile(a_ref[sl, :], (1, acc_ref.shape[1] // LANES))
                              + _dot(p_ref[sl, :], kc))

    needed = j * bk <= i * bq + bq - 1
    crosses_diagonal = j * bk + bk - 1 > i * bq

    @pl.when(needed & crosses_diagonal)
    def _():
        step(causal)

    @pl.when(needed & jnp.logical_not(crosses_diagonal))
    def _():
        step(None)

    @pl.when(j == pl.num_programs(1) - 1)
    def _():
        o = (acc_ref[...] / l_ref[...][:, :1]).astype(BF16)
        for h in range(heads):
            o_ref[:, h * vd:(h + 1) * vd] = _dot(o[h * bq:(h + 1) * bq], wv_ref[h]).astype(o_ref.dtype)


def _attn_prompt(ql, qp, kc, kp, wv, n_prompt):
    heads, _, kv = ql.shape
    rope = qp.shape[-1]
    vd = wv.shape[-1]
    bq = _tile(n_prompt, ATTN_BQ, SOFTMAX_ROWS)
    bk = _tile(n_prompt, ATTN_BK, LANES)
    nq, nk = n_prompt // bq, n_prompt // bk
    rows = heads * bq

    def kmap(i, j):
        return (jnp.minimum(j, (i * bq + bq - 1) // bk), 0)

    return pl.pallas_call(
        functools.partial(_attn_prompt_kernel, heads=heads, bq=bq, bk=bk, vd=vd),
        grid=(nq, nk),
        in_specs=[pl.BlockSpec((heads, bq, kv), lambda i, j: (0, i, 0)),
                  pl.BlockSpec((heads, bq, rope), lambda i, j: (0, i, 0)),
                  pl.BlockSpec((bk, kv), kmap),
                  pl.BlockSpec((bk, rope), kmap),
                  pl.BlockSpec(wv.shape, lambda i, j: (0, 0, 0))],
        out_specs=pl.BlockSpec((bq, heads * vd), lambda i, j: (i, 0)),
        out_shape=jax.ShapeDtypeStruct((n_prompt, heads * vd), BF16),
        scratch_shapes=[pltpu.VMEM((rows, LANES), F32), pltpu.VMEM((rows, LANES), F32),
                        pltpu.VMEM((rows, kv), F32), pltpu.VMEM((rows, bk), F32),
                        pltpu.VMEM((rows, bk), BF16), pltpu.VMEM((rows, LANES), F32)],
        compiler_params=_params("parallel", "arbitrary"),
        name="attn_prompt",
    )(ql, qp, kc, kp, wv)


def _attn_sample_kernel(pt_ref, ql_ref, qp_ref, kcn_ref, kpn_ref, wv_ref, ckv_hbm, kpe_hbm, o_ref,
                        ckv_buf, kpe_buf, sem, kc_s, kp_s, m_ref, l_ref, acc_ref, s_ref, p_ref, a_ref,
                        *, heads, lq, seqs, pages, page, vd):
    b = pl.program_id(0)
    j = pl.program_id(1)
    steps = pl.num_programs(1)
    g = b * steps + j
    total = pl.num_programs(0) * steps
    n_slots = ckv_buf.shape[0]
    ahead = n_slots - 1
    slot = g % n_slots

    def page_copies(gg, to_slot):
        bb, jj = gg // steps, gg % steps
        copies = []
        for e in range(seqs):
            for r in range(pages):
                pg = pt_ref[bb * seqs + e, jj * pages + r]
                k = e * pages + r
                copies.append(pltpu.make_async_copy(ckv_hbm.at[pg], ckv_buf.at[to_slot, k], sem.at[to_slot]))
                copies.append(pltpu.make_async_copy(kpe_hbm.at[pg], kpe_buf.at[to_slot, k], sem.at[to_slot]))
        return copies

    @pl.when(g == 0)
    def _():
        for first in range(ahead):
            @pl.when(first < total)
            def _():
                for cp in page_copies(first, first):
                    cp.start()

    @pl.when(g + ahead < total)
    def _():
        for cp in page_copies(g + ahead, (g + ahead) % n_slots):
            cp.start()

    for cp in page_copies(g, slot):
        cp.wait()
    ckv_pages = [ckv_buf.at[slot, k] for k in range(seqs * pages)]
    kpe_pages = [kpe_buf.at[slot, k] for k in range(seqs * pages)]

    @pl.when(j == 0)
    def _():
        for e in range(seqs):
            kcn = kcn_ref[e]
            s = _dot_nt(ql_ref[e].astype(F32), kcn) + _dot_nt(qp_ref[e].astype(F32), kpn_ref[e])
            t_q = lax.broadcasted_iota(jnp.int32, s.shape, 0) % lq
            t_k = lax.broadcasted_iota(jnp.int32, s.shape, 1)
            s = jnp.where(t_k <= t_q, s, NEG_INF)
            m0 = jnp.max(s, axis=-1, keepdims=True)
            p = jnp.exp(s - m0)
            m_ref[e] = jnp.broadcast_to(m0, m_ref.shape[1:])
            l_ref[e] = jnp.broadcast_to(jnp.sum(p, axis=-1, keepdims=True), l_ref.shape[1:])
            acc_ref[e] = _dot(p, kcn)

    for e in range(seqs):
        for r in range(pages):
            kc_s[e, r * page:(r + 1) * page, :] = ckv_pages[e * pages + r][...].astype(BF16)
            kp_s[e, :, r * page:(r + 1) * page] = kpe_pages[e * pages + r][...].astype(BF16)
        s_ref[e] = _dot_nt(ql_ref[e], kc_s[e]) + _dot(qp_ref[e], kp_s[e])
    for e in range(seqs):
        _softmax_rows(s_ref.at[e], p_ref.at[e], a_ref.at[e], m_ref.at[e], l_ref.at[e], 2 * SUBLANES,
                      static_rows=(0, s_ref.shape[1]))
        _rescale_add(acc_ref.at[e], a_ref.at[e], _dot(p_ref[e], kc_s[e]))

    @pl.when(j == pl.num_programs(1) - 1)
    def _():
        for e in range(seqs):
            o = (acc_ref[e] / l_ref[e][:, :1]).astype(BF16)
            for h in range(heads):
                full = _dot(o, wv_ref[h])
                o_ref[e, :, h * vd:(h + 1) * vd] = full[h * lq:(h + 1) * lq].astype(o_ref.dtype)


def _attn_sample(page_table, ql, qp, kc_new, kp_new, wv, cache_ckv, cache_kpe_t):
    nb, rows, kv = ql.shape
    rope = qp.shape[-1]
    heads, _, vd = wv.shape
    lq = rows // heads
    n_pages = page_table.shape[1]
    page = cache_ckv.shape[1]
    seqs = math.gcd(nb, SAMPLE_SEQS_PER_STEP)
    pages = _tile(n_pages, SAMPLE_PAGE_OPERANDS // seqs, 1)
    steps = n_pages // pages

    per_seq = lambda *shape: pl.BlockSpec((seqs,) + shape, lambda b, j, pt: (b, 0, 0))
    in_hbm = pl.BlockSpec(memory_space=pl.ANY)
    keys = pages * page
    n_ops = seqs * pages
    grid_spec = pltpu.PrefetchScalarGridSpec(
        num_scalar_prefetch=1,
        grid=(nb // seqs, steps),
        in_specs=[per_seq(rows, kv), per_seq(rows, rope), per_seq(lq, kv), per_seq(lq, rope),
                  pl.BlockSpec(wv.shape, lambda b, j, pt: (0, 0, 0)), in_hbm, in_hbm],
        out_specs=per_seq(lq, heads * vd),
        scratch_shapes=[pltpu.VMEM((SAMPLE_PAGE_SLOTS, n_ops, page, kv), cache_ckv.dtype),
                        pltpu.VMEM((SAMPLE_PAGE_SLOTS, n_ops, rope, page), cache_kpe_t.dtype),
                        pltpu.SemaphoreType.DMA((SAMPLE_PAGE_SLOTS,)),
                        pltpu.VMEM((seqs, keys, kv), BF16), pltpu.VMEM((seqs, rope, keys), BF16),
                        pltpu.VMEM((seqs, rows, LANES), F32), pltpu.VMEM((seqs, rows, LANES), F32),
                        pltpu.VMEM((seqs, rows, kv), F32), pltpu.VMEM((seqs, rows, keys), F32),
                        pltpu.VMEM((seqs, rows, keys), BF16), pltpu.VMEM((seqs, rows, LANES), F32)],
    )
    return pl.pallas_call(
        functools.partial(_attn_sample_kernel, heads=heads, lq=lq, seqs=seqs, pages=pages, page=page, vd=vd),
        grid_spec=grid_spec,
        out_shape=jax.ShapeDtypeStruct((nb, lq, heads * vd), BF16),
        compiler_params=_params("arbitrary", "arbitrary"),
        name="attn_sample",
    )(page_table, ql, qp, kc_new, kp_new, wv, cache_ckv, cache_kpe_t)


def _gather_rows_kernel(tok_ref, bv_ref, x_hbm, o_ref, buf, sem, *, tb, per_blk):
    i = pl.program_id(0)
    n = pl.num_programs(0)
    slot = i % 2

    def live(ii):
        return bv_ref[ii // per_blk] > (ii % per_blk) * tb

    def row_copy(ii, r, to_slot):
        tok = tok_ref[ii * tb + r]
        return pltpu.make_async_copy(x_hbm.at[pl.ds(tok, 1), :], buf.at[to_slot, pl.ds(r, 1), :], sem.at[to_slot])

    def for_rows(fn):
        def body(r, carry):
            fn(r)
            return carry
        lax.fori_loop(0, tb, body, 0, unroll=8)

    @pl.when((i == 0) & live(0))
    def _():
        for_rows(lambda r: row_copy(0, r, 0).start())

    nxt = jnp.minimum(i + 1, n - 1)

    @pl.when((i + 1 < n) & live(nxt))
    def _():
        for_rows(lambda r: row_copy(nxt, r, 1 - slot).start())

    @pl.when(live(i))
    def _():
        for_rows(lambda r: row_copy(i, r, slot).wait())
        o_ref[...] = buf[slot].astype(o_ref.dtype)

    @pl.when(jnp.logical_not(live(i)))
    def _():
        o_ref[...] = jnp.zeros(o_ref.shape, o_ref.dtype)


def _gather_rows(x, tok_of_slot, blk_valid, tm, dtype):
    d = x.shape[1]
    n_slots = tok_of_slot.shape[0]
    tb = _tile(tm, 256, 16)
    grid_spec = pltpu.PrefetchScalarGridSpec(
        num_scalar_prefetch=2,
        grid=(n_slots // tb,),
        in_specs=[pl.BlockSpec(memory_space=pl.ANY)],
        out_specs=pl.BlockSpec((tb, d), lambda i, tok, bv: (i, 0)),
        scratch_shapes=[pltpu.VMEM((2, tb, d), x.dtype), pltpu.SemaphoreType.DMA((2,))],
    )
    return pl.pallas_call(
        functools.partial(_gather_rows_kernel, tb=tb, per_blk=tm // tb),
        grid_spec=grid_spec,
        out_shape=jax.ShapeDtypeStruct((n_slots, d), dtype),
        compiler_params=_params("arbitrary"),
        name="gather_rows",
    )(tok_of_slot, blk_valid, x)


def _ffn_kernel(be_ref, bv_ref, x_ref, wg_ref, wu_ref, wd_ref, gate_ref, o_ref, wg_s, wu_s, wd_s, *, tm, ts):
    i = pl.program_id(0)
    j = pl.program_id(1)
    valid = bv_ref[i]

    @pl.when(j == 0)
    def _():
        o_ref[...] = jnp.zeros(o_ref.shape, F32)

    @pl.when(valid > 0)
    def _():
        wg_s[...] = wg_ref[0].astype(BF16)
        wu_s[...] = wu_ref[0].astype(BF16)
        wd_s[...] = wd_ref[0].astype(BF16)

    def swiglu(rows):
        x = x_ref[rows, :]
        h = (jax.nn.silu(_dot(x, wg_s[...])) * _dot(x, wu_s[...])).astype(BF16)
        o_ref[rows, :] += _dot(h, wd_s[...])

    @pl.when(valid == tm)
    def _():
        chunk = tm // math.gcd(tm // 16, FFN_FULL_BLOCK_CHUNKS)
        for c in range(tm // chunk):
            swiglu(slice(c * chunk, (c + 1) * chunk))

    for sub in range(tm // ts):
        @pl.when((valid > sub * ts) & (valid < tm))
        def _():
            swiglu(slice(sub * ts, (sub + 1) * ts))

    @pl.when(j == pl.num_programs(1) - 1)
    def _():
        o_ref[...] = o_ref[...] * gate_ref[...]


def _ffn(x, wg, wu, wd, gate, blk_expert, blk_valid, *, tm, tf_pref=512, ts_pref=256):
    s, d = x.shape
    f = wg.shape[2]
    tf = _tile(f, tf_pref, LANES)
    ts = _tile(tm, ts_pref, 16)
    nj = f // tf

    def jj(i, j, bv):
        return jnp.where(bv[i] > 0, j, nj - 1)

    once = pl.Buffered(1)
    scratch = [pltpu.VMEM((d, tf), BF16), pltpu.VMEM((d, tf), BF16), pltpu.VMEM((tf, d), BF16)]
    grid_spec = pltpu.PrefetchScalarGridSpec(
        num_scalar_prefetch=2,
        grid=(s // tm, nj),
        in_specs=[pl.BlockSpec((tm, d), lambda i, j, be, bv: (i, 0), pipeline_mode=once),
                  pl.BlockSpec((1, d, tf), lambda i, j, be, bv: (be[i], 0, jj(i, j, bv))),
                  pl.BlockSpec((1, d, tf), lambda i, j, be, bv: (be[i], 0, jj(i, j, bv))),
                  pl.BlockSpec((1, tf, d), lambda i, j, be, bv: (be[i], jj(i, j, bv), 0)),
                  pl.BlockSpec((tm, 1), lambda i, j, be, bv: (i, 0), pipeline_mode=once)],
        out_specs=pl.BlockSpec((tm, d), lambda i, j, be, bv: (i, 0), pipeline_mode=once),
        scratch_shapes=scratch,
    )
    return pl.pallas_call(
        functools.partial(_ffn_kernel, tm=tm, ts=ts),
        grid_spec=grid_spec,
        out_shape=jax.ShapeDtypeStruct((s, d), F32),
        compiler_params=_params("arbitrary", "arbitrary"),
        name="ffn",
    )(blk_expert, blk_valid, x, wg, wu, wd, gate)


def _split3(a):
    a1 = a.astype(BF16)
    r1 = a - a1.astype(F32)
    a2 = r1.astype(BF16)
    a3 = (r1 - a2.astype(F32)).astype(BF16)
    return a1, a2, a3


def _router_kernel(h_ref, d_ref, g_ref, wr_ref, hs_ref, hn_ref, comb_ref, sel_ref):
    x = h_ref[...] + d_ref[...]
    hs_ref[...] = x
    y = _rms(x, g_ref[...])
    hn_ref[...] = y
    ys = _split3(y)
    ws = _split3(wr_ref[...])
    logits = jnp.zeros((x.shape[0], wr_ref.shape[1]), F32)
    for a in range(3):
        for b in range(3 - a):
            logits = logits + _dot(ys[a], ws[b])
    n_e = logits.shape[1]
    lane = lax.broadcasted_iota(jnp.int32, logits.shape, 1).astype(F32)
    m1 = jnp.max(logits, axis=-1, keepdims=True)
    i1 = jnp.min(jnp.where(logits == m1, lane, n_e), axis=-1, keepdims=True)
    first = lane == i1
    rest = jnp.where(first, NEG_INF, logits)
    m2 = jnp.max(rest, axis=-1, keepdims=True)
    i2 = jnp.min(jnp.where(rest == m2, lane, n_e), axis=-1, keepdims=True)
    second = lane == i2
    e = jnp.exp(m2 - m1)
    den = 1.0 + e
    comb_ref[...] = jnp.where(first, 1.0 / den, 0.0) + jnp.where(second, e / den, 0.0)
    sel_ref[...] = jnp.where(first | second, 1.0, 0.0)


def _router(h, delta, gain, w_router):
    t, d = h.shape
    n_e = w_router.shape[1]
    tm = _tile(t, 256)
    row = pl.BlockSpec((tm, d), lambda i: (i, 0))
    small = pl.BlockSpec((tm, n_e), lambda i: (i, 0))
    return pl.pallas_call(
        _router_kernel,
        grid=(t // tm,),
        in_specs=[row, row, pl.BlockSpec((1, d), lambda i: (0, 0)), pl.BlockSpec((d, n_e), lambda i: (0, 0))],
        out_specs=[row, row, small, small],
        out_shape=[jax.ShapeDtypeStruct((t, d), F32), jax.ShapeDtypeStruct((t, d), F32),
                   jax.ShapeDtypeStruct((t, n_e), F32), jax.ShapeDtypeStruct((t, n_e), F32)],
        compiler_params=_params("parallel"),
        name="router",
    )(h, delta, gain.reshape(1, d), w_router)


def _roll_rows(x, shift):
    n = x.shape[0]
    shift = shift % n
    return x if shift == 0 else pltpu.roll(x, shift, 0)


def _hgrn_inputs(zq, zf, lb):
    q = jax.nn.silu(zq)
    lf = jnp.log(lb + (1.0 - lb) * jax.nn.sigmoid(zf))
    k = (1.0 - lb) * jax.nn.sigmoid(-zf)
    return q, k, lf


def _prefix8(lf, r8):
    p = lf
    for sh in (1, 2, 4):
        p = p + jnp.where(r8 >= sh, _roll_rows(p, sh), 0.0)
    return p


def _block8_intra(q, k, v, p8, r8):
    o = jnp.sum(q * k, axis=-1, keepdims=True) * v
    for d in range(1, SUBLANES):
        e = jnp.exp(jnp.where(r8 >= d, p8 - _roll_rows(p8, d), NEG_INF))
        w = jnp.sum(q * _roll_rows(k, d) * e, axis=-1, keepdims=True)
        o = o + w * _roll_rows(v, d)
    return o


def _hgrn_finish(o, zg, gw):
    return _rms(o, gw) * jax.nn.silu(zg)


def _hgrn_prompt_kernel(zq_ref, zf_ref, zi_ref, zg_ref, lb_ref, gw_ref, o_ref, st_ref, st_t, *, c, n_chunks, hp, kd):
    @pl.when(pl.program_id(1) == 0)
    def _():
        st_t[...] = jnp.zeros(st_t.shape, F32)

    gw = gw_ref[...]
    row = lax.broadcasted_iota(jnp.int32, (c, kd), 0)
    r8 = row % SUBLANES
    ri = lax.broadcasted_iota(jnp.int32, (c, c), 0)
    ci = lax.broadcasted_iota(jnp.int32, (c, c), 1)

    def one_head(rows, hh):
        cols = slice(hh * kd, (hh + 1) * kd)
        q, k, lf = _hgrn_inputs(zq_ref[rows, cols], zf_ref[rows, cols], lb_ref[:, cols])
        v = zi_ref[rows, cols]
        p = _prefix8(lf, r8)
        o = _block8_intra(q, k, v, p, r8)
        tot = jnp.where(r8 == SUBLANES - 1, p, 0.0)
        for sh in (1, 2, 4):
            tot = tot + _roll_rows(tot, -sh)
        a = jnp.zeros((c, c), F32)
        s = SUBLANES
        while s < c:
            second = row % (2 * s) >= s
            qs = (q * jnp.exp(jnp.where(second, p, NEG_INF))).astype(BF16)
            ks = (k * jnp.exp(jnp.where(second, NEG_INF, tot - p))).astype(BF16)
            a_s = _dot_nt(qs, ks)
            a = a + (a_s if 2 * s == c else jnp.where(ri // (2 * s) == ci // (2 * s), a_s, 0.0))
            prev = _roll_rows(tot, s)
            p = p + jnp.where(second, prev, 0.0)
            tot = tot + jnp.where(second, prev, _roll_rows(tot, -s))
            s *= 2
        st = st_t[hh]
        o = o + _dot(a.astype(BF16), v.astype(BF16)) + _dot_nt((q * jnp.exp(p)).astype(BF16), st.astype(BF16))
        kb = (k * jnp.exp(tot - p)).astype(BF16)
        st_t[hh] = st * jnp.exp(tot[0:1, :]) + _dot(v.T.astype(BF16), kb)
        o_ref[rows, cols] = _hgrn_finish(o, zg_ref[rows, cols], gw).astype(o_ref.dtype)

    def chunk(n, carry):
        rows = pl.ds(pl.multiple_of(n * c, c), c)
        for hh in range(hp):
            one_head(rows, hh)
        return carry

    lax.fori_loop(0, n_chunks, chunk, 0)

    @pl.when(pl.program_id(1) == pl.num_programs(1) - 1)
    def _():
        for hh in range(hp):
            st_ref[hh] = st_t[hh].T


def _hgrn_prompt(z, lb, gw, n_prompt, heads):
    kd = lb.shape[1] // heads
    c = HGRN_CHUNK
    hp = math.gcd(heads, HGRN_HEADS_PER_STEP)
    lbk = _tile(n_prompt, 1024, c)
    groups = heads // hp

    def zspec(seg):
        return pl.BlockSpec((lbk, hp * kd), lambda h, l: (l, seg * groups + h))

    return pl.pallas_call(
        functools.partial(_hgrn_prompt_kernel, c=c, n_chunks=lbk // c, hp=hp, kd=kd),
        grid=(groups, n_prompt // lbk),
        in_specs=[zspec(0), zspec(1), zspec(2), zspec(3),
                  pl.BlockSpec((1, hp * kd), lambda h, l: (0, h)),
                  pl.BlockSpec((1, kd), lambda h, l: (0, 0))],
        out_specs=[pl.BlockSpec((lbk, hp * kd), lambda h, l: (l, h)),
                   pl.BlockSpec((hp, kd, kd), lambda h, l: (h, 0, 0))],
        out_shape=[jax.ShapeDtypeStruct((n_prompt, heads * kd), BF16),
                   jax.ShapeDtypeStruct((heads, kd, kd), F32)],
        scratch_shapes=[pltpu.VMEM((hp, kd, kd), F32)],
        compiler_params=_params("parallel", "arbitrary"),
        name="hgrn_prompt",
    )(z, z, z, z, lb, gw)


def _hgrn_sample_kernel(z_ref, s0_ref, lb_ref, gw_ref, o_ref, s1_ref, *, heads, kd):
    lq = z_ref.shape[0]
    gw = gw_ref[...]
    r8 = lax.broadcasted_iota(jnp.int32, (lq, kd), 0)
    for h in range(heads):
        col = lambda seg: slice((seg * heads + h) * kd, (seg * heads + h + 1) * kd)
        q, k, lf = _hgrn_inputs(z_ref[:, col(0)], z_ref[:, col(1)], lb_ref[:, h * kd:(h + 1) * kd])
        v = z_ref[:, col(2)]
        p = _prefix8(lf, r8)
        last = p[lq - 1:lq, :]
        st = s0_ref[0, h]
        o = _block8_intra(q, k, v, p, r8) + _dot((q * jnp.exp(p)).astype(BF16), st.astype(BF16))
        cols = jnp.concatenate([k * jnp.exp(last - p), p], axis=0).T
        v_pad = jnp.concatenate([v, jnp.zeros_like(v)], axis=0)
        s1_ref[0, h] = st * jnp.exp(cols[:, 2 * lq - 1:2 * lq]) + _dot(cols.astype(BF16), v_pad.astype(BF16))
        o_ref[:, h * kd:(h + 1) * kd] = _hgrn_finish(o, z_ref[:, col(3)], gw).astype(o_ref.dtype)


def _hgrn_sample(z, state, lb, gw, row0):
    nb, heads, kd, _ = state.shape
    lq = SUBLANES
    blk0 = row0 // lq
    return pl.pallas_call(
        functools.partial(_hgrn_sample_kernel, heads=heads, kd=kd),
        grid=(nb,),
        in_specs=[pl.BlockSpec((lq, z.shape[1]), lambda b: (blk0 + b, 0)),
                  pl.BlockSpec((1, heads, kd, kd), lambda b: (b, 0, 0, 0)),
                  pl.BlockSpec((1, heads * kd), lambda b: (0, 0)),
                  pl.BlockSpec((1, kd), lambda b: (0, 0))],
        out_specs=[pl.BlockSpec((lq, heads * kd), lambda b: (b, 0)),
                   pl.BlockSpec((1, heads, kd, kd), lambda b: (b, 0, 0, 0))],
        out_shape=[jax.ShapeDtypeStruct((nb * lq, heads * kd), F32),
                   jax.ShapeDtypeStruct(state.shape, F32)],
        compiler_params=_params("parallel"),
        name="hgrn_sample",
    )(z, state, lb, gw)


def _route_plan(sel, comb, tm):
    t, n_e = sel.shape
    n_blk = -(-(TOP_K * t) // tm) + n_e
    n_slot = n_blk * tm
    picked = sel > 0
    seli = picked.astype(jnp.int32)
    incl = jnp.cumsum(seli, axis=0)
    counts = incl[-1]
    blocks_per = (counts + tm - 1) // tm
    blk_end = jnp.cumsum(blocks_per)
    start = (blk_end - blocks_per) * tm
    slot = start[None, :] + incl - seli
    slot_a = jnp.min(jnp.where(picked, slot, n_slot), axis=1)
    slot_b = jnp.max(jnp.where(picked, slot, -1), axis=1)
    tok_slots = jnp.stack([slot_a, slot_b], axis=1)
    gates = jnp.stack([jnp.sum(jnp.where(picked & (slot == s[:, None]), comb, 0.0), axis=1)
                       for s in (slot_a, slot_b)], axis=1).reshape(-1)
    pair = jnp.arange(TOP_K * t, dtype=jnp.int32)
    pair_of_slot = jnp.full((n_slot,), -1, jnp.int32).at[tok_slots.reshape(-1)].set(pair, unique_indices=True)
    filled = pair_of_slot >= 0
    safe = jnp.maximum(pair_of_slot, 0)
    tok_of_slot = safe // TOP_K
    gate_of_slot = jnp.where(filled, gates.at[safe].get(mode="promise_in_bounds"), 0.0)
    blk = jnp.arange(n_blk, dtype=jnp.int32)
    used = blk_end[-1]
    e_of_blk = jnp.sum((jnp.minimum(blk, used - 1)[:, None] >= blk_end[None, :]).astype(jnp.int32), axis=1)
    e_of_blk = jnp.minimum(e_of_blk, n_e - 1)
    valid = jnp.clip(counts[e_of_blk] - (blk * tm - start[e_of_blk]), 0, tm)
    valid = jnp.where(blk < used, valid, 0).astype(jnp.int32)
    return tok_of_slot, gate_of_slot, e_of_blk, valid, tok_slots


def kernel(x_prompt, x_sample, cache_ckv, cache_kpe, state_hgrn, page_table, norm_mix0, w_in_ab, ln_v_w, ln_v_b, w_s, b_s, q_norm_w, w_q_b, kv_norm_w, w_kv_b, w_out_ab, norm_ffn0, w_ffn_gate, w_ffn_up, w_ffn_down, norm_mix1, w_in_c, lb_logits, g_norm_w, w_out_c, norm_ffn1, w_router, w_exp_gate, w_exp_up, w_exp_down, norm_final):
    bp, lp, d = x_prompt.shape
    nb, lq, _ = x_sample.shape
    assert bp == 1, "the prompt group is one sequence"
    n_p, n_s = bp * lp, nb * lq
    t = n_p + n_s
    groups, chunk, _ = w_s.shape
    a_w = ln_v_w.shape[0]
    q_lora, heads, qk = w_q_b.shape
    kv_lora = kv_norm_w.shape[0]
    rope = cache_kpe.shape[-1]
    nope = qk - rope
    vd = w_kv_b.shape[-1] - nope
    page = cache_ckv.shape[1]
    past = page_table.shape[1] * page
    c_heads, c_k = state_hgrn.shape[1], state_hgrn.shape[2]
    c_f = c_heads * c_k
    assert lq == SUBLANES and chunk % lq == 0 and n_p % chunk == 0 and n_s % chunk == 0
    off_q, off_kv = 2 * a_w, 2 * a_w + q_lora
    scale = float(qk) ** -0.5

    x = jnp.concatenate([x_prompt.reshape(n_p, d), x_sample.reshape(n_s, d)], axis=0)

    pos = jnp.concatenate([jnp.arange(lp, dtype=F32), jnp.tile(past + jnp.arange(lq, dtype=F32), nb)])
    inv_freq = ROPE_THETA ** (-jnp.arange(rope // 2, dtype=F32) / (rope // 2))
    ang = pos[:, None] * inv_freq[None, :]
    cos, sin = jnp.cos(ang), jnp.sin(ang)

    hn = _add_norm([x], norm_mix0, want_sum=False, out_dtype=BF16)[0]
    (u,) = _mm(hn, w_in_ab[:, :a_w], _ep_gelu, [(a_w, BF16)], name="in_u")
    (v,) = _mm(hn, w_in_ab[:, a_w:off_q], _ep_gelu_ln, [(a_w, F32)],
               [(ln_v_w.reshape(1, a_w), "col"), (ln_v_b.reshape(1, a_w), "col")], name="in_v")
    (cq,) = _mm(hn, w_in_ab[:, off_q:off_kv], _ep_rms, [(q_lora, BF16)],
                [(q_norm_w.reshape(1, q_lora), "col")], name="in_cq")
    kvw = kv_lora + rope
    w_kvpe = w_in_ab[:, off_kv:]
    t_kv = _tile(t, 512, 16)
    ckv, kpe, ckv_b, kpe_b = pl.pallas_call(
        functools.partial(_mm_kernel, n_extra=3, n_out=4, epilogue=functools.partial(_ep_kv, kv_lora=kv_lora)),
        grid=(1, t // t_kv),
        in_specs=[pl.BlockSpec((t_kv, d), lambda j, i: (i, 0)),
                  pl.BlockSpec((d, kvw), lambda j, i: (0, 0)),
                  pl.BlockSpec((t_kv, rope // 2), lambda j, i: (i, 0)),
                  pl.BlockSpec((t_kv, rope // 2), lambda j, i: (i, 0)),
                  pl.BlockSpec((1, kv_lora), lambda j, i: (0, 0))],
        out_specs=[pl.BlockSpec((t_kv, kv_lora), lambda j, i: (i, 0)),
                   pl.BlockSpec((t_kv, rope), lambda j, i: (i, 0)),
                   pl.BlockSpec((t_kv, kv_lora), lambda j, i: (i, 0)),
                   pl.BlockSpec((t_kv, rope), lambda j, i: (i, 0))],
        out_shape=[jax.ShapeDtypeStruct((t, kv_lora), F32), jax.ShapeDtypeStruct((t, rope), F32),
                   jax.ShapeDtypeStruct((t, kv_lora), BF16), jax.ShapeDtypeStruct((t, rope), BF16)],
        scratch_shapes=[pltpu.VMEM((d, kvw), BF16)],
        compiler_params=_params("arbitrary", "arbitrary"),
        name="in_kv",
    )(hn, w_kvpe, cos, sin, kv_norm_w.reshape(1, kv_lora))

    wqn = w_q_b[:, :, :nope].reshape(q_lora, heads * nope).astype(BF16)
    wqr = w_q_b[:, :, nope:].reshape(q_lora, heads * rope).astype(BF16)
    wabs = jnp.transpose(w_kv_b[:, :, :nope], (1, 2, 0)).astype(BF16)
    wv = jnp.transpose(w_kv_b[:, :, nope:], (1, 0, 2)).astype(BF16)
    ql, qp = _q_proj(cq, wqn, wqr, wabs, cos, sin, scale)

    o_prompt = _attn_prompt(ql, qp, ckv_b, kpe_b, wv, n_p)
    to_rows = lambda a: jnp.transpose(a[:, n_p:].reshape(heads, nb, lq, a.shape[-1]), (1, 0, 2, 3)).reshape(
        nb, heads * lq, a.shape[-1])
    o_sample = _attn_sample(page_table, to_rows(ql), to_rows(qp),
                            ckv[n_p:].reshape(nb, lq, kv_lora), kpe[n_p:].reshape(nb, lq, rope),
                            wv, cache_ckv, jnp.swapaxes(cache_kpe, 1, 2))

    tril = jnp.tril(jnp.ones((chunk, chunk), dtype=bool))
    m_prompt = jnp.where(tril[None], w_s, 0.0)
    eye = jnp.eye(chunk // lq, dtype=F32)
    m_sample = jnp.einsum("ab,gts->gatbs", eye, m_prompt[:, :lq, :lq]).reshape(groups, chunk, chunk)
    mats = jnp.stack([m_prompt, m_sample]).astype(BF16)
    bias_p = jnp.repeat(b_s.T, a_w // groups, axis=1)
    bias = jnp.stack([bias_p, jnp.tile(bias_p[:lq], (chunk // lq, 1))])
    a_mix = _gmlp(u, v, mats, bias, n_p // chunk)

    mix = jnp.concatenate([a_mix, jnp.concatenate([o_prompt, o_sample.reshape(n_s, heads * vd)], axis=0)], axis=1)
    (h1,) = _mm(mix, w_out_ab, _ep_add, [(_tile(d, 1024, LANES), F32)], [(x, "tile")], name="out_ab")

    hn = _add_norm([h1], norm_ffn0, want_sum=False, out_dtype=BF16)[0]
    tm_f = DENSE_BLOCK_ROWS if t % DENSE_BLOCK_ROWS == 0 else _tile(t, 1024, 16)
    nblk = t // tm_f
    y_ffn = _ffn(hn, w_ffn_gate[None], w_ffn_up[None], w_ffn_down[None], jnp.ones((t, 1), F32),
                 jnp.zeros((nblk,), jnp.int32), jnp.full((nblk,), tm_f, jnp.int32), tm=tm_f)

    h2, hn = _add_norm([h1, y_ffn], norm_mix1, want_sum=True, out_dtype=BF16)
    (z,) = _mm(hn, w_in_c, _ep_id, [(_tile(w_in_c.shape[1], 1024, LANES), F32)], name="in_c")
    lb_cum = jnp.cumsum(jax.nn.softmax(lb_logits.astype(F32), axis=0), axis=0)
    lb = (lb_cum[1] - lb_cum[0]).reshape(1, c_f)
    gw = g_norm_w.reshape(1, -1)
    o_p, state_prompt = _hgrn_prompt(z, lb, gw, n_p, c_heads)
    o_s, state_sample = _hgrn_sample(z, state_hgrn, lb, gw, n_p)
    o_c = jnp.concatenate([o_p, o_s.astype(BF16)], axis=0)
    (d3,) = _mm(o_c, w_out_c, _ep_id, [(_tile(d, 1024, LANES), F32)], name="out_c")

    h3, hn_f32, comb, sel = _router(h2, d3, norm_ffn1, w_router)
    tm_e = MOE_BLOCK_ROWS if TOP_K * t >= 4 * MOE_BLOCK_ROWS else _tile(t, 1024, 16)
    tok_of_slot, gate_of_slot, e_of_blk, valid, tok_slots = _route_plan(sel, comb, tm_e)
    rows_of = lambda a, idx: a.at[idx].get(mode="promise_in_bounds")
    x_sorted = _gather_rows(hn_f32, tok_of_slot, valid, tm_e, BF16)
    y_sorted = _ffn(x_sorted, w_exp_gate, w_exp_up, w_exp_down,
                    gate_of_slot[:, None], e_of_blk, valid, tm=tm_e)
    parts = [h3] + [rows_of(y_sorted, tok_slots[:, r]) for r in range(TOP_K)]
    y = _add_norm(parts, norm_final, want_sum=False, out_dtype=F32)[0]

    y_prompt = y[:n_p].reshape(bp, lp, d)
    y_sample = y[n_p:].reshape(nb, lq, d)
    return (y_prompt, y_sample,
            ckv[:n_p].reshape(bp, lp, kv_lora), kpe[:n_p].reshape(bp, lp, rope),
            ckv[n_p:].reshape(nb, lq, kv_lora), kpe[n_p:].reshape(nb, lq, rope),
            v[n_p:].reshape(nb, lq, a_w),
            state_prompt.reshape(bp, c_heads, c_k, -1), state_sample)
```

```python
import functools
import math

import jax
import jax.numpy as jnp
from jax import lax
from jax.experimental import pallas as pl
from jax.experimental.pallas import tpu as pltpu

F32 = jnp.float32
BF16 = jnp.bfloat16
EPS = 1e-6
ROPE_THETA = 10000.0
TOP_K = 2
VMEM_LIMIT_BYTES = 56 * 1024 * 1024
LANES = 128
SUBLANES = 8
HGRN_CHUNK = 64
HGRN_HEADS_PER_STEP = 4
ATTN_BQ = 512
ATTN_BK = 512
ATTN_ROW_GROUPS = 4
SAMPLE_SEQS_PER_STEP = 2
SAMPLE_PAGE_OPERANDS = 16
SAMPLE_PAGE_SLOTS = 3
FFN_FULL_BLOCK_CHUNKS = 2
DENSE_BLOCK_ROWS = 1152
MOE_BLOCK_ROWS = 1280
SOFTMAX_ROWS = 32
SOFTMAX_UNROLL = 8
NEG_INF = float("-inf")


def _params(*sem):
    return pltpu.CompilerParams(dimension_semantics=sem, vmem_limit_bytes=VMEM_LIMIT_BYTES)


def _tile(n, pref, mult=SUBLANES):
    if n <= pref:
        return n
    for t in range(pref, 0, -1):
        if n % t == 0 and t % mult == 0:
            return t
    raise ValueError(f"no tile for {n} <= {pref}")


def _dot(a, b):
    return jnp.dot(a, b, preferred_element_type=F32)


def _dot_nt(a, b):
    return lax.dot_general(a, b, (((1,), (1,)), ((), ())), preferred_element_type=F32)


def _rms(x, w):
    return x * lax.rsqrt(jnp.mean(x * x, axis=-1, keepdims=True) + EPS) * w


def _rope_rows(x, cos, sin):
    half = x.shape[-1] // 2
    x1, x2 = x[:, :half], x[:, half:]
    return jnp.concatenate([x1 * cos - x2 * sin, x1 * sin + x2 * cos], axis=-1)


def _add_norm_kernel(*refs, n_in, want_sum, first_blocks):
    x = refs[0][...]
    for r in refs[1:n_in]:
        x = x + r[...]
    g_ref = refs[n_in]
    outs = refs[n_in + 1:]
    k = 0
    if want_sum:
        outs[0][...] = x
        k = 1
    y = _rms(x, g_ref[...]).astype(outs[k].dtype)
    if first_blocks is None:
        outs[k][...] = y
    else:
        @pl.when(pl.program_id(0) < first_blocks)
        def _():
            outs[k][...] = y

        @pl.when(pl.program_id(0) >= first_blocks)
        def _():
            outs[k + 1][...] = y


def _add_norm(parts, gain, *, want_sum, out_dtype, split_rows=None):
    t, d = parts[0].shape
    tm = _tile(t if split_rows is None else math.gcd(split_rows, t - split_rows), 256)
    row = pl.BlockSpec((tm, d), lambda i: (i, 0))
    out_shape, out_specs = [], []
    if want_sum:
        out_shape.append(jax.ShapeDtypeStruct((t, d), F32))
        out_specs.append(row)
    first_blocks = None
    if split_rows is None:
        out_shape.append(jax.ShapeDtypeStruct((t, d), out_dtype))
        out_specs.append(row)
    else:
        assert split_rows % tm == 0 and 0 < split_rows < t
        first_blocks = split_rows // tm
        out_shape += [jax.ShapeDtypeStruct((split_rows, d), out_dtype),
                      jax.ShapeDtypeStruct((t - split_rows, d), out_dtype)]
        out_specs += [pl.BlockSpec((tm, d), lambda i: (jnp.minimum(i, first_blocks - 1), 0)),
                      pl.BlockSpec((tm, d), lambda i: (jnp.maximum(i - first_blocks, 0), 0))]
    return pl.pallas_call(
        functools.partial(_add_norm_kernel, n_in=len(parts), want_sum=want_sum, first_blocks=first_blocks),
        grid=(t // tm,),
        in_specs=[row] * len(parts) + [pl.BlockSpec((1, d), lambda i: (0, 0))],
        out_specs=out_specs,
        out_shape=out_shape,
        compiler_params=_params("arbitrary" if split_rows is not None else "parallel"),
        name="add_norm",
    )(*parts, gain.reshape(1, d))


def _mm_kernel(x_ref, w_ref, *rest, n_extra, n_out, epilogue):
    extras = rest[:n_extra]
    outs = rest[n_extra:n_extra + n_out]
    w_bf = rest[n_extra + n_out]

    @pl.when(pl.program_id(1) == 0)
    def _():
        w_bf[...] = w_ref[...].astype(BF16)

    acc = _dot(x_ref[...], w_bf[...])
    res = epilogue(acc, *[e[...] for e in extras])
    for o, r in zip(outs, res):
        o[...] = r.astype(o.dtype)


def _mm(x, w, epilogue, outs, extras=(), *, tm_pref=1024, tn_pref=1024, name="mm"):
    t, k = x.shape
    n = w.shape[1]
    tm = _tile(t, tm_pref, 16)
    tn = _tile(n, tn_pref, LANES)
    nj = n // tn
    in_specs = [pl.BlockSpec((tm, k), lambda j, i: (i, 0)),
                pl.BlockSpec((k, tn), lambda j, i: (0, j))]
    args = [x, w]
    for arr, kind in extras:
        if kind == "row":
            in_specs.append(pl.BlockSpec((tm, arr.shape[1]), lambda j, i: (i, 0)))
        elif kind == "col":
            in_specs.append(pl.BlockSpec((1, tn), lambda j, i: (0, j)))
        elif kind == "tile":
            in_specs.append(pl.BlockSpec((tm, tn), lambda j, i: (i, j)))
        else:
            raise ValueError(kind)
        args.append(arr)
    out_specs = [pl.BlockSpec((tm, wd), lambda j, i: (i, j)) for wd, _ in outs]
    out_shape = [jax.ShapeDtypeStruct((t, nj * wd), dt) for wd, dt in outs]
    res = pl.pallas_call(
        functools.partial(_mm_kernel, n_extra=len(extras), n_out=len(outs), epilogue=epilogue),
        grid=(nj, t // tm),
        in_specs=in_specs,
        out_specs=out_specs,
        out_shape=out_shape,
        scratch_shapes=[pltpu.VMEM((k, tn), BF16)],
        compiler_params=_params("arbitrary", "arbitrary"),
        name=name,
    )(*args)
    return res


def _ep_gelu(acc):
    return (jax.nn.gelu(acc),)


def _ep_gelu_ln(acc, w, b):
    g = jax.nn.gelu(acc)
    mu = jnp.mean(g, axis=-1, keepdims=True)
    c = g - mu
    var = jnp.mean(c * c, axis=-1, keepdims=True)
    return (c * lax.rsqrt(var + EPS) * w + b,)


def _ep_rms(acc, w):
    return (_rms(acc, w),)


def _ep_kv(acc, cos, sin, w, *, kv_lora):
    ckv = _rms(acc[:, :kv_lora], w)
    kpe = _rope_rows(acc[:, kv_lora:], cos, sin)
    return ckv, kpe, ckv, kpe


def _ep_add(acc, resid):
    return (acc + resid,)


def _ep_id(acc):
    return (acc,)


def _q_proj_kernel(cq_ref, wqn_ref, wqr_ref, wabs_ref, cos_ref, sin_ref, ql_ref, qp_ref, *, heads, nope, rope, scale):
    cq = cq_ref[...]
    qn = _dot(cq, wqn_ref[...])
    qr = _dot(cq, wqr_ref[...])
    cos = cos_ref[...]
    sin = sin_ref[...]
    for h in range(heads):
        lat = _dot(qn[:, h * nope:(h + 1) * nope].astype(BF16), wabs_ref[h])
        ql_ref[h] = (lat * scale).astype(ql_ref.dtype)
        qp_ref[h] = (_rope_rows(qr[:, h * rope:(h + 1) * rope], cos, sin) * scale).astype(qp_ref.dtype)


def _q_proj(cq, wqn, wqr, wabs, cos, sin, scale):
    t, ql = cq.shape
    heads, nope, kv = wabs.shape
    rope = wqr.shape[1] // heads
    tm = _tile(t, 512, 16)
    const2 = lambda i: (0, 0)
    return pl.pallas_call(
        functools.partial(_q_proj_kernel, heads=heads, nope=nope, rope=rope, scale=scale),
        grid=(t // tm,),
        in_specs=[pl.BlockSpec((tm, ql), lambda i: (i, 0)),
                  pl.BlockSpec(wqn.shape, const2),
                  pl.BlockSpec(wqr.shape, const2),
                  pl.BlockSpec(wabs.shape, lambda i: (0, 0, 0)),
                  pl.BlockSpec((tm, rope // 2), lambda i: (i, 0)),
                  pl.BlockSpec((tm, rope // 2), lambda i: (i, 0))],
        out_specs=[pl.BlockSpec((heads, tm, kv), lambda i: (0, i, 0)),
                   pl.BlockSpec((heads, tm, rope), lambda i: (0, i, 0))],
        out_shape=[jax.ShapeDtypeStruct((heads, t, kv), BF16),
                   jax.ShapeDtypeStruct((heads, t, rope), BF16)],
        compiler_params=_params("parallel"),
        name="q_proj",
    )(cq, wqn, wqr, wabs, cos, sin)


def _gmlp_kernel(u_ref, v_ref, m_ref, b_ref, a_ref, *, groups, gd):
    for g in range(groups):
        sl = slice(g * gd, (g + 1) * gd)
        mixed = _dot(m_ref[0, g], v_ref[:, sl].astype(BF16)) + b_ref[0, :, sl]
        a_ref[:, sl] = (u_ref[:, sl].astype(F32) * mixed).astype(a_ref.dtype)


def _gmlp(u, v, mats, bias, n_first):
    t, a = u.shape
    _, groups, c, _ = mats.shape
    sel = lambda n: (jnp.where(n < n_first, 0, 1), 0, 0)
    return pl.pallas_call(
        functools.partial(_gmlp_kernel, groups=groups, gd=a // groups),
        grid=(t // c,),
        in_specs=[pl.BlockSpec((c, a), lambda n: (n, 0)),
                  pl.BlockSpec((c, a), lambda n: (n, 0)),
                  pl.BlockSpec((1, groups, c, c), lambda n: (jnp.where(n < n_first, 0, 1), 0, 0, 0)),
                  pl.BlockSpec((1, c, a), sel)],
        out_specs=pl.BlockSpec((c, a), lambda n: (n, 0)),
        out_shape=jax.ShapeDtypeStruct((t, a), BF16),
        compiler_params=_params("parallel"),
        name="gmlp",
    )(u, v, mats, bias)


def _softmax_rows(s_ref, p_ref, a_ref, m_ref, l_ref, rc, mask_fn=None, static_rows=None):
    width = s_ref.shape[1]

    def chunk(rs, row0):
        s = s_ref[rs, :]
        if mask_fn is not None:
            s = mask_fn(s, row0)
        m_prev = m_ref[rs, :]
        m_new = jnp.maximum(m_prev, jnp.max(s, axis=-1, keepdims=True))
        alpha = jnp.exp(m_prev - m_new)
        p = jnp.exp(s - jnp.tile(m_new, (1, width // LANES)))
        l_ref[rs, :] = alpha * l_ref[rs, :] + jnp.sum(p, axis=-1, keepdims=True)
        m_ref[rs, :] = m_new
        a_ref[rs, :] = alpha
        p_ref[rs, :] = p.astype(p_ref.dtype)

    if static_rows is not None:
        for row0 in range(static_rows[0], static_rows[1], rc):
            chunk(slice(row0, row0 + rc), row0)
        return

    def body(r, carry):
        chunk(pl.ds(pl.multiple_of(r * rc, rc), rc), r * rc)
        return carry

    n = s_ref.shape[0] // rc
    lax.fori_loop(0, n, body, 0, unroll=math.gcd(n, SOFTMAX_UNROLL))


def _rescale_add(acc_ref, a_ref, pv):
    acc_ref[...] = acc_ref[...] * jnp.tile(a_ref[...], (1, acc_ref.shape[1] // LANES)) + pv


def _attn_prompt_kernel(ql_ref, qp_ref, kc_ref, kp_ref, wv_ref, o_ref, m_ref, l_ref, acc_ref, s_ref, p_ref, a_ref,
                        *, heads, bq, bk, vd):
    i = pl.program_id(0)
    j = pl.program_id(1)
    rows = heads * bq

    @pl.when(j == 0)
    def _():
        m_ref[...] = jnp.full(m_ref.shape, NEG_INF, F32)
        l_ref[...] = jnp.zeros(l_ref.shape, F32)
        acc_ref[...] = jnp.zeros(acc_ref.shape, F32)

    def causal(s, row0):
        q_pos = i * bq + (row0 + lax.broadcasted_iota(jnp.int32, s.shape, 0)) % bq
        k_pos = j * bk + lax.broadcasted_iota(jnp.int32, s.shape, 1)
        return jnp.where(k_pos <= q_pos, s, NEG_INF)

    def step(mask_fn):
        hg = heads // math.gcd(heads, ATTN_ROW_GROUPS)
        gr = hg * bq
        kc = kc_ref[...]
        for g in range(heads // hg):
            q = ql_ref[g * hg:(g + 1) * hg].reshape(gr, ql_ref.shape[-1])
            qp = qp_ref[g * hg:(g + 1) * hg].reshape(gr, qp_ref.shape[-1])
            s_ref[g * gr:(g + 1) * gr, :] = _dot_nt(q, kc) + _dot_nt(qp, kp_ref[...])
        for g in range(heads // hg):
            sl = slice(g * gr, (g + 1) * gr)
            _softmax_rows(s_ref, p_ref, a_ref, m_ref, l_ref, SOFTMAX_ROWS, mask_fn, static_rows=(g * gr, (g + 1) * gr))
            acc_ref[sl, :] = (acc_ref[sl, :] * jnp.tile(a_ref[sl, :], (1, acc_ref.shape[1] // LANES))
                              + _dot(p_ref[sl, :], kc))

    needed = j * bk <= i * bq + bq - 1
    crosses_diagonal = j * bk + bk - 1 > i * bq

    @pl.when(needed & crosses_diagonal)
    def _():
        step(causal)

    @pl.when(needed & jnp.logical_not(crosses_diagonal))
    def _():
        step(None)

    @pl.when(j == pl.num_programs(1) - 1)
    def _():
        o = (acc_ref[...] / l_ref[...][:, :1]).astype(BF16)
        for h in range(heads):
            o_ref[:, h * vd:(h + 1) * vd] = _dot(o[h * bq:(h + 1) * bq], wv_ref[h]).astype(o_ref.dtype)


def _attn_prompt(ql, qp, kc, kp, wv, n_prompt):
    heads, _, kv = ql.shape
    rope = qp.shape[-1]
    vd = wv.shape[-1]
    bq = _tile(n_prompt, ATTN_BQ, SOFTMAX_ROWS)
    bk = _tile(n_prompt, ATTN_BK, LANES)
    nq, nk = n_prompt // bq, n_prompt // bk
    rows = heads * bq

    def kmap(i, j):
        return (jnp.minimum(j, (i * bq + bq - 1) // bk), 0)

    return pl.pallas_call(
        functools.partial(_attn_prompt_kernel, heads=heads, bq=bq, bk=bk, vd=vd),
        grid=(nq, nk),
        in_specs=[pl.BlockSpec((heads, bq, kv), lambda i, j: (0, i, 0)),
                  pl.BlockSpec((heads, bq, rope), lambda i, j: (0, i, 0)),
                  pl.BlockSpec((bk, kv), kmap),
                  pl.BlockSpec((bk, rope), kmap),
                  pl.BlockSpec(wv.shape, lambda i, j: (0, 0, 0))],
        out_specs=pl.BlockSpec((bq, heads * vd), lambda i, j: (i, 0)),
        out_shape=jax.ShapeDtypeStruct((n_prompt, heads * vd), BF16),
        scratch_shapes=[pltpu.VMEM((rows, LANES), F32), pltpu.VMEM((rows, LANES), F32),
                        pltpu.VMEM((rows, kv), F32), pltpu.VMEM((rows, bk), F32),
                        pltpu.VMEM((rows, bk), BF16), pltpu.VMEM((rows, LANES), F32)],
        compiler_params=_params("parallel", "arbitrary"),
        name="attn_prompt",
    )(ql, qp, kc, kp, wv)


def _attn_sample_kernel(pt_ref, ql_ref, qp_ref, kcn_ref, kpn_ref, wv_ref, ckv_hbm, kpe_hbm, o_ref,
                        ckv_buf, kpe_buf, sem, kc_s, kp_s, m_ref, l_ref, acc_ref, s_ref, p_ref, a_ref,
                        *, heads, lq, seqs, pages, page, vd):
    b = pl.program_id(0)
    j = pl.program_id(1)
    steps = pl.num_programs(1)
    g = b * steps + j
    total = pl.num_programs(0) * steps
    n_slots = ckv_buf.shape[0]
    ahead = n_slots - 1
    slot = g % n_slots

    def page_copies(gg, to_slot):
        bb, jj = gg // steps, gg % steps
        copies = []
        for e in range(seqs):
            for r in range(pages):
                pg = pt_ref[bb * seqs + e, jj * pages + r]
                k = e * pages + r
                copies.append(pltpu.make_async_copy(ckv_hbm.at[pg], ckv_buf.at[to_slot, k], sem.at[to_slot]))
                copies.append(pltpu.make_async_copy(kpe_hbm.at[pg], kpe_buf.at[to_slot, k], sem.at[to_slot]))
        return copies

    @pl.when(g == 0)
    def _():
        for first in range(ahead):
            @pl.when(first < total)
            def _():
                for cp in page_copies(first, first):
                    cp.start()

    @pl.when(g + ahead < total)
    def _():
        for cp in page_copies(g + ahead, (g + ahead) % n_slots):
            cp.start()

    for cp in page_copies(g, slot):
        cp.wait()
    ckv_pages = [ckv_buf.at[slot, k] for k in range(seqs * pages)]
    kpe_pages = [kpe_buf.at[slot, k] for k in range(seqs * pages)]

    @pl.when(j == 0)
    def _():
        for e in range(seqs):
            kcn = kcn_ref[e]
            s = _dot_nt(ql_ref[e].astype(F32), kcn) + _dot_nt(qp_ref[e].astype(F32), kpn_ref[e])
            t_q = lax.broadcasted_iota(jnp.int32, s.shape, 0) % lq
            t_k = lax.broadcasted_iota(jnp.int32, s.shape, 1)
            s = jnp.where(t_k <= t_q, s, NEG_INF)
            m0 = jnp.max(s, axis=-1, keepdims=True)
            p = jnp.exp(s - m0)
            m_ref[e] = jnp.broadcast_to(m0, m_ref.shape[1:])
            l_ref[e] = jnp.broadcast_to(jnp.sum(p, axis=-1, keepdims=True), l_ref.shape[1:])
            acc_ref[e] = _dot(p, kcn)

    for e in range(seqs):
        for r in range(pages):
            kc_s[e, r * page:(r + 1) * page, :] = ckv_pages[e * pages + r][...].astype(BF16)
            kp_s[e, :, r * page:(r + 1) * page] = kpe_pages[e * pages + r][...].astype(BF16)
        s_ref[e] = _dot_nt(ql_ref[e], kc_s[e]) + _dot(qp_ref[e], kp_s[e])
    for e in range(seqs):
        _softmax_rows(s_ref.at[e], p_ref.at[e], a_ref.at[e], m_ref.at[e], l_ref.at[e], 2 * SUBLANES,
                      static_rows=(0, s_ref.shape[1]))
        _rescale_add(acc_ref.at[e], a_ref.at[e], _dot(p_ref[e], kc_s[e]))

    @pl.when(j == pl.num_programs(1) - 1)
    def _():
        for e in range(seqs):
            o = (acc_ref[e] / l_ref[e][:, :1]).astype(BF16)
            for h in range(heads):
                full = _dot(o, wv_ref[h])
                o_ref[e, :, h * vd:(h + 1) * vd] = full[h * lq:(h + 1) * lq].astype(o_ref.dtype)


def _attn_sample(page_table, ql, qp, kc_new, kp_new, wv, cache_ckv, cache_kpe_t):
    nb, rows, kv = ql.shape
    rope = qp.shape[-1]
    heads, _, vd = wv.shape
    lq = rows // heads
    n_pages = page_table.shape[1]
    page = cache_ckv.shape[1]
    seqs = math.gcd(nb, SAMPLE_SEQS_PER_STEP)
    pages = _tile(n_pages, SAMPLE_PAGE_OPERANDS // seqs, 1)
    steps = n_pages // pages

    per_seq = lambda *shape: pl.BlockSpec((seqs,) + shape, lambda b, j, pt: (b, 0, 0))
    in_hbm = pl.BlockSpec(memory_space=pl.ANY)
    keys = pages * page
    n_ops = seqs * pages
    grid_spec = pltpu.PrefetchScalarGridSpec(
        num_scalar_prefetch=1,
        grid=(nb // seqs, steps),
        in_specs=[per_seq(rows, kv), per_seq(rows, rope), per_seq(lq, kv), per_seq(lq, rope),
                  pl.BlockSpec(wv.shape, lambda b, j, pt: (0, 0, 0)), in_hbm, in_hbm],
        out_specs=per_seq(lq, heads * vd),
        scratch_shapes=[pltpu.VMEM((SAMPLE_PAGE_SLOTS, n_ops, page, kv), cache_ckv.dtype),
                        pltpu.VMEM((SAMPLE_PAGE_SLOTS, n_ops, rope, page), cache_kpe_t.dtype),
                        pltpu.SemaphoreType.DMA((SAMPLE_PAGE_SLOTS,)),
                        pltpu.VMEM((seqs, keys, kv), BF16), pltpu.VMEM((seqs, rope, keys), BF16),
                        pltpu.VMEM((seqs, rows, LANES), F32), pltpu.VMEM((seqs, rows, LANES), F32),
                        pltpu.VMEM((seqs, rows, kv), F32), pltpu.VMEM((seqs, rows, keys), F32),
                        pltpu.VMEM((seqs, rows, keys), BF16), pltpu.VMEM((seqs, rows, LANES), F32)],
    )
    return pl.pallas_call(
        functools.partial(_attn_sample_kernel, heads=heads, lq=lq, seqs=seqs, pages=pages, page=page, vd=vd),
        grid_spec=grid_spec,
        out_shape=jax.ShapeDtypeStruct((nb, lq, heads * vd), BF16),
        compiler_params=_params("arbitrary", "arbitrary"),
        name="attn_sample",
    )(page_table, ql, qp, kc_new, kp_new, wv, cache_ckv, cache_kpe_t)


def _gather_rows_kernel(tok_ref, bv_ref, x_hbm, o_ref, buf, sem, *, tb, per_blk):
    i = pl.program_id(0)
    n = pl.num_programs(0)
    slot = i % 2

    def live(ii):
        return bv_ref[ii // per_blk] > (ii % per_blk) * tb

    def row_copy(ii, r, to_slot):
        tok = tok_ref[ii * tb + r]
        return pltpu.make_async_copy(x_hbm.at[pl.ds(tok, 1), :], buf.at[to_slot, pl.ds(r, 1), :], sem.at[to_slot])

    def for_rows(fn):
        def body(r, carry):
            fn(r)
            return carry
        lax.fori_loop(0, tb, body, 0, unroll=8)

    @pl.when((i == 0) & live(0))
    def _():
        for_rows(lambda r: row_copy(0, r, 0).start())

    nxt = jnp.minimum(i + 1, n - 1)

    @pl.when((i + 1 < n) & live(nxt))
    def _():
        for_rows(lambda r: row_copy(nxt, r, 1 - slot).start())

    @pl.when(live(i))
    def _():
        for_rows(lambda r: row_copy(i, r, slot).wait())
        o_ref[...] = buf[slot].astype(o_ref.dtype)

    @pl.when(jnp.logical_not(live(i)))
    def _():
        o_ref[...] = jnp.zeros(o_ref.shape, o_ref.dtype)


def _gather_rows(x, tok_of_slot, blk_valid, tm, dtype):
    d = x.shape[1]
    n_slots = tok_of_slot.shape[0]
    tb = _tile(tm, 256, 16)
    grid_spec = pltpu.PrefetchScalarGridSpec(
        num_scalar_prefetch=2,
        grid=(n_slots // tb,),
        in_specs=[pl.BlockSpec(memory_space=pl.ANY)],
        out_specs=pl.BlockSpec((tb, d), lambda i, tok, bv: (i, 0)),
        scratch_shapes=[pltpu.VMEM((2, tb, d), x.dtype), pltpu.SemaphoreType.DMA((2,))],
    )
    return pl.pallas_call(
        functools.partial(_gather_rows_kernel, tb=tb, per_blk=tm // tb),
        grid_spec=grid_spec,
        out_shape=jax.ShapeDtypeStruct((n_slots, d), dtype),
        compiler_params=_params("arbitrary"),
        name="gather_rows",
    )(tok_of_slot, blk_valid, x)


def _ffn_kernel(be_ref, bv_ref, x_ref, wg_ref, wu_ref, wd_ref, gate_ref, o_ref, wg_s, wu_s, wd_s, *, tm, ts):
    i = pl.program_id(0)
    j = pl.program_id(1)
    valid = bv_ref[i]

    @pl.when(j == 0)
    def _():
        o_ref[...] = jnp.zeros(o_ref.shape, F32)

    @pl.when(valid > 0)
    def _():
        wg_s[...] = wg_ref[0].astype(BF16)
        wu_s[...] = wu_ref[0].astype(BF16)
        wd_s[...] = wd_ref[0].astype(BF16)

    def swiglu(rows):
        x = x_ref[rows, :]
        h = (jax.nn.silu(_dot(x, wg_s[...])) * _dot(x, wu_s[...])).astype(BF16)
        o_ref[rows, :] += _dot(h, wd_s[...])

    @pl.when(valid == tm)
    def _():
        chunk = tm // math.gcd(tm // 16, FFN_FULL_BLOCK_CHUNKS)
        for c in range(tm // chunk):
            swiglu(slice(c * chunk, (c + 1) * chunk))

    for sub in range(tm // ts):
        @pl.when((valid > sub * ts) & (valid < tm))
        def _():
            swiglu(slice(sub * ts, (sub + 1) * ts))

    @pl.when(j == pl.num_programs(1) - 1)
    def _():
        o_ref[...] = o_ref[...] * gate_ref[...]


def _ffn(x, wg, wu, wd, gate, blk_expert, blk_valid, *, tm, tf_pref=512, ts_pref=256):
    s, d = x.shape
    f = wg.shape[2]
    tf = _tile(f, tf_pref, LANES)
    ts = _tile(tm, ts_pref, 16)
    nj = f // tf

    def jj(i, j, bv):
        return jnp.where(bv[i] > 0, j, nj - 1)

    once = pl.Buffered(1)
    scratch = [pltpu.VMEM((d, tf), BF16), pltpu.VMEM((d, tf), BF16), pltpu.VMEM((tf, d), BF16)]
    grid_spec = pltpu.PrefetchScalarGridSpec(
        num_scalar_prefetch=2,
        grid=(s // tm, nj),
        in_specs=[pl.BlockSpec((tm, d), lambda i, j, be, bv: (i, 0), pipeline_mode=once),
                  pl.BlockSpec((1, d, tf), lambda i, j, be, bv: (be[i], 0, jj(i, j, bv))),
                  pl.BlockSpec((1, d, tf), lambda i, j, be, bv: (be[i], 0, jj(i, j, bv))),
                  pl.BlockSpec((1, tf, d), lambda i, j, be, bv: (be[i], jj(i, j, bv), 0)),
                  pl.BlockSpec((tm, 1), lambda i, j, be, bv: (i, 0), pipeline_mode=once)],
        out_specs=pl.BlockSpec((tm, d), lambda i, j, be, bv: (i, 0), pipeline_mode=once),
        scratch_shapes=scratch,
    )
    return pl.pallas_call(
        functools.partial(_ffn_kernel, tm=tm, ts=ts),
        grid_spec=grid_spec,
        out_shape=jax.ShapeDtypeStruct((s, d), F32),
        compiler_params=_params("arbitrary", "arbitrary"),
        name="ffn",
    )(blk_expert, blk_valid, x, wg, wu, wd, gate)


def _split3(a):
    a1 = a.astype(BF16)
    r1 = a - a1.astype(F32)
    a2 = r1.astype(BF16)
    a3 = (r1 - a2.astype(F32)).astype(BF16)
    return a1, a2, a3


def _router_kernel(h_ref, d_ref, g_ref, wr_ref, hs_ref, hn_ref, comb_ref, sel_ref):
    x = h_ref[...] + d_ref[...]
    hs_ref[...] = x
    y = _rms(x, g_ref[...])
    hn_ref[...] = y
    ys = _split3(y)
    ws = _split3(wr_ref[...])
    logits = jnp.zeros((x.shape[0], wr_ref.shape[1]), F32)
    for a in range(3):
        for b in range(3 - a):
            logits = logits + _dot(ys[a], ws[b])
    n_e = logits.shape[1]
    lane = lax.broadcasted_iota(jnp.int32, logits.shape, 1).astype(F32)
    m1 = jnp.max(logits, axis=-1, keepdims=True)
    i1 = jnp.min(jnp.where(logits == m1, lane, n_e), axis=-1, keepdims=True)
    first = lane == i1
    rest = jnp.where(first, NEG_INF, logits)
    m2 = jnp.max(rest, axis=-1, keepdims=True)
    i2 = jnp.min(jnp.where(rest == m2, lane, n_e), axis=-1, keepdims=True)
    second = lane == i2
    e = jnp.exp(m2 - m1)
    den = 1.0 + e
    comb_ref[...] = jnp.where(first, 1.0 / den, 0.0) + jnp.where(second, e / den, 0.0)
    sel_ref[...] = jnp.where(first | second, 1.0, 0.0)


def _router(h, delta, gain, w_router):
    t, d = h.shape
    n_e = w_router.shape[1]
    tm = _tile(t, 256)
    row = pl.BlockSpec((tm, d), lambda i: (i, 0))
    small = pl.BlockSpec((tm, n_e), lambda i: (i, 0))
    return pl.pallas_call(
        _router_kernel,
        grid=(t // tm,),
        in_specs=[row, row, pl.BlockSpec((1, d), lambda i: (0, 0)), pl.BlockSpec((d, n_e), lambda i: (0, 0))],
        out_specs=[row, row, small, small],
        out_shape=[jax.ShapeDtypeStruct((t, d), F32), jax.ShapeDtypeStruct((t, d), F32),
                   jax.ShapeDtypeStruct((t, n_e), F32), jax.ShapeDtypeStruct((t, n_e), F32)],
        compiler_params=_params("parallel"),
        name="router",
    )(h, delta, gain.reshape(1, d), w_router)


def _roll_rows(x, shift):
    n = x.shape[0]
    shift = shift % n
    return x if shift == 0 else pltpu.roll(x, shift, 0)


def _hgrn_inputs(zq, zf, lb):
    q = jax.nn.silu(zq)
    lf = jnp.log(lb + (1.0 - lb) * jax.nn.sigmoid(zf))
    k = (1.0 - lb) * jax.nn.sigmoid(-zf)
    return q, k, lf


def _prefix8(lf, r8):
    p = lf
    for sh in (1, 2, 4):
        p = p + jnp.where(r8 >= sh, _roll_rows(p, sh), 0.0)
    return p


def _block8_intra(q, k, v, p8, r8):
    o = jnp.sum(q * k, axis=-1, keepdims=True) * v
    for d in range(1, SUBLANES):
        e = jnp.exp(jnp.where(r8 >= d, p8 - _roll_rows(p8, d), NEG_INF))
        w = jnp.sum(q * _roll_rows(k, d) * e, axis=-1, keepdims=True)
        o = o + w * _roll_rows(v, d)
    return o


def _hgrn_finish(o, zg, gw):
    return _rms(o, gw) * jax.nn.silu(zg)


def _hgrn_prompt_kernel(zq_ref, zf_ref, zi_ref, zg_ref, lb_ref, gw_ref, o_ref, st_ref, st_t, *, c, n_chunks, hp, kd):
    @pl.when(pl.program_id(1) == 0)
    def _():
        st_t[...] = jnp.zeros(st_t.shape, F32)

    gw = gw_ref[...]
    row = lax.broadcasted_iota(jnp.int32, (c, kd), 0)
    r8 = row % SUBLANES
    ri = lax.broadcasted_iota(jnp.int32, (c, c), 0)
    ci = lax.broadcasted_iota(jnp.int32, (c, c), 1)

    def one_head(rows, hh):
        cols = slice(hh * kd, (hh + 1) * kd)
        q, k, lf = _hgrn_inputs(zq_ref[rows, cols], zf_ref[rows, cols], lb_ref[:, cols])
        v = zi_ref[rows, cols]
        p = _prefix8(lf, r8)
        o = _block8_intra(q, k, v, p, r8)
        tot = jnp.where(r8 == SUBLANES - 1, p, 0.0)
        for sh in (1, 2, 4):
            tot = tot + _roll_rows(tot, -sh)
        a = jnp.zeros((c, c), F32)
        s = SUBLANES
        while s < c:
            second = row % (2 * s) >= s
            qs = (q * jnp.exp(jnp.where(second, p, NEG_INF))).astype(BF16)
            ks = (k * jnp.exp(jnp.where(second, NEG_INF, tot - p))).astype(BF16)
            a_s = _dot_nt(qs, ks)
            a = a + (a_s if 2 * s == c else jnp.where(ri // (2 * s) == ci // (2 * s), a_s, 0.0))
            prev = _roll_rows(tot, s)
            p = p + jnp.where(second, prev, 0.0)
            tot = tot + jnp.where(second, prev, _roll_rows(tot, -s))
            s *= 2
        st = st_t[hh]
        o = o + _dot(a.astype(BF16), v.astype(BF16)) + _dot_nt((q * jnp.exp(p)).astype(BF16), st.astype(BF16))
        kb = (k * jnp.exp(tot - p)).astype(BF16)
        st_t[hh] = st * jnp.exp(tot[0:1, :]) + _dot(v.T.astype(BF16), kb)
        o_ref[rows, cols] = _hgrn_finish(o, zg_ref[rows, cols], gw).astype(o_ref.dtype)

    def chunk(n, carry):
        rows = pl.ds(pl.multiple_of(n * c, c), c)
        for hh in range(hp):
            one_head(rows, hh)
        return carry

    lax.fori_loop(0, n_chunks, chunk, 0)

    @pl.when(pl.program_id(1) == pl.num_programs(1) - 1)
    def _():
        for hh in range(hp):
            st_ref[hh] = st_t[hh].T


def _hgrn_prompt(z, lb, gw, n_prompt, heads):
    kd = lb.shape[1] // heads
    c = HGRN_CHUNK
    hp = math.gcd(heads, HGRN_HEADS_PER_STEP)
    lbk = _tile(n_prompt, 1024, c)
    groups = heads // hp

    def zspec(seg):
        return pl.BlockSpec((lbk, hp * kd), lambda h, l: (l, seg * groups + h))

    return pl.pallas_call(
        functools.partial(_hgrn_prompt_kernel, c=c, n_chunks=lbk // c, hp=hp, kd=kd),
        grid=(groups, n_prompt // lbk),
        in_specs=[zspec(0), zspec(1), zspec(2), zspec(3),
                  pl.BlockSpec((1, hp * kd), lambda h, l: (0, h)),
                  pl.BlockSpec((1, kd), lambda h, l: (0, 0))],
        out_specs=[pl.BlockSpec((lbk, hp * kd), lambda h, l: (l, h)),
                   pl.BlockSpec((hp, kd, kd), lambda h, l: (h, 0, 0))],
        out_shape=[jax.ShapeDtypeStruct((n_prompt, heads * kd), BF16),
                   jax.ShapeDtypeStruct((heads, kd, kd), F32)],
        scratch_shapes=[pltpu.VMEM((hp, kd, kd), F32)],
        compiler_params=_params("parallel", "arbitrary"),
        name="hgrn_prompt",
    )(z, z, z, z, lb, gw)


def _hgrn_sample_kernel(z_ref, s0_ref, lb_ref, gw_ref, o_ref, s1_ref, *, heads, kd):
    lq = z_ref.shape[0]
    gw = gw_ref[...]
    r8 = lax.broadcasted_iota(jnp.int32, (lq, kd), 0)
    for h in range(heads):
        col = lambda seg: slice((seg * heads + h) * kd, (seg * heads + h + 1) * kd)
        q, k, lf = _hgrn_inputs(z_ref[:, col(0)], z_ref[:, col(1)], lb_ref[:, h * kd:(h + 1) * kd])
        v = z_ref[:, col(2)]
        p = _prefix8(lf, r8)
        last = p[lq - 1:lq, :]
        st = s0_ref[0, h]
        o = _block8_intra(q, k, v, p, r8) + _dot((q * jnp.exp(p)).astype(BF16), st.astype(BF16))
        cols = jnp.concatenate([k * jnp.exp(last - p), p], axis=0).T
        v_pad = jnp.concatenate([v, jnp.zeros_like(v)], axis=0)
        s1_ref[0, h] = st * jnp.exp(cols[:, 2 * lq - 1:2 * lq]) + _dot(cols.astype(BF16), v_pad.astype(BF16))
        o_ref[:, h * kd:(h + 1) * kd] = _hgrn_finish(o, z_ref[:, col(3)], gw).astype(o_ref.dtype)


def _hgrn_sample(z, state, lb, gw, row0):
    nb, heads, kd, _ = state.shape
    lq = SUBLANES
    blk0 = row0 // lq
    return pl.pallas_call(
        functools.partial(_hgrn_sample_kernel, heads=heads, kd=kd),
        grid=(nb,),
        in_specs=[pl.BlockSpec((lq, z.shape[1]), lambda b: (blk0 + b, 0)),
                  pl.BlockSpec((1, heads, kd, kd), lambda b: (b, 0, 0, 0)),
                  pl.BlockSpec((1, heads * kd), lambda b: (0, 0)),
                  pl.BlockSpec((1, kd), lambda b: (0, 0))],
        out_specs=[pl.BlockSpec((lq, heads * kd), lambda b: (b, 0)),
                   pl.BlockSpec((1, heads, kd, kd), lambda b: (b, 0, 0, 0))],
        out_shape=[jax.ShapeDtypeStruct((nb * lq, heads * kd), F32),
                   jax.ShapeDtypeStruct(state.shape, F32)],
        compiler_params=_params("parallel"),
        name="hgrn_sample",
    )(z, state, lb, gw)


def _route_plan(sel, comb, tm):
    t, n_e = sel.shape
    n_blk = -(-(TOP_K * t) // tm) + n_e
    n_slot = n_blk * tm
    picked = sel > 0
    seli = picked.astype(jnp.int32)
    incl = jnp.cumsum(seli, axis=0)
    counts = incl[-1]
    blocks_per = (counts + tm - 1) // tm
    blk_end = jnp.cumsum(blocks_per)
    start = (blk_end - blocks_per) * tm
    slot = start[None, :] + incl - seli
    slot_a = jnp.min(jnp.where(picked, slot, n_slot), axis=1)
    slot_b = jnp.max(jnp.where(picked, slot, -1), axis=1)
    tok_slots = jnp.stack([slot_a, slot_b], axis=1)
    gates = jnp.stack([jnp.sum(jnp.where(picked & (slot == s[:, None]), comb, 0.0), axis=1)
                       for s in (slot_a, slot_b)], axis=1).reshape(-1)
    pair = jnp.arange(TOP_K * t, dtype=jnp.int32)
    pair_of_slot = jnp.full((n_slot,), -1, jnp.int32).at[tok_slots.reshape(-1)].set(pair, unique_indices=True)
    filled = pair_of_slot >= 0
    safe = jnp.maximum(pair_of_slot, 0)
    tok_of_slot = safe // TOP_K
    gate_of_slot = jnp.where(filled, gates.at[safe].get(mode="promise_in_bounds"), 0.0)
    blk = jnp.arange(n_blk, dtype=jnp.int32)
    used = blk_end[-1]
    e_of_blk = jnp.sum((jnp.minimum(blk, used - 1)[:, None] >= blk_end[None, :]).astype(jnp.int32), axis=1)
    e_of_blk = jnp.minimum(e_of_blk, n_e - 1)
    valid = jnp.clip(counts[e_of_blk] - (blk * tm - start[e_of_blk]), 0, tm)
    valid = jnp.where(blk < used, valid, 0).astype(jnp.int32)
    return tok_of_slot, gate_of_slot, e_of_blk, valid, tok_slots


def kernel(x_prompt, x_sample, cache_ckv, cache_kpe, state_hgrn, page_table, norm_mix0, w_in_ab, ln_v_w, ln_v_b, w_s, b_s, q_norm_w, w_q_b, kv_norm_w, w_kv_b, w_out_ab, norm_ffn0, w_ffn_gate, w_ffn_up, w_ffn_down, norm_mix1, w_in_c, lb_logits, g_norm_w, w_out_c, norm_ffn1, w_router, w_exp_gate, w_exp_up, w_exp_down, norm_final):
    bp, lp, d = x_prompt.shape
    nb, lq, _ = x_sample.shape
    assert bp == 1, "the prompt group is one sequence"
    n_p, n_s = bp * lp, nb * lq
    t = n_p + n_s
    groups, chunk, _ = w_s.shape
    a_w = ln_v_w.shape[0]
    q_lora, heads, qk = w_q_b.shape
    kv_lora = kv_norm_w.shape[0]
    rope = cache_kpe.shape[-1]
    nope = qk - rope
    vd = w_kv_b.shape[-1] - nope
    page = cache_ckv.shape[1]
    past = page_table.shape[1] * page
    c_heads, c_k = state_hgrn.shape[1], state_hgrn.shape[2]
    c_f = c_heads * c_k
    assert lq == SUBLANES and chunk % lq == 0 and n_p % chunk == 0 and n_s % chunk == 0
    off_q, off_kv = 2 * a_w, 2 * a_w + q_lora
    scale = float(qk) ** -0.5

    x = jnp.concatenate([x_prompt.reshape(n_p, d), x_sample.reshape(n_s, d)], axis=0)

    pos = jnp.concatenate([jnp.arange(lp, dtype=F32), jnp.tile(past + jnp.arange(lq, dtype=F32), nb)])
    inv_freq = ROPE_THETA ** (-jnp.arange(rope // 2, dtype=F32) / (rope // 2))
    ang = pos[:, None] * inv_freq[None, :]
    cos, sin = jnp.cos(ang), jnp.sin(ang)

    hn = _add_norm([x], norm_mix0, want_sum=False, out_dtype=BF16)[0]
    (u,) = _mm(hn, w_in_ab[:, :a_w], _ep_gelu, [(a_w, BF16)], name="in_u")
    (v,) = _mm(hn, w_in_ab[:, a_w:off_q], _ep_gelu_ln, [(a_w, F32)],
               [(ln_v_w.reshape(1, a_w), "col"), (ln_v_b.reshape(1, a_w), "col")], name="in_v")
    (cq,) = _mm(hn, w_in_ab[:, off_q:off_kv], _ep_rms, [(q_lora, BF16)],
                [(q_norm_w.reshape(1, q_lora), "col")], name="in_cq")
    kvw = kv_lora + rope
    w_kvpe = w_in_ab[:, off_kv:]
    t_kv = _tile(t, 512, 16)
    ckv, kpe, ckv_b, kpe_b = pl.pallas_call(
        functools.partial(_mm_kernel, n_extra=3, n_out=4, epilogue=functools.partial(_ep_kv, kv_lora=kv_lora)),
        grid=(1, t // t_kv),
        in_specs=[pl.BlockSpec((t_kv, d), lambda j, i: (i, 0)),
                  pl.BlockSpec((d, kvw), lambda j, i: (0, 0)),
                  pl.BlockSpec((t_kv, rope // 2), lambda j, i: (i, 0)),
                  pl.BlockSpec((t_kv, rope // 2), lambda j, i: (i, 0)),
                  pl.BlockSpec((1, kv_lora), lambda j, i: (0, 0))],
        out_specs=[pl.BlockSpec((t_kv, kv_lora), lambda j, i: (i, 0)),
                   pl.BlockSpec((t_kv, rope), lambda j, i: (i, 0)),
                   pl.BlockSpec((t_kv, kv_lora), lambda j, i: (i, 0)),
                   pl.BlockSpec((t_kv, rope), lambda j, i: (i, 0))],
        out_shape=[jax.ShapeDtypeStruct((t, kv_lora), F32), jax.ShapeDtypeStruct((t, rope), F32),
                   jax.ShapeDtypeStruct((t, kv_lora), BF16), jax.ShapeDtypeStruct((t, rope), BF16)],
        scratch_shapes=[pltpu.VMEM((d, kvw), BF16)],
        compiler_params=_params("arbitrary", "arbitrary"),
        name="in_kv",
    )(hn, w_kvpe, cos, sin, kv_norm_w.reshape(1, kv_lora))

    wqn = w_q_b[:, :, :nope].reshape(q_lora, heads * nope).astype(BF16)
    wqr = w_q_b[:, :, nope:].reshape(q_lora, heads * rope).astype(BF16)
    wabs = jnp.transpose(w_kv_b[:, :, :nope], (1, 2, 0)).astype(BF16)
    wv = jnp.transpose(w_kv_b[:, :, nope:], (1, 0, 2)).astype(BF16)
    ql, qp = _q_proj(cq, wqn, wqr, wabs, cos, sin, scale)

    o_prompt = _attn_prompt(ql, qp, ckv_b, kpe_b, wv, n_p)
    to_rows = lambda a: jnp.transpose(a[:, n_p:].reshape(heads, nb, lq, a.shape[-1]), (1, 0, 2, 3)).reshape(
        nb, heads * lq, a.shape[-1])
    o_sample = _attn_sample(page_table, to_rows(ql), to_rows(qp),
                            ckv[n_p:].reshape(nb, lq, kv_lora), kpe[n_p:].reshape(nb, lq, rope),
                            wv, cache_ckv, jnp.swapaxes(cache_kpe, 1, 2))

    tril = jnp.tril(jnp.ones((chunk, chunk), dtype=bool))
    m_prompt = jnp.where(tril[None], w_s, 0.0)
    eye = jnp.eye(chunk // lq, dtype=F32)
    m_sample = jnp.einsum("ab,gts->gatbs", eye, m_prompt[:, :lq, :lq]).reshape(groups, chunk, chunk)
    mats = jnp.stack([m_prompt, m_sample]).astype(BF16)
    bias_p = jnp.repeat(b_s.T, a_w // groups, axis=1)
    bias = jnp.stack([bias_p, jnp.tile(bias_p[:lq], (chunk // lq, 1))])
    a_mix = _gmlp(u, v, mats, bias, n_p // chunk)

    mix = jnp.concatenate([a_mix, jnp.concatenate([o_prompt, o_sample.reshape(n_s, heads * vd)], axis=0)], axis=1)
    (h1,) = _mm(mix, w_out_ab, _ep_add, [(_tile(d, 1024, LANES), F32)], [(x, "tile")], name="out_ab")

    hn = _add_norm([h1], norm_ffn0, want_sum=False, out_dtype=BF16)[0]
    tm_f = DENSE_BLOCK_ROWS if t % DENSE_BLOCK_ROWS == 0 else _tile(t, 1024, 16)
    nblk = t // tm_f
    y_ffn = _ffn(hn, w_ffn_gate[None], w_ffn_up[None], w_ffn_down[None], jnp.ones((t, 1), F32),
                 jnp.zeros((nblk,), jnp.int32), jnp.full((nblk,), tm_f, jnp.int32), tm=tm_f)

    h2, hn = _add_norm([h1, y_ffn], norm_mix1, want_sum=True, out_dtype=BF16)
    (z,) = _mm(hn, w_in_c, _ep_id, [(_tile(w_in_c.shape[1], 1024, LANES), F32)], name="in_c")
    lb_cum = jnp.cumsum(jax.nn.softmax(lb_logits.astype(F32), axis=0), axis=0)
    lb = (lb_cum[1] - lb_cum[0]).reshape(1, c_f)
    gw = g_norm_w.reshape(1, -1)
    o_p, state_prompt = _hgrn_prompt(z, lb, gw, n_p, c_heads)
    o_s, state_sample = _hgrn_sample(z, state_hgrn, lb, gw, n_p)
    o_c = jnp.concatenate([o_p, o_s.astype(BF16)], axis=0)
    (d3,) = _mm(o_c, w_out_c, _ep_id, [(_tile(d, 1024, LANES), F32)], name="out_c")

    h3, hn_f32, comb, sel = _router(h2, d3, norm_ffn1, w_router)
    tm_e = MOE_BLOCK_ROWS if TOP_K * t >= 4 * MOE_BLOCK_ROWS else _tile(t, 1024, 16)
    tok_of_slot, gate_of_slot, e_of_blk, valid, tok_slots = _route_plan(sel, comb, tm_e)
    rows_of = lambda a, idx: a.at[idx].get(mode="promise_in_bounds")
    x_sorted = _gather_rows(hn_f32, tok_of_slot, valid, tm_e, BF16)
    y_sorted = _ffn(x_sorted, w_exp_gate, w_exp_up, w_exp_down,
                    gate_of_slot[:, None], e_of_blk, valid, tm=tm_e)
    parts = [h3] + [rows_of(y_sorted, tok_slots[:, r]) for r in range(TOP_K)]
    y_prompt, y_sample = _add_norm(parts, norm_final, want_sum=False, out_dtype=F32, split_rows=n_p)
    return (y_prompt.reshape(bp, lp, d), y_sample.reshape(nb, lq, d),
            ckv[:n_p].reshape(bp, lp, kv_lora), kpe[:n_p].reshape(bp, lp, rope),
            ckv[n_p:].reshape(nb, lq, kv_lora), kpe[n_p:].reshape(nb, lq, rope),
            v[n_p:].reshape(nb, lq, a_w),
            state_prompt.reshape(bp, c_heads, c_k, -1), state_sample)
```

```python
import functools
import math

import jax
import jax.numpy as jnp
from jax import lax
from jax.experimental import pallas as pl
from jax.experimental.pallas import tpu as pltpu

F32 = jnp.float32
BF16 = jnp.bfloat16
EPS = 1e-6
ROPE_THETA = 10000.0
TOP_K = 2
VMEM_LIMIT_BYTES = 56 * 1024 * 1024
LANES = 128
SUBLANES = 8
HGRN_CHUNK = 64
HGRN_HEADS_PER_STEP = 4
ATTN_BQ = 512
ATTN_BK = 512
ATTN_ROW_GROUPS = 4
SAMPLE_SEQS_PER_STEP = 2
SAMPLE_PAGE_OPERANDS = 16
SAMPLE_PAGE_SLOTS = 3
FFN_FULL_BLOCK_CHUNKS = 2
DMA_PRIORITIES = 2
DENSE_BLOCK_ROWS = 1152
MOE_BLOCK_ROWS = 1280
SOFTMAX_ROWS = 32
SOFTMAX_UNROLL = 8
NEG_INF = float("-inf")


def _params(*sem):
    return pltpu.CompilerParams(dimension_semantics=sem, vmem_limit_bytes=VMEM_LIMIT_BYTES)


def _tile(n, pref, mult=SUBLANES):
    if n <= pref:
        return n
    for t in range(pref, 0, -1):
        if n % t == 0 and t % mult == 0:
            return t
    raise ValueError(f"no tile for {n} <= {pref}")


def _dot(a, b):
    return jnp.dot(a, b, preferred_element_type=F32)


def _dot_nt(a, b):
    return lax.dot_general(a, b, (((1,), (1,)), ((), ())), preferred_element_type=F32)


def _rms(x, w):
    return x * lax.rsqrt(jnp.mean(x * x, axis=-1, keepdims=True) + EPS) * w


def _rope_rows(x, cos, sin):
    half = x.shape[-1] // 2
    x1, x2 = x[:, :half], x[:, half:]
    return jnp.concatenate([x1 * cos - x2 * sin, x1 * sin + x2 * cos], axis=-1)


def _add_norm_kernel(*refs, n_in, want_sum, first_blocks):
    x = refs[0][...]
    for r in refs[1:n_in]:
        x = x + r[...]
    g_ref = refs[n_in]
    outs = refs[n_in + 1:]
    k = 0
    if want_sum:
        outs[0][...] = x
        k = 1
    y = _rms(x, g_ref[...]).astype(outs[k].dtype)
    if first_blocks is None:
        outs[k][...] = y
    else:
        @pl.when(pl.program_id(0) < first_blocks)
        def _():
            outs[k][...] = y

        @pl.when(pl.program_id(0) >= first_blocks)
        def _():
            outs[k + 1][...] = y


def _add_norm(parts, gain, *, want_sum, out_dtype, split_rows=None):
    t, d = parts[0].shape
    tm = _tile(t if split_rows is None else math.gcd(split_rows, t - split_rows), 256)
    row = pl.BlockSpec((tm, d), lambda i: (i, 0))
    out_shape, out_specs = [], []
    if want_sum:
        out_shape.append(jax.ShapeDtypeStruct((t, d), F32))
        out_specs.append(row)
    first_blocks = None
    if split_rows is None:
        out_shape.append(jax.ShapeDtypeStruct((t, d), out_dtype))
        out_specs.append(row)
    else:
        assert split_rows % tm == 0 and 0 < split_rows < t
        first_blocks = split_rows // tm
        out_shape += [jax.ShapeDtypeStruct((split_rows, d), out_dtype),
                      jax.ShapeDtypeStruct((t - split_rows, d), out_dtype)]
        out_specs += [pl.BlockSpec((tm, d), lambda i: (jnp.minimum(i, first_blocks - 1), 0)),
                      pl.BlockSpec((tm, d), lambda i: (jnp.maximum(i - first_blocks, 0), 0))]
    return pl.pallas_call(
        functools.partial(_add_norm_kernel, n_in=len(parts), want_sum=want_sum, first_blocks=first_blocks),
        grid=(t // tm,),
        in_specs=[row] * len(parts) + [pl.BlockSpec((1, d), lambda i: (0, 0))],
        out_specs=out_specs,
        out_shape=out_shape,
        compiler_params=_params("arbitrary" if split_rows is not None else "parallel"),
        name="add_norm",
    )(*parts, gain.reshape(1, d))


def _mm_kernel(x_ref, w_ref, *rest, n_extra, n_out, epilogue):
    extras = rest[:n_extra]
    outs = rest[n_extra:n_extra + n_out]
    w_bf = rest[n_extra + n_out]

    @pl.when(pl.program_id(1) == 0)
    def _():
        w_bf[...] = w_ref[...].astype(BF16)

    acc = _dot(x_ref[...], w_bf[...])
    res = epilogue(acc, *[e[...] for e in extras])
    for o, r in zip(outs, res):
        o[...] = r.astype(o.dtype)


def _mm(x, w, epilogue, outs, extras=(), *, tm_pref=1024, tn_pref=1024, name="mm"):
    t, k = x.shape
    n = w.shape[1]
    tm = _tile(t, tm_pref, 16)
    tn = _tile(n, tn_pref, LANES)
    nj = n // tn
    in_specs = [pl.BlockSpec((tm, k), lambda j, i: (i, 0)),
                pl.BlockSpec((k, tn), lambda j, i: (0, j))]
    args = [x, w]
    for arr, kind in extras:
        if kind == "row":
            in_specs.append(pl.BlockSpec((tm, arr.shape[1]), lambda j, i: (i, 0)))
        elif kind == "col":
            in_specs.append(pl.BlockSpec((1, tn), lambda j, i: (0, j)))
        elif kind == "tile":
            in_specs.append(pl.BlockSpec((tm, tn), lambda j, i: (i, j)))
        else:
            raise ValueError(kind)
        args.append(arr)
    out_specs = [pl.BlockSpec((tm, wd), lambda j, i: (i, j)) for wd, _ in outs]
    out_shape = [jax.ShapeDtypeStruct((t, nj * wd), dt) for wd, dt in outs]
    res = pl.pallas_call(
        functools.partial(_mm_kernel, n_extra=len(extras), n_out=len(outs), epilogue=epilogue),
        grid=(nj, t // tm),
        in_specs=in_specs,
        out_specs=out_specs,
        out_shape=out_shape,
        scratch_shapes=[pltpu.VMEM((k, tn), BF16)],
        compiler_params=_params("arbitrary", "arbitrary"),
        name=name,
    )(*args)
    return res


def _ep_gelu(acc):
    return (jax.nn.gelu(acc),)


def _ep_gelu_ln(acc, w, b):
    g = jax.nn.gelu(acc)
    mu = jnp.mean(g, axis=-1, keepdims=True)
    c = g - mu
    var = jnp.mean(c * c, axis=-1, keepdims=True)
    return (c * lax.rsqrt(var + EPS) * w + b,)


def _ep_rms(acc, w):
    return (_rms(acc, w),)


def _ep_kv(acc, cos, sin, w, *, kv_lora):
    ckv = _rms(acc[:, :kv_lora], w)
    kpe = _rope_rows(acc[:, kv_lora:], cos, sin)
    return ckv, kpe, ckv, kpe


def _ep_add(acc, resid):
    return (acc + resid,)


def _ep_id(acc):
    return (acc,)


def _q_proj_kernel(cq_ref, wqn_ref, wqr_ref, wabs_ref, cos_ref, sin_ref, ql_ref, qp_ref, *, heads, nope, rope, scale):
    cq = cq_ref[...]
    qn = _dot(cq, wqn_ref[...])
    qr = _dot(cq, wqr_ref[...])
    cos = cos_ref[...]
    sin = sin_ref[...]
    for h in range(heads):
        lat = _dot(qn[:, h * nope:(h + 1) * nope].astype(BF16), wabs_ref[h])
        ql_ref[h] = (lat * scale).astype(ql_ref.dtype)
        qp_ref[h] = (_rope_rows(qr[:, h * rope:(h + 1) * rope], cos, sin) * scale).astype(qp_ref.dtype)


def _q_proj(cq, wqn, wqr, wabs, cos, sin, scale):
    t, ql = cq.shape
    heads, nope, kv = wabs.shape
    rope = wqr.shape[1] // heads
    tm = _tile(t, 512, 16)
    const2 = lambda i: (0, 0)
    return pl.pallas_call(
        functools.partial(_q_proj_kernel, heads=heads, nope=nope, rope=rope, scale=scale),
        grid=(t // tm,),
        in_specs=[pl.BlockSpec((tm, ql), lambda i: (i, 0)),
                  pl.BlockSpec(wqn.shape, const2),
                  pl.BlockSpec(wqr.shape, const2),
                  pl.BlockSpec(wabs.shape, lambda i: (0, 0, 0)),
                  pl.BlockSpec((tm, rope // 2), lambda i: (i, 0)),
                  pl.BlockSpec((tm, rope // 2), lambda i: (i, 0))],
        out_specs=[pl.BlockSpec((heads, tm, kv), lambda i: (0, i, 0)),
                   pl.BlockSpec((heads, tm, rope), lambda i: (0, i, 0))],
        out_shape=[jax.ShapeDtypeStruct((heads, t, kv), BF16),
                   jax.ShapeDtypeStruct((heads, t, rope), BF16)],
        compiler_params=_params("parallel"),
        name="q_proj",
    )(cq, wqn, wqr, wabs, cos, sin)


def _gmlp_kernel(u_ref, v_ref, m_ref, b_ref, a_ref, *, groups, gd):
    for g in range(groups):
        sl = slice(g * gd, (g + 1) * gd)
        mixed = _dot(m_ref[0, g], v_ref[:, sl].astype(BF16)) + b_ref[0, :, sl]
        a_ref[:, sl] = (u_ref[:, sl].astype(F32) * mixed).astype(a_ref.dtype)


def _gmlp(u, v, mats, bias, n_first):
    t, a = u.shape
    _, groups, c, _ = mats.shape
    sel = lambda n: (jnp.where(n < n_first, 0, 1), 0, 0)
    return pl.pallas_call(
        functools.partial(_gmlp_kernel, groups=groups, gd=a // groups),
        grid=(t // c,),
        in_specs=[pl.BlockSpec((c, a), lambda n: (n, 0)),
                  pl.BlockSpec((c, a), lambda n: (n, 0)),
                  pl.BlockSpec((1, groups, c, c), lambda n: (jnp.where(n < n_first, 0, 1), 0, 0, 0)),
                  pl.BlockSpec((1, c, a), sel)],
        out_specs=pl.BlockSpec((c, a), lambda n: (n, 0)),
        out_shape=jax.ShapeDtypeStruct((t, a), BF16),
        compiler_params=_params("parallel"),
        name="gmlp",
    )(u, v, mats, bias)


def _softmax_rows(s_ref, p_ref, a_ref, m_ref, l_ref, rc, mask_fn=None, static_rows=None):
    width = s_ref.shape[1]

    def chunk(rs, row0):
        s = s_ref[rs, :]
        if mask_fn is not None:
            s = mask_fn(s, row0)
        m_prev = m_ref[rs, :]
        m_new = jnp.maximum(m_prev, jnp.max(s, axis=-1, keepdims=True))
        alpha = jnp.exp(m_prev - m_new)
        p = jnp.exp(s - jnp.tile(m_new, (1, width // LANES)))
        l_ref[rs, :] = alpha * l_ref[rs, :] + jnp.sum(p, axis=-1, keepdims=True)
        m_ref[rs, :] = m_new
        a_ref[rs, :] = alpha
        p_ref[rs, :] = p.astype(p_ref.dtype)

    if static_rows is not None:
        for row0 in range(static_rows[0], static_rows[1], rc):
            chunk(slice(row0, row0 + rc), row0)
        return

    def body(r, carry):
        chunk(pl.ds(pl.multiple_of(r * rc, rc), rc), r * rc)
        return carry

    n = s_ref.shape[0] // rc
    lax.fori_loop(0, n, body, 0, unroll=math.gcd(n, SOFTMAX_UNROLL))


def _rescale_add(acc_ref, a_ref, pv):
    acc_ref[...] = acc_ref[...] * jnp.tile(a_ref[...], (1, acc_ref.shape[1] // LANES)) + pv


def _attn_prompt_kernel(ql_ref, qp_ref, kc_ref, kp_ref, wv_ref, o_ref, m_ref, l_ref, acc_ref, s_ref, p_ref, a_ref,
                        *, heads, bq, bk, vd):
    i = pl.program_id(0)
    j = pl.program_id(1)
    rows = heads * bq

    @pl.when(j == 0)
    def _():
        m_ref[...] = jnp.full(m_ref.shape, NEG_INF, F32)
        l_ref[...] = jnp.zeros(l_ref.shape, F32)
        acc_ref[...] = jnp.zeros(acc_ref.shape, F32)

    def causal(s, row0):
        q_pos = i * bq + (row0 + lax.broadcasted_iota(jnp.int32, s.shape, 0)) % bq
        k_pos = j * bk + lax.broadcasted_iota(jnp.int32, s.shape, 1)
        return jnp.where(k_pos <= q_pos, s, NEG_INF)

    def step(mask_fn):
        hg = heads // math.gcd(heads, ATTN_ROW_GROUPS)
        gr = hg * bq
        kc = kc_ref[...]
        for g in range(heads // hg):
            q = ql_ref[g * hg:(g + 1) * hg].reshape(gr, ql_ref.shape[-1])
            qp = qp_ref[g * hg:(g + 1) * hg].reshape(gr, qp_ref.shape[-1])
            s_ref[g * gr:(g + 1) * gr, :] = _dot_nt(q, kc) + _dot_nt(qp, kp_ref[...])
        for g in range(heads // hg):
            sl = slice(g * gr, (g + 1) * gr)
            _softmax_rows(s_ref, p_ref, a_ref, m_ref, l_ref, SOFTMAX_ROWS, mask_fn, static_rows=(g * gr, (g + 1) * gr))
            acc_ref[sl, :] = (acc_ref[sl, :] * jnp.tile(a_ref[sl, :], (1, acc_ref.shape[1] // LANES))
                              + _dot(p_ref[sl, :], kc))

    needed = j * bk <= i * bq + bq - 1
    crosses_diagonal = j * bk + bk - 1 > i * bq

    @pl.when(needed & crosses_diagonal)
    def _():
        step(causal)

    @pl.when(needed & jnp.logical_not(crosses_diagonal))
    def _():
        step(None)

    @pl.when(j == pl.num_programs(1) - 1)
    def _():
        o = (acc_ref[...] / l_ref[...][:, :1]).astype(BF16)
        for h in range(heads):
            o_ref[:, h * vd:(h + 1) * vd] = _dot(o[h * bq:(h + 1) * bq], wv_ref[h]).astype(o_ref.dtype)


def _attn_prompt(ql, qp, kc, kp, wv, n_prompt):
    heads, _, kv = ql.shape
    rope = qp.shape[-1]
    vd = wv.shape[-1]
    bq = _tile(n_prompt, ATTN_BQ, SOFTMAX_ROWS)
    bk = _tile(n_prompt, ATTN_BK, LANES)
    nq, nk = n_prompt // bq, n_prompt // bk
    rows = heads * bq

    def kmap(i, j):
        return (jnp.minimum(j, (i * bq + bq - 1) // bk), 0)

    return pl.pallas_call(
        functools.partial(_attn_prompt_kernel, heads=heads, bq=bq, bk=bk, vd=vd),
        grid=(nq, nk),
        in_specs=[pl.BlockSpec((heads, bq, kv), lambda i, j: (0, i, 0)),
                  pl.BlockSpec((heads, bq, rope), lambda i, j: (0, i, 0)),
                  pl.BlockSpec((bk, kv), kmap),
                  pl.BlockSpec((bk, rope), kmap),
                  pl.BlockSpec(wv.shape, lambda i, j: (0, 0, 0))],
        out_specs=pl.BlockSpec((bq, heads * vd), lambda i, j: (i, 0)),
        out_shape=jax.ShapeDtypeStruct((n_prompt, heads * vd), BF16),
        scratch_shapes=[pltpu.VMEM((rows, LANES), F32), pltpu.VMEM((rows, LANES), F32),
                        pltpu.VMEM((rows, kv), F32), pltpu.VMEM((rows, bk), F32),
                        pltpu.VMEM((rows, bk), BF16), pltpu.VMEM((rows, LANES), F32)],
        compiler_params=_params("parallel", "arbitrary"),
        name="attn_prompt",
    )(ql, qp, kc, kp, wv)


def _attn_sample_kernel(pt_ref, ql_ref, qp_ref, kcn_ref, kpn_ref, wv_ref, ckv_hbm, kpe_hbm, o_ref,
                        ckv_buf, kpe_buf, sem, kc_s, kp_s, m_ref, l_ref, acc_ref, s_ref, p_ref, a_ref,
                        *, heads, lq, seqs, pages, page, vd):
    b = pl.program_id(0)
    j = pl.program_id(1)
    steps = pl.num_programs(1)
    g = b * steps + j
    total = pl.num_programs(0) * steps
    n_slots = ckv_buf.shape[0]
    ahead = n_slots - 1
    slot = g % n_slots

    def page_copies(gg, to_slot):
        bb, jj = gg // steps, gg % steps
        copies = []
        for e in range(seqs):
            for r in range(pages):
                pg = pt_ref[bb * seqs + e, jj * pages + r]
                k = e * pages + r
                copies.append(pltpu.make_async_copy(ckv_hbm.at[pg], ckv_buf.at[to_slot, k], sem.at[to_slot]))
                copies.append(pltpu.make_async_copy(kpe_hbm.at[pg], kpe_buf.at[to_slot, k], sem.at[to_slot]))
        return copies

    @pl.when(g == 0)
    def _():
        for first in range(ahead):
            @pl.when(first < total)
            def _():
                for cp in page_copies(first, first):
                    cp.start()

    @pl.when(g + ahead < total)
    def _():
        for cp in page_copies(g + ahead, (g + ahead) % n_slots):
            cp.start()

    for cp in page_copies(g, slot):
        cp.wait()
    ckv_pages = [ckv_buf.at[slot, k] for k in range(seqs * pages)]
    kpe_pages = [kpe_buf.at[slot, k] for k in range(seqs * pages)]

    @pl.when(j == 0)
    def _():
        for e in range(seqs):
            kcn = kcn_ref[e]
            s = _dot_nt(ql_ref[e].astype(F32), kcn) + _dot_nt(qp_ref[e].astype(F32), kpn_ref[e])
            t_q = lax.broadcasted_iota(jnp.int32, s.shape, 0) % lq
            t_k = lax.broadcasted_iota(jnp.int32, s.shape, 1)
            s = jnp.where(t_k <= t_q, s, NEG_INF)
            m0 = jnp.max(s, axis=-1, keepdims=True)
            p = jnp.exp(s - m0)
            m_ref[e] = jnp.broadcast_to(m0, m_ref.shape[1:])
            l_ref[e] = jnp.broadcast_to(jnp.sum(p, axis=-1, keepdims=True), l_ref.shape[1:])
            acc_ref[e] = _dot(p, kcn)

    for e in range(seqs):
        for r in range(pages):
            kc_s[e, r * page:(r + 1) * page, :] = ckv_pages[e * pages + r][...].astype(BF16)
            kp_s[e, :, r * page:(r + 1) * page] = kpe_pages[e * pages + r][...].astype(BF16)
        s_ref[e] = _dot_nt(ql_ref[e], kc_s[e]) + _dot(qp_ref[e], kp_s[e])
    for e in range(seqs):
        _softmax_rows(s_ref.at[e], p_ref.at[e], a_ref.at[e], m_ref.at[e], l_ref.at[e], 2 * SUBLANES,
                      static_rows=(0, s_ref.shape[1]))
        _rescale_add(acc_ref.at[e], a_ref.at[e], _dot(p_ref[e], kc_s[e]))

    @pl.when(j == pl.num_programs(1) - 1)
    def _():
        for e in range(seqs):
            o = (acc_ref[e] / l_ref[e][:, :1]).astype(BF16)
            for h in range(heads):
                full = _dot(o, wv_ref[h])
                o_ref[e, :, h * vd:(h + 1) * vd] = full[h * lq:(h + 1) * lq].astype(o_ref.dtype)


def _attn_sample(page_table, ql, qp, kc_new, kp_new, wv, cache_ckv, cache_kpe_t):
    nb, rows, kv = ql.shape
    rope = qp.shape[-1]
    heads, _, vd = wv.shape
    lq = rows // heads
    n_pages = page_table.shape[1]
    page = cache_ckv.shape[1]
    seqs = math.gcd(nb, SAMPLE_SEQS_PER_STEP)
    pages = _tile(n_pages, SAMPLE_PAGE_OPERANDS // seqs, 1)
    steps = n_pages // pages

    per_seq = lambda *shape: pl.BlockSpec((seqs,) + shape, lambda b, j, pt: (b, 0, 0))
    in_hbm = pl.BlockSpec(memory_space=pl.ANY)
    keys = pages * page
    n_ops = seqs * pages
    grid_spec = pltpu.PrefetchScalarGridSpec(
        num_scalar_prefetch=1,
        grid=(nb // seqs, steps),
        in_specs=[per_seq(rows, kv), per_seq(rows, rope), per_seq(lq, kv), per_seq(lq, rope),
                  pl.BlockSpec(wv.shape, lambda b, j, pt: (0, 0, 0)), in_hbm, in_hbm],
        out_specs=per_seq(lq, heads * vd),
        scratch_shapes=[pltpu.VMEM((SAMPLE_PAGE_SLOTS, n_ops, page, kv), cache_ckv.dtype),
                        pltpu.VMEM((SAMPLE_PAGE_SLOTS, n_ops, rope, page), cache_kpe_t.dtype),
                        pltpu.SemaphoreType.DMA((SAMPLE_PAGE_SLOTS,)),
                        pltpu.VMEM((seqs, keys, kv), BF16), pltpu.VMEM((seqs, rope, keys), BF16),
                        pltpu.VMEM((seqs, rows, LANES), F32), pltpu.VMEM((seqs, rows, LANES), F32),
                        pltpu.VMEM((seqs, rows, kv), F32), pltpu.VMEM((seqs, rows, keys), F32),
                        pltpu.VMEM((seqs, rows, keys), BF16), pltpu.VMEM((seqs, rows, LANES), F32)],
    )
    return pl.pallas_call(
        functools.partial(_attn_sample_kernel, heads=heads, lq=lq, seqs=seqs, pages=pages, page=page, vd=vd),
        grid_spec=grid_spec,
        out_shape=jax.ShapeDtypeStruct((nb, lq, heads * vd), BF16),
        compiler_params=_params("arbitrary", "arbitrary"),
        name="attn_sample",
    )(page_table, ql, qp, kc_new, kp_new, wv, cache_ckv, cache_kpe_t)


def _gather_rows_kernel(tok_ref, bv_ref, x_hbm, o_ref, buf, sem, *, tb, per_blk):
    i = pl.program_id(0)
    n = pl.num_programs(0)
    slot = i % 2

    def live(ii):
        return bv_ref[ii // per_blk] > (ii % per_blk) * tb

    def row_copy(ii, r, to_slot):
        tok = tok_ref[ii * tb + r]
        return pltpu.make_async_copy(x_hbm.at[pl.ds(tok, 1), :], buf.at[to_slot, pl.ds(r, 1), :], sem.at[to_slot])

    def for_rows(fn):
        def body(r2, carry):
            for parity in range(DMA_PRIORITIES):
                fn(r2 * DMA_PRIORITIES + parity, parity)
            return carry
        lax.fori_loop(0, tb // DMA_PRIORITIES, body, 0, unroll=4)

    @pl.when((i == 0) & live(0))
    def _():
        for_rows(lambda r, prio: row_copy(0, r, 0).start(priority=prio))

    nxt = jnp.minimum(i + 1, n - 1)

    @pl.when((i + 1 < n) & live(nxt))
    def _():
        for_rows(lambda r, prio: row_copy(nxt, r, 1 - slot).start(priority=prio))

    @pl.when(live(i))
    def _():
        for_rows(lambda r, prio: row_copy(i, r, slot).wait())
        o_ref[...] = buf[slot].astype(o_ref.dtype)

    @pl.when(jnp.logical_not(live(i)))
    def _():
        o_ref[...] = jnp.zeros(o_ref.shape, o_ref.dtype)


def _gather_rows(x, tok_of_slot, blk_valid, tm, dtype):
    d = x.shape[1]
    n_slots = tok_of_slot.shape[0]
    tb = _tile(tm, 256, 16)
    grid_spec = pltpu.PrefetchScalarGridSpec(
        num_scalar_prefetch=2,
        grid=(n_slots // tb,),
        in_specs=[pl.BlockSpec(memory_space=pl.ANY)],
        out_specs=pl.BlockSpec((tb, d), lambda i, tok, bv: (i, 0)),
        scratch_shapes=[pltpu.VMEM((2, tb, d), x.dtype), pltpu.SemaphoreType.DMA((2,))],
    )
    return pl.pallas_call(
        functools.partial(_gather_rows_kernel, tb=tb, per_blk=tm // tb),
        grid_spec=grid_spec,
        out_shape=jax.ShapeDtypeStruct((n_slots, d), dtype),
        compiler_params=_params("arbitrary"),
        name="gather_rows",
    )(tok_of_slot, blk_valid, x)


def _ffn_kernel(be_ref, bv_ref, x_ref, wg_ref, wu_ref, wd_ref, gate_ref, o_ref, wg_s, wu_s, wd_s, *, tm, ts):
    i = pl.program_id(0)
    j = pl.program_id(1)
    valid = bv_ref[i]

    @pl.when(j == 0)
    def _():
        o_ref[...] = jnp.zeros(o_ref.shape, F32)

    @pl.when(valid > 0)
    def _():
        wg_s[...] = wg_ref[0].astype(BF16)
        wu_s[...] = wu_ref[0].astype(BF16)
        wd_s[...] = wd_ref[0].astype(BF16)

    def swiglu(rows):
        x = x_ref[rows, :]
        h = (jax.nn.silu(_dot(x, wg_s[...])) * _dot(x, wu_s[...])).astype(BF16)
        o_ref[rows, :] += _dot(h, wd_s[...])

    @pl.when(valid == tm)
    def _():
        chunk = tm // math.gcd(tm // 16, FFN_FULL_BLOCK_CHUNKS)
        for c in range(tm // chunk):
            swiglu(slice(c * chunk, (c + 1) * chunk))

    for sub in range(tm // ts):
        @pl.when((valid > sub * ts) & (valid < tm))
        def _():
            swiglu(slice(sub * ts, (sub + 1) * ts))

    @pl.when(j == pl.num_programs(1) - 1)
    def _():
        o_ref[...] = o_ref[...] * gate_ref[...]


def _ffn(x, wg, wu, wd, gate, blk_expert, blk_valid, *, tm, tf_pref=512, ts_pref=256):
    s, d = x.shape
    f = wg.shape[2]
    tf = _tile(f, tf_pref, LANES)
    ts = _tile(tm, ts_pref, 16)
    nj = f // tf

    def jj(i, j, bv):
        return jnp.where(bv[i] > 0, j, nj - 1)

    once = pl.Buffered(1)
    scratch = [pltpu.VMEM((d, tf), BF16), pltpu.VMEM((d, tf), BF16), pltpu.VMEM((tf, d), BF16)]
    grid_spec = pltpu.PrefetchScalarGridSpec(
        num_scalar_prefetch=2,
        grid=(s // tm, nj),
        in_specs=[pl.BlockSpec((tm, d), lambda i, j, be, bv: (i, 0), pipeline_mode=once),
                  pl.BlockSpec((1, d, tf), lambda i, j, be, bv: (be[i], 0, jj(i, j, bv))),
                  pl.BlockSpec((1, d, tf), lambda i, j, be, bv: (be[i], 0, jj(i, j, bv))),
                  pl.BlockSpec((1, tf, d), lambda i, j, be, bv: (be[i], jj(i, j, bv), 0)),
                  pl.BlockSpec((tm, 1), lambda i, j, be, bv: (i, 0), pipeline_mode=once)],
        out_specs=pl.BlockSpec((tm, d), lambda i, j, be, bv: (i, 0), pipeline_mode=once),
        scratch_shapes=scratch,
    )
    return pl.pallas_call(
        functools.partial(_ffn_kernel, tm=tm, ts=ts),
        grid_spec=grid_spec,
        out_shape=jax.ShapeDtypeStruct((s, d), F32),
        compiler_params=_params("arbitrary", "arbitrary"),
        name="ffn",
    )(blk_expert, blk_valid, x, wg, wu, wd, gate)


def _split3(a):
    a1 = a.astype(BF16)
    r1 = a - a1.astype(F32)
    a2 = r1.astype(BF16)
    a3 = (r1 - a2.astype(F32)).astype(BF16)
    return a1, a2, a3


def _router_kernel(h_ref, d_ref, g_ref, wr_ref, hs_ref, hn_ref, comb_ref, sel_ref):
    x = h_ref[...] + d_ref[...]
    hs_ref[...] = x
    y = _rms(x, g_ref[...])
    hn_ref[...] = y
    ys = _split3(y)
    ws = _split3(wr_ref[...])
    logits = jnp.zeros((x.shape[0], wr_ref.shape[1]), F32)
    for a in range(3):
        for b in range(3 - a):
            logits = logits + _dot(ys[a], ws[b])
    n_e = logits.shape[1]
    lane = lax.broadcasted_iota(jnp.int32, logits.shape, 1).astype(F32)
    m1 = jnp.max(logits, axis=-1, keepdims=True)
    i1 = jnp.min(jnp.where(logits == m1, lane, n_e), axis=-1, keepdims=True)
    first = lane == i1
    rest = jnp.where(first, NEG_INF, logits)
    m2 = jnp.max(rest, axis=-1, keepdims=True)
    i2 = jnp.min(jnp.where(rest == m2, lane, n_e), axis=-1, keepdims=True)
    second = lane == i2
    e = jnp.exp(m2 - m1)
    den = 1.0 + e
    comb_ref[...] = jnp.where(first, 1.0 / den, 0.0) + jnp.where(second, e / den, 0.0)
    sel_ref[...] = jnp.where(first | second, 1.0, 0.0)


def _router(h, delta, gain, w_router):
    t, d = h.shape
    n_e = w_router.shape[1]
    tm = _tile(t, 256)
    row = pl.BlockSpec((tm, d), lambda i: (i, 0))
    small = pl.BlockSpec((tm, n_e), lambda i: (i, 0))
    return pl.pallas_call(
        _router_kernel,
        grid=(t // tm,),
        in_specs=[row, row, pl.BlockSpec((1, d), lambda i: (0, 0)), pl.BlockSpec((d, n_e), lambda i: (0, 0))],
        out_specs=[row, row, small, small],
        out_shape=[jax.ShapeDtypeStruct((t, d), F32), jax.ShapeDtypeStruct((t, d), F32),
                   jax.ShapeDtypeStruct((t, n_e), F32), jax.ShapeDtypeStruct((t, n_e), F32)],
        compiler_params=_params("parallel"),
        name="router",
    )(h, delta, gain.reshape(1, d), w_router)


def _roll_rows(x, shift):
    n = x.shape[0]
    shift = shift % n
    return x if shift == 0 else pltpu.roll(x, shift, 0)


def _hgrn_inputs(zq, zf, lb):
    q = jax.nn.silu(zq)
    lf = jnp.log(lb + (1.0 - lb) * jax.nn.sigmoid(zf))
    k = (1.0 - lb) * jax.nn.sigmoid(-zf)
    return q, k, lf


def _prefix8(lf, r8):
    p = lf
    for sh in (1, 2, 4):
        p = p + jnp.where(r8 >= sh, _roll_rows(p, sh), 0.0)
    return p


def _block8_intra(q, k, v, p8, r8):
    o = jnp.sum(q * k, axis=-1, keepdims=True) * v
    for d in range(1, SUBLANES):
        e = jnp.exp(jnp.where(r8 >= d, p8 - _roll_rows(p8, d), NEG_INF))
        w = jnp.sum(q * _roll_rows(k, d) * e, axis=-1, keepdims=True)
        o = o + w * _roll_rows(v, d)
    return o


def _hgrn_finish(o, zg, gw):
    return _rms(o, gw) * jax.nn.silu(zg)


def _hgrn_prompt_kernel(zq_ref, zf_ref, zi_ref, zg_ref, lb_ref, gw_ref, o_ref, st_ref, st_t, *, c, n_chunks, hp, kd):
    @pl.when(pl.program_id(1) == 0)
    def _():
        st_t[...] = jnp.zeros(st_t.shape, F32)

    gw = gw_ref[...]
    row = lax.broadcasted_iota(jnp.int32, (c, kd), 0)
    r8 = row % SUBLANES
    ri = lax.broadcasted_iota(jnp.int32, (c, c), 0)
    ci = lax.broadcasted_iota(jnp.int32, (c, c), 1)

    def one_head(rows, hh):
        cols = slice(hh * kd, (hh + 1) * kd)
        q, k, lf = _hgrn_inputs(zq_ref[rows, cols], zf_ref[rows, cols], lb_ref[:, cols])
        v = zi_ref[rows, cols]
        p = _prefix8(lf, r8)
        o = _block8_intra(q, k, v, p, r8)
        tot = jnp.where(r8 == SUBLANES - 1, p, 0.0)
        for sh in (1, 2, 4):
            tot = tot + _roll_rows(tot, -sh)
        a = jnp.zeros((c, c), F32)
        s = SUBLANES
        while s < c:
            second = row % (2 * s) >= s
            qs = (q * jnp.exp(jnp.where(second, p, NEG_INF))).astype(BF16)
            ks = (k * jnp.exp(jnp.where(second, NEG_INF, tot - p))).astype(BF16)
            a_s = _dot_nt(qs, ks)
            a = a + (a_s if 2 * s == c else jnp.where(ri // (2 * s) == ci // (2 * s), a_s, 0.0))
            prev = _roll_rows(tot, s)
            p = p + jnp.where(second, prev, 0.0)
            tot = tot + jnp.where(second, prev, _roll_rows(tot, -s))
            s *= 2
        st = st_t[hh]
        o = o + _dot(a.astype(BF16), v.astype(BF16)) + _dot_nt((q * jnp.exp(p)).astype(BF16), st.astype(BF16))
        kb = (k * jnp.exp(tot - p)).astype(BF16)
        st_t[hh] = st * jnp.exp(tot[0:1, :]) + _dot(v.T.astype(BF16), kb)
        o_ref[rows, cols] = _hgrn_finish(o, zg_ref[rows, cols], gw).astype(o_ref.dtype)

    def chunk(n, carry):
        rows = pl.ds(pl.multiple_of(n * c, c), c)
        for hh in range(hp):
            one_head(rows, hh)
        return carry

    lax.fori_loop(0, n_chunks, chunk, 0)

    @pl.when(pl.program_id(1) == pl.num_programs(1) - 1)
    def _():
        for hh in range(hp):
            st_ref[hh] = st_t[hh].T


def _hgrn_prompt(z, lb, gw, n_prompt, heads):
    kd = lb.shape[1] // heads
    c = HGRN_CHUNK
    hp = math.gcd(heads, HGRN_HEADS_PER_STEP)
    lbk = _tile(n_prompt, 1024, c)
    groups = heads // hp

    def zspec(seg):
        return pl.BlockSpec((lbk, hp * kd), lambda h, l: (l, seg * groups + h))

    return pl.pallas_call(
        functools.partial(_hgrn_prompt_kernel, c=c, n_chunks=lbk // c, hp=hp, kd=kd),
        grid=(groups, n_prompt // lbk),
        in_specs=[zspec(0), zspec(1), zspec(2), zspec(3),
                  pl.BlockSpec((1, hp * kd), lambda h, l: (0, h)),
                  pl.BlockSpec((1, kd), lambda h, l: (0, 0))],
        out_specs=[pl.BlockSpec((lbk, hp * kd), lambda h, l: (l, h)),
                   pl.BlockSpec((hp, kd, kd), lambda h, l: (h, 0, 0))],
        out_shape=[jax.ShapeDtypeStruct((n_prompt, heads * kd), BF16),
                   jax.ShapeDtypeStruct((heads, kd, kd), F32)],
        scratch_shapes=[pltpu.VMEM((hp, kd, kd), F32)],
        compiler_params=_params("parallel", "arbitrary"),
        name="hgrn_prompt",
    )(z, z, z, z, lb, gw)


def _hgrn_sample_kernel(z_ref, s0_ref, lb_ref, gw_ref, o_ref, s1_ref, *, heads, kd):
    lq = z_ref.shape[0]
    gw = gw_ref[...]
    r8 = lax.broadcasted_iota(jnp.int32, (lq, kd), 0)
    for h in range(heads):
        col = lambda seg: slice((seg * heads + h) * kd, (seg * heads + h + 1) * kd)
        q, k, lf = _hgrn_inputs(z_ref[:, col(0)], z_ref[:, col(1)], lb_ref[:, h * kd:(h + 1) * kd])
        v = z_ref[:, col(2)]
        p = _prefix8(lf, r8)
        last = p[lq - 1:lq, :]
        st = s0_ref[0, h]
        o = _block8_intra(q, k, v, p, r8) + _dot((q * jnp.exp(p)).astype(BF16), st.astype(BF16))
        cols = jnp.concatenate([k * jnp.exp(last - p), p], axis=0).T
        v_pad = jnp.concatenate([v, jnp.zeros_like(v)], axis=0)
        s1_ref[0, h] = st * jnp.exp(cols[:, 2 * lq - 1:2 * lq]) + _dot(cols.astype(BF16), v_pad.astype(BF16))
        o_ref[:, h * kd:(h + 1) * kd] = _hgrn_finish(o, z_ref[:, col(3)], gw).astype(o_ref.dtype)


def _hgrn_sample(z, state, lb, gw, row0):
    nb, heads, kd, _ = state.shape
    lq = SUBLANES
    blk0 = row0 // lq
    return pl.pallas_call(
        functools.partial(_hgrn_sample_kernel, heads=heads, kd=kd),
        grid=(nb,),
        in_specs=[pl.BlockSpec((lq, z.shape[1]), lambda b: (blk0 + b, 0)),
                  pl.BlockSpec((1, heads, kd, kd), lambda b: (b, 0, 0, 0)),
                  pl.BlockSpec((1, heads * kd), lambda b: (0, 0)),
                  pl.BlockSpec((1, kd), lambda b: (0, 0))],
        out_specs=[pl.BlockSpec((lq, heads * kd), lambda b: (b, 0)),
                   pl.BlockSpec((1, heads, kd, kd), lambda b: (b, 0, 0, 0))],
        out_shape=[jax.ShapeDtypeStruct((nb * lq, heads * kd), F32),
                   jax.ShapeDtypeStruct(state.shape, F32)],
        compiler_params=_params("parallel"),
        name="hgrn_sample",
    )(z, state, lb, gw)


def _route_plan(sel, comb, tm):
    t, n_e = sel.shape
    n_blk = -(-(TOP_K * t) // tm) + n_e
    n_slot = n_blk * tm
    picked = sel > 0
    seli = picked.astype(jnp.int32)
    incl = jnp.cumsum(seli, axis=0)
    counts = incl[-1]
    blocks_per = (counts + tm - 1) // tm
    blk_end = jnp.cumsum(blocks_per)
    start = (blk_end - blocks_per) * tm
    slot = start[None, :] + incl - seli
    slot_a = jnp.min(jnp.where(picked, slot, n_slot), axis=1)
    slot_b = jnp.max(jnp.where(picked, slot, -1), axis=1)
    tok_slots = jnp.stack([slot_a, slot_b], axis=1)
    gates = jnp.stack([jnp.sum(jnp.where(picked & (slot == s[:, None]), comb, 0.0), axis=1)
                       for s in (slot_a, slot_b)], axis=1).reshape(-1)
    pair = jnp.arange(TOP_K * t, dtype=jnp.int32)
    pair_of_slot = jnp.full((n_slot,), -1, jnp.int32).at[tok_slots.reshape(-1)].set(pair, unique_indices=True)
    filled = pair_of_slot >= 0
    safe = jnp.maximum(pair_of_slot, 0)
    tok_of_slot = safe // TOP_K
    gate_of_slot = jnp.where(filled, gates.at[safe].get(mode="promise_in_bounds"), 0.0)
    blk = jnp.arange(n_blk, dtype=jnp.int32)
    used = blk_end[-1]
    e_of_blk = jnp.sum((jnp.minimum(blk, used - 1)[:, None] >= blk_end[None, :]).astype(jnp.int32), axis=1)
    e_of_blk = jnp.minimum(e_of_blk, n_e - 1)
    valid = jnp.clip(counts[e_of_blk] - (blk * tm - start[e_of_blk]), 0, tm)
    valid = jnp.where(blk < used, valid, 0).astype(jnp.int32)
    return tok_of_slot, gate_of_slot, e_of_blk, valid, tok_slots


def kernel(x_prompt, x_sample, cache_ckv, cache_kpe, state_hgrn, page_table, norm_mix0, w_in_ab, ln_v_w, ln_v_b, w_s, b_s, q_norm_w, w_q_b, kv_norm_w, w_kv_b, w_out_ab, norm_ffn0, w_ffn_gate, w_ffn_up, w_ffn_down, norm_mix1, w_in_c, lb_logits, g_norm_w, w_out_c, norm_ffn1, w_router, w_exp_gate, w_exp_up, w_exp_down, norm_final):
    bp, lp, d = x_prompt.shape
    nb, lq, _ = x_sample.shape
    assert bp == 1, "the prompt group is one sequence"
    n_p, n_s = bp * lp, nb * lq
    t = n_p + n_s
    groups, chunk, _ = w_s.shape
    a_w = ln_v_w.shape[0]
    q_lora, heads, qk = w_q_b.shape
    kv_lora = kv_norm_w.shape[0]
    rope = cache_kpe.shape[-1]
    nope = qk - rope
    vd = w_kv_b.shape[-1] - nope
    page = cache_ckv.shape[1]
    past = page_table.shape[1] * page
    c_heads, c_k = state_hgrn.shape[1], state_hgrn.shape[2]
    c_f = c_heads * c_k
    assert lq == SUBLANES and chunk % lq == 0 and n_p % chunk == 0 and n_s % chunk == 0
    off_q, off_kv = 2 * a_w, 2 * a_w + q_lora
    scale = float(qk) ** -0.5

    x = jnp.concatenate([x_prompt.reshape(n_p, d), x_sample.reshape(n_s, d)], axis=0)

    pos = jnp.concatenate([jnp.arange(lp, dtype=F32), jnp.tile(past + jnp.arange(lq, dtype=F32), nb)])
    inv_freq = ROPE_THETA ** (-jnp.arange(rope // 2, dtype=F32) / (rope // 2))
    ang = pos[:, None] * inv_freq[None, :]
    cos, sin = jnp.cos(ang), jnp.sin(ang)

    hn = _add_norm([x], norm_mix0, want_sum=False, out_dtype=BF16)[0]
    (u,) = _mm(hn, w_in_ab[:, :a_w], _ep_gelu, [(a_w, BF16)], name="in_u")
    (v,) = _mm(hn, w_in_ab[:, a_w:off_q], _ep_gelu_ln, [(a_w, F32)],
               [(ln_v_w.reshape(1, a_w), "col"), (ln_v_b.reshape(1, a_w), "col")], name="in_v")
    (cq,) = _mm(hn, w_in_ab[:, off_q:off_kv], _ep_rms, [(q_lora, BF16)],
                [(q_norm_w.reshape(1, q_lora), "col")], name="in_cq")
    kvw = kv_lora + rope
    w_kvpe = w_in_ab[:, off_kv:]
    t_kv = _tile(t, 512, 16)
    ckv, kpe, ckv_b, kpe_b = pl.pallas_call(
        functools.partial(_mm_kernel, n_extra=3, n_out=4, epilogue=functools.partial(_ep_kv, kv_lora=kv_lora)),
        grid=(1, t // t_kv),
        in_specs=[pl.BlockSpec((t_kv, d), lambda j, i: (i, 0)),
                  pl.BlockSpec((d, kvw), lambda j, i: (0, 0)),
                  pl.BlockSpec((t_kv, rope // 2), lambda j, i: (i, 0)),
                  pl.BlockSpec((t_kv, rope // 2), lambda j, i: (i, 0)),
                  pl.BlockSpec((1, kv_lora), lambda j, i: (0, 0))],
        out_specs=[pl.BlockSpec((t_kv, kv_lora), lambda j, i: (i, 0)),
                   pl.BlockSpec((t_kv, rope), lambda j, i: (i, 0)),
                   pl.BlockSpec((t_kv, kv_lora), lambda j, i: (i, 0)),
                   pl.BlockSpec((t_kv, rope), lambda j, i: (i, 0))],
        out_shape=[jax.ShapeDtypeStruct((t, kv_lora), F32), jax.ShapeDtypeStruct((t, rope), F32),
                   jax.ShapeDtypeStruct((t, kv_lora), BF16), jax.ShapeDtypeStruct((t, rope), BF16)],
        scratch_shapes=[pltpu.VMEM((d, kvw), BF16)],
        compiler_params=_params("arbitrary", "arbitrary"),
        name="in_kv",
    )(hn, w_kvpe, cos, sin, kv_norm_w.reshape(1, kv_lora))

    wqn = w_q_b[:, :, :nope].reshape(q_lora, heads * nope).astype(BF16)
    wqr = w_q_b[:, :, nope:].reshape(q_lora, heads * rope).astype(BF16)
    wabs = jnp.transpose(w_kv_b[:, :, :nope], (1, 2, 0)).astype(BF16)
    wv = jnp.transpose(w_kv_b[:, :, nope:], (1, 0, 2)).astype(BF16)
    ql, qp = _q_proj(cq, wqn, wqr, wabs, cos, sin, scale)

    o_prompt = _attn_prompt(ql, qp, ckv_b, kpe_b, wv, n_p)
    to_rows = lambda a: jnp.transpose(a[:, n_p:].reshape(heads, nb, lq, a.shape[-1]), (1, 0, 2, 3)).reshape(
        nb, heads * lq, a.shape[-1])
    o_sample = _attn_sample(page_table, to_rows(ql), to_rows(qp),
                            ckv[n_p:].reshape(nb, lq, kv_lora), kpe[n_p:].reshape(nb, lq, rope),
                            wv, cache_ckv, jnp.swapaxes(cache_kpe, 1, 2))

    tril = jnp.tril(jnp.ones((chunk, chunk), dtype=bool))
    m_prompt = jnp.where(tril[None], w_s, 0.0)
    eye = jnp.eye(chunk // lq, dtype=F32)
    m_sample = jnp.einsum("ab,gts->gatbs", eye, m_prompt[:, :lq, :lq]).reshape(groups, chunk, chunk)
    mats = jnp.stack([m_prompt, m_sample]).astype(BF16)
    bias_p = jnp.repeat(b_s.T, a_w // groups, axis=1)
    bias = jnp.stack([bias_p, jnp.tile(bias_p[:lq], (chunk // lq, 1))])
    a_mix = _gmlp(u, v, mats, bias, n_p // chunk)

    mix = jnp.concatenate([a_mix, jnp.concatenate([o_prompt, o_sample.reshape(n_s, heads * vd)], axis=0)], axis=1)
    (h1,) = _mm(mix, w_out_ab, _ep_add, [(_tile(d, 1024, LANES), F32)], [(x, "tile")], name="out_ab")

    hn = _add_norm([h1], norm_ffn0, want_sum=False, out_dtype=BF16)[0]
    tm_f = DENSE_BLOCK_ROWS if t % DENSE_BLOCK_ROWS == 0 else _tile(t, 1024, 16)
    nblk = t // tm_f
    y_ffn = _ffn(hn, w_ffn_gate[None], w_ffn_up[None], w_ffn_down[None], jnp.ones((t, 1), F32),
                 jnp.zeros((nblk,), jnp.int32), jnp.full((nblk,), tm_f, jnp.int32), tm=tm_f)

    h2, hn = _add_norm([h1, y_ffn], norm_mix1, want_sum=True, out_dtype=BF16)
    (z,) = _mm(hn, w_in_c, _ep_id, [(_tile(w_in_c.shape[1], 1024, LANES), F32)], name="in_c")
    lb_cum = jnp.cumsum(jax.nn.softmax(lb_logits.astype(F32), axis=0), axis=0)
    lb = (lb_cum[1] - lb_cum[0]).reshape(1, c_f)
    gw = g_norm_w.reshape(1, -1)
    o_p, state_prompt = _hgrn_prompt(z, lb, gw, n_p, c_heads)
    o_s, state_sample = _hgrn_sample(z, state_hgrn, lb, gw, n_p)
    o_c = jnp.concatenate([o_p, o_s.astype(BF16)], axis=0)
    (d3,) = _mm(o_c, w_out_c, _ep_id, [(_tile(d, 1024, LANES), F32)], name="out_c")

    h3, hn_f32, comb, sel = _router(h2, d3, norm_ffn1, w_router)
    tm_e = MOE_BLOCK_ROWS if TOP_K * t >= 4 * MOE_BLOCK_ROWS else _tile(t, 1024, 16)
    tok_of_slot, gate_of_slot, e_of_blk, valid, tok_slots = _route_plan(sel, comb, tm_e)
    rows_of = lambda a, idx: a.at[idx].get(mode="promise_in_bounds")
    x_sorted = _gather_rows(hn_f32, tok_of_slot, valid, tm_e, BF16)
    y_sorted = _ffn(x_sorted, w_exp_gate, w_exp_up, w_exp_down,
                    gate_of_slot[:, None], e_of_blk, valid, tm=tm_e)
    parts = [h3] + [rows_of(y_sorted, tok_slots[:, r]) for r in range(TOP_K)]
    y_prompt, y_sample = _add_norm(parts, norm_final, want_sum=False, out_dtype=F32, split_rows=n_p)
    return (y_prompt.reshape(bp, lp, d), y_sample.reshape(nb, lq, d),
            ckv[:n_p].reshape(bp, lp, kv_lora), kpe[:n_p].reshape(bp, lp, rope),
            ckv[n_p:].reshape(nb, lq, kv_lora), kpe[n_p:].reshape(nb, lq, rope),
            v[n_p:].reshape(nb, lq, a_w),
            state_prompt.reshape(bp, c_heads, c_k, -1), state_sample)
```
